```python
import math
import jax
import jax.numpy as jnp
from jax import lax
import numpy as np

D_MODEL = 4096
BATCH = 2
SEQ = 8192
DEPTH = 4
DEC_BATCH = 16
DEC_SEQ = 32
PAST_LEN = 1024

CHUNK = 64
QBLOCK = 128
N_PAIR = DEPTH // 2
A_HEADS = 32
A_KV_HEADS = 8
A_HEAD_DIM = 64
A_GROUP = A_HEADS // A_KV_HEADS
WINDOW = 128
WIN_CHUNKS = WINDOW // CHUNK
POOL_WIDTH = 2048
POOL_GROUPS = 4
POOL_GROUP_WIDTH = POOL_WIDTH // POOL_GROUPS
POOL_WINDOWS = (2, 4, 8, 16)
POOL_HIST = max(POOL_WINDOWS) - 1
C_HEADS = 16
Q_LORA = 1024
KV_LORA = 512
NOPE_DIM = 128
ROPE_DIM = 64
V_DIM = 128
ROPE_THETA = 10000.0
D_HEADS = 32
D_KV_HEADS = 8
D_HEAD_DIM = 64
D_GROUP = D_HEADS // D_KV_HEADS
IDX_HEADS = 32
IDX_DIM = 64
TOPK_MAX = 256
NUM_BUCKETS = 32
MAX_DISTANCE = 128
BIAS_HEADS = 32
N_GROUPS = 8
EXPERTS_PER_GROUP = 8
N_EXPERTS = N_GROUPS * EXPERTS_PER_GROUP
TOP_K = 2
D_EXPERT = 512
MOE_BLOCK = 64
RMS_EPS = 1e-6
NEG_INF = -1e30

EVEN_SPLITS = (A_HEADS * A_HEAD_DIM, A_KV_HEADS * A_HEAD_DIM, A_KV_HEADS * A_HEAD_DIM, POOL_WIDTH)
EVEN_IN = sum(EVEN_SPLITS)
EVEN_MIX = A_HEADS * A_HEAD_DIM + POOL_WIDTH
ODD_SPLITS = (Q_LORA, KV_LORA, ROPE_DIM, D_HEADS * D_HEAD_DIM, D_KV_HEADS * D_HEAD_DIM,
              D_KV_HEADS * D_HEAD_DIM, IDX_HEADS * IDX_DIM, IDX_DIM, IDX_HEADS)
ODD_IN = sum(ODD_SPLITS)
ODD_MIX = C_HEADS * V_DIM + D_HEADS * D_HEAD_DIM

kernel_name = 'hybrid_streaming_encoder_step'


def split_cols(z, sizes):
    out, o = [], 0
    for s in sizes:
        out.append(z[..., o:o + s])
        o += s
    return out


def rmsnorm(x, g):
    xf = x.astype(jnp.float32)
    y = xf * lax.rsqrt(jnp.mean(xf * xf, axis=-1, keepdims=True) + RMS_EPS)
    return (y * g.astype(jnp.float32)).astype(x.dtype)


def modulation(c, w, b):
    m = jax.nn.silu(c) @ w + b
    shift, scale, gate = jnp.split(m[:, None, :], 3, axis=-1)
    return shift, scale, gate


def t5_bucket(rel):
    half = NUM_BUCKETS // 2
    max_exact = half // 2
    ret = jnp.where(rel > 0, half, 0)
    n = jnp.abs(rel)
    nf = jnp.maximum(n, 1).astype(jnp.float32)
    large = max_exact + (jnp.log(nf / max_exact) / math.log(MAX_DISTANCE / max_exact)
                         * (half - max_exact)).astype(jnp.int32)
    large = jnp.minimum(large, half - 1)
    return ret + jnp.where(n < max_exact, n, large)


def head_bias(rel, table, n_kv, group):
    b = table[t5_bucket(rel)]
    return jnp.moveaxis(b, -1, 0).reshape((n_kv, group) + rel.shape)


def apply_rope(x, pos):
    inv = jnp.exp(-math.log(ROPE_THETA) * jnp.arange(0, ROPE_DIM, 2, dtype=jnp.float32) / ROPE_DIM)
    ang = pos.astype(jnp.float32)[:, None] * inv[None, :]
    ang = ang.reshape(ang.shape[:1] + (1,) * (x.ndim - 3) + ang.shape[1:])
    cos, sin = jnp.cos(ang), jnp.sin(ang)
    x1, x2 = jnp.split(x.astype(jnp.float32), 2, axis=-1)
    return jnp.concatenate([x1 * cos - x2 * sin, x2 * cos + x1 * sin], axis=-1).astype(x.dtype)


def sink_attention(q, k, v, bias, valid, sinks):
    n, tq = q.shape[:2]
    s = jnp.einsum('nqkgd,nskd->nkgqs', q, k).astype(jnp.float32) * A_HEAD_DIM ** -0.5 + bias.astype(jnp.float32)
    s = jnp.where(valid[:, None, None], s, NEG_INF)
    sink = sinks.astype(jnp.float32).reshape(1, A_KV_HEADS, A_GROUP, 1, 1)
    mx = jnp.maximum(s.max(-1, keepdims=True), sink)
    e = jnp.exp(s - mx)
    p = e / (e.sum(-1, keepdims=True) + jnp.exp(sink - mx))
    o = jnp.einsum('nkgqs,nskd->nqkgd', p.astype(v.dtype), v)
    return o.reshape(n, tq, A_HEADS * A_HEAD_DIM)


def window_attention_prompt(q, k, v, sinks, rel_bias):
    B, S = q.shape[:2]
    nc = S // CHUNK
    band = (WIN_CHUNKS + 1) * CHUNK

    def banded(t):
        tc = t.reshape(B, nc, CHUNK, A_KV_HEADS, A_HEAD_DIM)
        tp = jnp.concatenate([jnp.zeros_like(tc[:, :WIN_CHUNKS]), tc], axis=1)
        tb = jnp.concatenate([tp[:, j:j + nc] for j in range(WIN_CHUNKS + 1)], axis=2)
        return tb.reshape(B * nc, band, A_KV_HEADS, A_HEAD_DIM)

    qi = jnp.arange(CHUNK)[:, None]
    kr = jnp.arange(band)[None, :]
    rel = kr - WIN_CHUNKS * CHUNK - qi
    key_chunk = jnp.arange(nc)[:, None] - WIN_CHUNKS + kr // CHUNK
    valid = jnp.broadcast_to((key_chunk >= 0)[None, :, None, :], (B, nc, CHUNK, band)).reshape(B * nc, CHUNK, band)
    qb = q.reshape(B * nc, CHUNK, A_KV_HEADS, A_GROUP, A_HEAD_DIM)
    o = sink_attention(qb, banded(k), banded(v), head_bias(rel, rel_bias, A_KV_HEADS, A_GROUP), valid, sinks)
    return o.reshape(B, S, A_HEADS * A_HEAD_DIM)


def window_attention_sample(q, k_all, v_all, pos, W, sinks, rel_bias):
    B, T = q.shape[:2]
    k_pos = jnp.concatenate([PAST_LEN - W + jnp.arange(W), pos])
    qc = (pos // CHUNK)[:, None]
    kc = (k_pos // CHUNK)[None, :]
    valid = (kc <= qc) & (kc >= qc - WIN_CHUNKS)
    valid = jnp.broadcast_to(valid[None], (B,) + valid.shape)
    bias = head_bias(k_pos[None, :] - pos[:, None], rel_bias, A_KV_HEADS, A_GROUP)
    return sink_attention(q, k_all, v_all, bias, valid, sinks)


def pool_mix(u_ext, pos, n_hist, pool_w, pool_scale):
    B, L, P = u_ext.shape
    T = L - n_hist
    cs = jnp.cumsum(u_ext.astype(jnp.float32), axis=1)
    cs = jnp.concatenate([jnp.zeros_like(cs[:, :1]), cs], axis=1)
    end = n_hist + jnp.arange(T) + 1
    outs = []
    for g, w in enumerate(POOL_WINDOWS):
        lo, hi = g * POOL_GROUP_WIDTH, (g + 1) * POOL_GROUP_WIDTH
        csg = cs[..., lo:hi]
        start = jnp.maximum(end - w, 0)
        cnt = jnp.minimum(w, pos + 1).astype(jnp.float32)[None, :, None]
        mean = (csg[:, end] - csg[:, start]) / cnt
        outs.append(mean - u_ext[:, n_hist:, lo:hi].astype(jnp.float32))
    d = jnp.stack(outs, axis=2).astype(u_ext.dtype)
    y = jnp.einsum('btgc,gcd->btgd', d, pool_w).reshape(B, T, P)
    return y * pool_scale


def even_mixer(h, pos, w_in, w_out, sinks, pool_w, pool_scale, rel_bias, cache):
    B, T, _ = h.shape
    q, k, v, u = split_cols(h @ w_in, EVEN_SPLITS)
    q = q.reshape(B, T, A_KV_HEADS, A_GROUP, A_HEAD_DIM)
    k = k.reshape(B, T, A_KV_HEADS, A_HEAD_DIM)
    v = v.reshape(B, T, A_KV_HEADS, A_HEAD_DIM)
    if cache is None:
        a = window_attention_prompt(q, k, v, sinks, rel_bias)
        p = pool_mix(u, pos, 0, pool_w, pool_scale)
        new = (k[:, -WINDOW:], v[:, -WINDOW:], u[:, -POOL_HIST:])
    else:
        k_cache, v_cache, u_cache = cache
        W = k_cache.shape[1]
        k_all = jnp.concatenate([k_cache, k], axis=1)
        v_all = jnp.concatenate([v_cache, v], axis=1)
        a = window_attention_sample(q, k_all, v_all, pos, W, sinks, rel_bias)
        u_all = jnp.concatenate([u_cache, u], axis=1)
        p = pool_mix(u_all, pos, POOL_HIST, pool_w, pool_scale)
        new = (k_all[:, -W:], v_all[:, -W:], u_all[:, -POOL_HIST:])
    return jnp.concatenate([a, p], axis=-1) @ w_out, new


def mla_attend(q_nope, q_pe, k_nope, k_pe, v, q_pos):
    L = k_nope.shape[1]
    s = (jnp.einsum('bqhd,bshd->bhqs', q_nope, k_nope)
         + jnp.einsum('bqhr,bsr->bhqs', q_pe, k_pe)).astype(jnp.float32) * (NOPE_DIM + ROPE_DIM) ** -0.5
    ok = (jnp.arange(L) // CHUNK)[None, :] <= (q_pos // CHUNK)[:, None]
    p = jax.nn.softmax(jnp.where(ok, s, NEG_INF), axis=-1)
    o = jnp.einsum('bhqs,bshd->bqhd', p.astype(v.dtype), v)
    return o.reshape(o.shape[0], o.shape[1], C_HEADS * V_DIM)


def dsa_attend(q, qi, wi, k, v, ki, q_pos, n_sel, rel_bias):
    B, Tq = q.shape[:2]
    L = k.shape[1]
    sc = jnp.einsum('bqhd,bsd->bqhs', qi, ki).astype(jnp.float32) * IDX_DIM ** -0.5
    score = jnp.einsum('bqhs,bqh->bqs', jax.nn.relu(sc), wi.astype(jnp.float32) * IDX_HEADS ** -0.5)
    adm = (jnp.arange(L) // CHUNK)[None, :] <= (q_pos // CHUNK)[:, None]
    score = jnp.where(adm[None], score, -jnp.inf)
    _, idx = lax.top_k(score, n_sel)
    sel_ok = (idx // CHUNK) <= (q_pos // CHUNK)[None, :, None]
    gather = jax.vmap(lambda t, i: t[i])
    k_sel = gather(k, idx)
    v_sel = gather(v, idx)
    s = jnp.einsum('bqhgd,bqshd->bqhgs', q, k_sel).astype(jnp.float32) * D_HEAD_DIM ** -0.5
    bias = rel_bias[t5_bucket(idx - q_pos[None, :, None])]
    bias = bias.reshape(B, Tq, n_sel, D_KV_HEADS, D_GROUP).transpose(0, 1, 3, 4, 2)
    s = jnp.where(sel_ok[:, :, None, None, :], s + bias.astype(jnp.float32), NEG_INF)
    p = jax.nn.softmax(s, axis=-1)
    o = jnp.einsum('bqhgs,bqshd->bqhgd', p.astype(v.dtype), v_sel)
    return o.reshape(B, Tq, D_HEADS * D_HEAD_DIM)


def odd_mixer(h, pos, w_in, w_out, g_qa, g_kva, w_qb, w_kvb, rel_bias, cache):
    B, T, _ = h.shape
    q_lat, kv_lat, kpe_raw, dq, dk, dv, iq, ik, iw = split_cols(h @ w_in, ODD_SPLITS)
    q = (rmsnorm(q_lat, g_qa) @ w_qb).reshape(B, T, C_HEADS, NOPE_DIM + ROPE_DIM)
    q_nope = q[..., :NOPE_DIM]
    q_pe = apply_rope(q[..., NOPE_DIM:], pos)
    lat = rmsnorm(kv_lat, g_kva)
    kpe = apply_rope(kpe_raw, pos)
    dq = dq.reshape(B, T, D_KV_HEADS, D_GROUP, D_HEAD_DIM)
    dk = dk.reshape(B, T, D_KV_HEADS, D_HEAD_DIM)
    dv = dv.reshape(B, T, D_KV_HEADS, D_HEAD_DIM)
    iq = iq.reshape(B, T, IDX_HEADS, IDX_DIM)
    if cache is None:
        lat_all, kpe_all, dk_all, dv_all, ik_all = lat, kpe, dk, dv, ik
    else:
        c_lat, c_kpe, c_dk, c_dv, c_ik = cache
        lat_all = jnp.concatenate([c_lat, lat], axis=1)
        kpe_all = jnp.concatenate([c_kpe, kpe], axis=1)
        dk_all = jnp.concatenate([c_dk, dk], axis=1)
        dv_all = jnp.concatenate([c_dv, dv], axis=1)
        ik_all = jnp.concatenate([c_ik, ik], axis=1)
    L = lat_all.shape[1]
    kv = (lat_all @ w_kvb).reshape(B, L, C_HEADS, NOPE_DIM + V_DIM)
    k_nope, v_c = kv[..., :NOPE_DIM], kv[..., NOPE_DIM:]
    n_sel = min(TOPK_MAX, L // 4)
    if cache is None:
        nb = T // QBLOCK

        def blocks(t):
            return jnp.swapaxes(t.reshape((B, nb, QBLOCK) + t.shape[2:]), 0, 1)

        def body(args):
            qn, qp, dqb, iqb, iwb, qpos = args
            oc = mla_attend(qn, qp, k_nope, kpe_all, v_c, qpos)
            od = dsa_attend(dqb, iqb, iwb, dk_all, dv_all, ik_all, qpos, n_sel, rel_bias)
            return oc, od

        oc, od = lax.map(body, (blocks(q_nope), blocks(q_pe), blocks(dq), blocks(iq), blocks(iw),
                                pos.reshape(nb, QBLOCK)))
        oc = jnp.swapaxes(oc, 0, 1).reshape(B, T, C_HEADS * V_DIM)
        od = jnp.swapaxes(od, 0, 1).reshape(B, T, D_HEADS * D_HEAD_DIM)
    else:
        oc = mla_attend(q_nope, q_pe, k_nope, kpe_all, v_c, pos)
        od = dsa_attend(dq, iq, iw, dk_all, dv_all, ik_all, pos, n_sel, rel_bias)
    new = (lat, kpe, dk, dv, ik)
    return jnp.concatenate([oc, od], axis=-1) @ w_out, new


def moe_ffn(h, w_rg, w_re, w_gate, w_up, w_down):
    n = h.shape[0]
    lg = (h @ w_rg).astype(jnp.float32)
    grp = jnp.argmax(lg, axis=-1)
    p_grp = jnp.take_along_axis(jax.nn.softmax(lg, axis=-1), grp[:, None], axis=-1)
    le = (h @ w_re).astype(jnp.float32).reshape(n, N_GROUPS, EXPERTS_PER_GROUP)
    le = jnp.take_along_axis(le, grp[:, None, None], axis=1)[:, 0]
    pe, ie = lax.top_k(jax.nn.softmax(le, axis=-1), TOP_K)
    gates = (p_grp * pe / pe.sum(-1, keepdims=True)).reshape(-1)
    expert = (grp[:, None] * EXPERTS_PER_GROUP + ie).reshape(-1)
    m = n * TOP_K
    order = jnp.argsort(expert)
    e_sorted = expert[order]
    tok = order // TOP_K
    sizes = jnp.bincount(expert, length=N_EXPERTS)
    padded = (sizes + MOE_BLOCK - 1) // MOE_BLOCK * MOE_BLOCK
    start = jnp.cumsum(sizes) - sizes
    pend = jnp.cumsum(padded)
    pstart = pend - padded
    dest = pstart[e_sorted] + jnp.arange(m) - start[e_sorted]
    n_blocks = -(-m // MOE_BLOCK) + N_EXPERTS
    rows = jnp.zeros((n_blocks * MOE_BLOCK, h.shape[1]), h.dtype).at[dest].set(h[tok])
    block_e = jnp.minimum(jnp.searchsorted(pend, jnp.arange(n_blocks) * MOE_BLOCK, side='right'), N_EXPERTS - 1)

    def expert_block(args):
        xb, e = args
        return (jax.nn.silu(xb @ w_gate[e]) * (xb @ w_up[e])) @ w_down[e]

    y = lax.map(expert_block, (rows.reshape(n_blocks, MOE_BLOCK, h.shape[1]), block_e))
    y = y.reshape(-1, h.shape[1])[dest] * gates[order][:, None].astype(h.dtype)
    return jnp.zeros_like(h).at[tok].add(y)


def trunk(x, c, caches, p):
    B, T, _ = x.shape
    pos = jnp.arange(T) if caches is None else PAST_LEN + jnp.arange(T)
    new = [[] for _ in range(8)]
    for l in range(DEPTH):
        j = l // 2
        shift, scale, gate = modulation(c, p['ada_mix_w'][l], p['ada_mix_b'][l])
        h = rmsnorm(x, p['norm_mix'][l]) * (1 + scale) + shift
        if l % 2 == 0:
            cache = None if caches is None else (caches[0][j], caches[1][j], caches[2][j])
            y, st = even_mixer(h, pos, p['even_w_in'][j], p['even_w_out'][j], p['a_sinks'][j],
                               p['pool_w'][j], p['pool_scale'][j], p['rel_bias'], cache)
            for i, s in enumerate(st):
                new[i].append(s)
        else:
            cache = None if caches is None else tuple(caches[3 + i][j] for i in range(5))
            y, st = odd_mixer(h, pos, p['odd_w_in'][j], p['odd_w_out'][j], p['c_q_norm'][j],
                              p['c_kv_norm'][j], p['c_w_qb'][j], p['c_w_kvb'][j], p['rel_bias'], cache)
            for i, s in enumerate(st):
                new[3 + i].append(s)
        x = x + gate * y
        shift, scale, gate = modulation(c, p['ada_ffn_w'][l], p['ada_ffn_b'][l])
        h = rmsnorm(x, p['norm_ffn'][l]) * (1 + scale) + shift
        f = moe_ffn(h.reshape(B * T, D_MODEL), p['moe_router_group'][l], p['moe_router_expert'][l],
                    p['moe_w_gate'][l], p['moe_w_up'][l], p['moe_w_down'][l])
        x = x + gate * f.reshape(B, T, D_MODEL)
    return rmsnorm(x, p['norm_final']), [jnp.stack(s) for s in new]


def setup_inputs(seed: int = 0) -> dict:
    key = jax.random.key(seed)
    ks = iter(jax.random.split(key, 48))

    def nrm(shape, scale):
        return jax.random.normal(next(ks), shape, jnp.float32) * scale

    def gain(shape):
        return 1.0 + nrm(shape, 0.05)

    d = D_MODEL
    win_rows = min(WINDOW, PAST_LEN)
    return {
        'x_prompt': nrm((BATCH, SEQ, d), 1.0),
        'x_sample': nrm((DEC_BATCH, DEC_SEQ, d), 1.0),
        'c_prompt': nrm((BATCH, d), 1.0),
        'c_sample': nrm((DEC_BATCH, d), 1.0),
        'cache_a_k': nrm((N_PAIR, DEC_BATCH, win_rows, A_KV_HEADS, A_HEAD_DIM), 1.0),
        'cache_a_v': nrm((N_PAIR, DEC_BATCH, win_rows, A_KV_HEADS, A_HEAD_DIM), 1.0),
        'state_b_pool': nrm((N_PAIR, DEC_BATCH, POOL_HIST, POOL_WIDTH), 1.0),
        'cache_c_latent': nrm((N_PAIR, DEC_BATCH, PAST_LEN, KV_LORA), 1.0),
        'cache_c_kpe': nrm((N_PAIR, DEC_BATCH, PAST_LEN, ROPE_DIM), 1.0),
        'cache_d_k': nrm((N_PAIR, DEC_BATCH, PAST_LEN, D_KV_HEADS, D_HEAD_DIM), 1.0),
        'cache_d_v': nrm((N_PAIR, DEC_BATCH, PAST_LEN, D_KV_HEADS, D_HEAD_DIM), 1.0),
        'cache_d_idx': nrm((N_PAIR, DEC_BATCH, PAST_LEN, IDX_DIM), 1.0),
        'rel_bias': nrm((NUM_BUCKETS, BIAS_HEADS), 0.5),
        'ada_mix_w': nrm((DEPTH, d, 3 * d), 0.5 * d ** -0.5),
        'ada_mix_b': nrm((DEPTH, 3 * d), 0.02),
        'ada_ffn_w': nrm((DEPTH, d, 3 * d), 0.5 * d ** -0.5),
        'ada_ffn_b': nrm((DEPTH, 3 * d), 0.02),
        'norm_mix': gain((DEPTH, d)),
        'norm_ffn': gain((DEPTH, d)),
        'norm_final': gain((d,)),
        'even_w_in': nrm((N_PAIR, d, EVEN_IN), d ** -0.5),
        'even_w_out': nrm((N_PAIR, EVEN_MIX, d), EVEN_MIX ** -0.5),
        'a_sinks': nrm((N_PAIR, A_HEADS), 0.5),
        'pool_w': nrm((N_PAIR, POOL_GROUPS, POOL_GROUP_WIDTH, POOL_GROUP_WIDTH), POOL_GROUP_WIDTH ** -0.5),
        'pool_scale': gain((N_PAIR, POOL_WIDTH)),
        'odd_w_in': nrm((N_PAIR, d, ODD_IN), d ** -0.5),
        'odd_w_out': nrm((N_PAIR, ODD_MIX, d), ODD_MIX ** -0.5),
        'c_q_norm': gain((N_PAIR, Q_LORA)),
        'c_kv_norm': gain((N_PAIR, KV_LORA)),
        'c_w_qb': nrm((N_PAIR, Q_LORA, C_HEADS * (NOPE_DIM + ROPE_DIM)), Q_LORA ** -0.5),
        'c_w_kvb': nrm((N_PAIR, KV_LORA, C_HEADS * (NOPE_DIM + V_DIM)), KV_LORA ** -0.5),
        'moe_router_group': nrm((DEPTH, d, N_GROUPS), d ** -0.5),
        'moe_router_expert': nrm((DEPTH, d, N_EXPERTS), d ** -0.5),
        'moe_w_gate': nrm((DEPTH, N_EXPERTS, d, D_EXPERT), d ** -0.5),
        'moe_w_up': nrm((DEPTH, N_EXPERTS, d, D_EXPERT), d ** -0.5),
        'moe_w_down': nrm((DEPTH, N_EXPERTS, D_EXPERT, d), D_EXPERT ** -0.5),
    }


def reference(x_prompt, x_sample, c_prompt, c_sample, cache_a_k, cache_a_v, state_b_pool,
              cache_c_latent, cache_c_kpe, cache_d_k, cache_d_v, cache_d_idx, rel_bias,
              ada_mix_w, ada_mix_b, ada_ffn_w, ada_ffn_b, norm_mix, norm_ffn, norm_final,
              even_w_in, even_w_out, a_sinks, pool_w, pool_scale, odd_w_in, odd_w_out,
              c_q_norm, c_kv_norm, c_w_qb, c_w_kvb, moe_router_group, moe_router_expert,
              moe_w_gate, moe_w_up, moe_w_down):
    p = dict(rel_bias=rel_bias, ada_mix_w=ada_mix_w, ada_mix_b=ada_mix_b, ada_ffn_w=ada_ffn_w,
             ada_ffn_b=ada_ffn_b, norm_mix=norm_mix, norm_ffn=norm_ffn, norm_final=norm_final,
             even_w_in=even_w_in, even_w_out=even_w_out, a_sinks=a_sinks, pool_w=pool_w,
             pool_scale=pool_scale, odd_w_in=odd_w_in, odd_w_out=odd_w_out, c_q_norm=c_q_norm,
             c_kv_norm=c_kv_norm, c_w_qb=c_w_qb, c_w_kvb=c_w_kvb, moe_router_group=moe_router_group,
             moe_router_expert=moe_router_expert, moe_w_gate=moe_w_gate, moe_w_up=moe_w_up,
             moe_w_down=moe_w_down)
    y_prompt, st_p = trunk(x_prompt, c_prompt, None, p)
    caches = (cache_a_k, cache_a_v, state_b_pool, cache_c_latent, cache_c_kpe, cache_d_k, cache_d_v, cache_d_idx)
    y_sample, st_s = trunk(x_sample, c_sample, caches, p)
    a_k_p, a_v_p, pool_p, lat_p, kpe_p, dk_p, dv_p, didx_p = st_p
    a_k_s, a_v_s, pool_s, lat_s, kpe_s, dk_s, dv_s, didx_s = st_s
    return (y_prompt, y_sample, a_k_p, a_k_s, a_v_p, a_v_s, pool_p, pool_s, lat_p, lat_s,
            kpe_p, kpe_s, dk_p, dk_s, dv_p, dv_s, didx_p, didx_s)
```

```python
import functools
import math

import jax
import jax.numpy as jnp
from jax import lax
from jax.experimental import pallas as pl
from jax.experimental.pallas import tpu as pltpu

D_MODEL = 4096
BATCH = 2
SEQ = 8192
DEPTH = 4
DEC_BATCH = 16
DEC_SEQ = 32
PAST_LEN = 1024
CHUNK = 64
QBLOCK = 128
N_PAIR = DEPTH // 2
A_HEADS = 32
A_KV_HEADS = 8
A_HEAD_DIM = 64
A_GROUP = A_HEADS // A_KV_HEADS
WINDOW = 128
WIN_CHUNKS = WINDOW // CHUNK
POOL_WIDTH = 2048
POOL_GROUPS = 4
POOL_GROUP_WIDTH = POOL_WIDTH // POOL_GROUPS
POOL_WINDOWS = (2, 4, 8, 16)
POOL_HIST = max(POOL_WINDOWS) - 1
C_HEADS = 16
Q_LORA = 1024
KV_LORA = 512
NOPE_DIM = 128
ROPE_DIM = 64
V_DIM = 128
ROPE_THETA = 10000.0
D_HEADS = 32
D_KV_HEADS = 8
D_HEAD_DIM = 64
D_GROUP = D_HEADS // D_KV_HEADS
IDX_HEADS = 32
IDX_DIM = 64
TOPK_MAX = 256
NUM_BUCKETS = 32
MAX_DISTANCE = 128
N_GROUPS = 8
EXPERTS_PER_GROUP = 8
N_EXPERTS = N_GROUPS * EXPERTS_PER_GROUP
TOP_K = 2
D_EXPERT = 512
RMS_EPS = 1e-6
NEG_INF = -1e30

EVEN_SPLITS = (A_HEADS * A_HEAD_DIM, A_KV_HEADS * A_HEAD_DIM, A_KV_HEADS * A_HEAD_DIM, POOL_WIDTH)
ODD_SPLITS = (Q_LORA, KV_LORA, ROPE_DIM, D_HEADS * D_HEAD_DIM, D_KV_HEADS * D_HEAD_DIM,
              D_KV_HEADS * D_HEAD_DIM, IDX_HEADS * IDX_DIM, IDX_DIM, IDX_HEADS)
ODD_IN = sum(ODD_SPLITS)
ODD_IN_PAD = -(-ODD_IN // 128) * 128

N_PROMPT = BATCH * SEQ
N_SAMPLE = DEC_BATCH * DEC_SEQ
N_TOK = N_PROMPT + N_SAMPLE
N_COND = BATCH + DEC_BATCH

ROW_BLOCK = 512
ROW_GROUPS = ROW_BLOCK // DEC_SEQ
N_ROW_BLOCKS = N_TOK // ROW_BLOCK
MOE_ROWS = 256
ROUTER_PAD = 128
VMEM_LIMIT = 56 * 1024 * 1024

assert N_SAMPLE == ROW_BLOCK and SEQ % ROW_BLOCK == 0 and N_TOK % ROW_BLOCK == 0


def _params(*sem):
    return pltpu.CompilerParams(dimension_semantics=sem, vmem_limit_bytes=VMEM_LIMIT)


def split_cols(z, sizes):
    out, o = [], 0
    for s in sizes:
        out.append(z[..., o:o + s])
        o += s
    return out


def _modulation_kernel(c_ref, w_ref, b_ref, o_ref):
    c = c_ref[...]
    a = (c * jax.nn.sigmoid(c)).astype(jnp.bfloat16)
    o_ref[0] = jnp.dot(a, w_ref[0].astype(jnp.bfloat16), preferred_element_type=jnp.float32) + b_ref[0]


def modulation_all(c, w, b):
    tn = 1024
    depth, d, n = w.shape
    return pl.pallas_call(
        _modulation_kernel,
        grid=(depth, n // tn),
        in_specs=[pl.BlockSpec((N_COND, d), lambda l, j: (0, 0)),
                  pl.BlockSpec((1, d, tn), lambda l, j: (l, 0, j)),
                  pl.BlockSpec((1, 1, tn), lambda l, j: (l, 0, j))],
        out_specs=pl.BlockSpec((1, N_COND, tn), lambda l, j: (l, 0, j)),
        out_shape=jax.ShapeDtypeStruct((depth, N_COND, n), jnp.float32),
        compiler_params=_params("parallel", "parallel"),
        name="modulation",
    )(c, w, b.reshape(depth, 1, n))


def expand_rows(m):
    d = m.shape[-1]
    per_batch = SEQ // ROW_BLOCK
    p = jnp.broadcast_to(m[:BATCH, None, None, :], (BATCH, per_batch, ROW_GROUPS, d))
    return jnp.concatenate([p.reshape(BATCH * per_batch, ROW_GROUPS, d), m[None, BATCH:]], axis=0)


def _norm_mod(x_ref, g_ref, scale_ref, shift_ref):
    x = x_ref[...]
    y = x * lax.rsqrt(jnp.mean(x * x, axis=-1, keepdims=True) + RMS_EPS) * g_ref[...]
    y = y.reshape(ROW_GROUPS, DEC_SEQ, D_MODEL)
    y = y * (1.0 + scale_ref[0][:, None, :]) + shift_ref[0][:, None, :]
    return y.reshape(ROW_BLOCK, D_MODEL)


def _norm_mod_kernel(x_ref, g_ref, scale_ref, shift_ref, h_ref):
    h_ref[...] = _norm_mod(x_ref, g_ref, scale_ref, shift_ref).astype(jnp.bfloat16)


def _norm_mod_router_kernel(x_ref, g_ref, scale_ref, shift_ref, whi_ref, wlo_ref, h_ref, lg_ref):
    h = _norm_mod(x_ref, g_ref, scale_ref, shift_ref)
    hi = h.astype(jnp.bfloat16)
    lo = (h - hi.astype(jnp.float32)).astype(jnp.bfloat16)
    h_ref[...] = hi
    lg_ref[...] = (jnp.dot(hi, whi_ref[...], preferred_element_type=jnp.float32)
                   + jnp.dot(lo, whi_ref[...], preferred_element_type=jnp.float32)
                   + jnp.dot(hi, wlo_ref[...], preferred_element_type=jnp.float32))


def norm_mod(x, g, scale, shift, w_router=None):
    row = pl.BlockSpec((ROW_BLOCK, D_MODEL), lambda i: (i, 0))
    grp = pl.BlockSpec((1, ROW_GROUPS, D_MODEL), lambda i: (i, 0, 0))
    in_specs = [row, pl.BlockSpec((1, D_MODEL), lambda i: (0, 0)), grp, grp]
    h_shape = jax.ShapeDtypeStruct((N_TOK, D_MODEL), jnp.bfloat16)
    args = (x, g.reshape(1, D_MODEL), expand_rows(scale), expand_rows(shift))
    if w_router is None:
        return pl.pallas_call(
            _norm_mod_kernel, grid=(N_ROW_BLOCKS,), in_specs=in_specs, out_specs=row, out_shape=h_shape,
            compiler_params=_params("parallel"), name="norm_mod")(*args)
    w_hi = w_router.astype(jnp.bfloat16)
    w_lo = (w_router - w_hi.astype(jnp.float32)).astype(jnp.bfloat16)
    wspec = pl.BlockSpec((D_MODEL, ROUTER_PAD), lambda i: (0, 0))
    return pl.pallas_call(
        _norm_mod_router_kernel, grid=(N_ROW_BLOCKS,), in_specs=in_specs + [wspec, wspec],
        out_specs=(row, pl.BlockSpec((ROW_BLOCK, ROUTER_PAD), lambda i: (i, 0))),
        out_shape=(h_shape, jax.ShapeDtypeStruct((N_TOK, ROUTER_PAD), jnp.float32)),
        compiler_params=_params("parallel"), name="norm_mod_router")(*args, w_hi, w_lo)


def _matmul_kernel(a_ref, b_ref, o_ref):
    o_ref[...] = jnp.dot(a_ref[...].astype(jnp.bfloat16), b_ref[...],
                         preferred_element_type=jnp.float32).astype(o_ref.dtype)


def matmul(a, b, *, tm=ROW_BLOCK, tn=512, out_dtype=jnp.float32):
    m, k = a.shape
    n = b.shape[1]
    tm, tn = min(tm, m), min(tn, n)
    assert m % tm == 0 and n % tn == 0, (a.shape, b.shape, tm, tn)
    return pl.pallas_call(
        _matmul_kernel, grid=(m // tm, n // tn),
        in_specs=[pl.BlockSpec((tm, k), lambda i, j: (i, 0)), pl.BlockSpec((k, tn), lambda i, j: (0, j))],
        out_specs=pl.BlockSpec((tm, tn), lambda i, j: (i, j)),
        out_shape=jax.ShapeDtypeStruct((m, n), out_dtype),
        compiler_params=_params("parallel", "parallel"), name="matmul",
    )(a, b)


def _matmul_resid_kernel(a_ref, b_ref, x_ref, gate_ref, o_ref):
    y = jnp.dot(a_ref[...], b_ref[...], preferred_element_type=jnp.float32)
    tn = y.shape[-1]
    y = y.reshape(ROW_GROUPS, DEC_SEQ, tn) * gate_ref[0][:, None, :]
    o_ref[...] = x_ref[...] + y.reshape(ROW_BLOCK, tn)


def matmul_resid(a, b, x, gate, *, tn=512):
    k = a.shape[1]
    return pl.pallas_call(
        _matmul_resid_kernel, grid=(N_ROW_BLOCKS, D_MODEL // tn),
        in_specs=[pl.BlockSpec((ROW_BLOCK, k), lambda i, j: (i, 0)),
                  pl.BlockSpec((k, tn), lambda i, j: (0, j)),
                  pl.BlockSpec((ROW_BLOCK, tn), lambda i, j: (i, j)),
                  pl.BlockSpec((1, ROW_GROUPS, tn), lambda i, j: (i, 0, j))],
        out_specs=pl.BlockSpec((ROW_BLOCK, tn), lambda i, j: (i, j)),
        out_shape=jax.ShapeDtypeStruct((N_TOK, D_MODEL), jnp.float32),
        compiler_params=_params("parallel", "parallel"), name="matmul_resid",
    )(a, b, x, expand_rows(gate))


def _expert_kernel(block_e_ref, n_active_ref, x_ref, wg_ref, wu_ref, wd_ref, o_ref):
    i = pl.program_id(0)

    @pl.when(i < n_active_ref[0])
    def _():
        x = x_ref[...]
        g = jnp.dot(x, wg_ref[0], preferred_element_type=jnp.float32)
        u = jnp.dot(x, wu_ref[0], preferred_element_type=jnp.float32)
        a = (g * jax.nn.sigmoid(g) * u).astype(jnp.bfloat16)
        o_ref[...] = jnp.dot(a, wd_ref[0], preferred_element_type=jnp.float32)

    @pl.when(i >= n_active_ref[0])
    def _():
        o_ref[...] = jnp.zeros_like(o_ref)


def expert_blocks(rows, block_e, n_active, w_gate, w_up, w_down):
    n_blocks = rows.shape[0] // MOE_ROWS
    return pl.pallas_call(
        _expert_kernel,
        grid_spec=pltpu.PrefetchScalarGridSpec(
            num_scalar_prefetch=2, grid=(n_blocks,),
            in_specs=[pl.BlockSpec((MOE_ROWS, D_MODEL), lambda i, be, na: (i, 0)),
                      pl.BlockSpec((1, D_MODEL, D_EXPERT), lambda i, be, na: (be[i], 0, 0)),
                      pl.BlockSpec((1, D_MODEL, D_EXPERT), lambda i, be, na: (be[i], 0, 0)),
                      pl.BlockSpec((1, D_EXPERT, D_MODEL), lambda i, be, na: (be[i], 0, 0))],
            out_specs=pl.BlockSpec((MOE_ROWS, D_MODEL), lambda i, be, na: (i, 0))),
        out_shape=jax.ShapeDtypeStruct(rows.shape, jnp.float32),
        compiler_params=_params("arbitrary"), name="moe_experts",
    )(block_e, n_active, rows, w_gate, w_up, w_down)


def moe_ffn(h, logits, w_gate, w_up, w_down):
    n = h.shape[0]
    lg = logits[:, :N_GROUPS]
    grp = jnp.argmax(lg, axis=-1)
    p_grp = jnp.take_along_axis(jax.nn.softmax(lg, axis=-1), grp[:, None], axis=-1)
    le = logits[:, N_GROUPS:N_GROUPS + N_EXPERTS].reshape(n, N_GROUPS, EXPERTS_PER_GROUP)
    le = jnp.take_along_axis(le, grp[:, None, None], axis=1)[:, 0]
    pe, ie = lax.top_k(jax.nn.softmax(le, axis=-1), TOP_K)
    gates = p_grp * pe / pe.sum(-1, keepdims=True)
    expert = (grp[:, None] * EXPERTS_PER_GROUP + ie).astype(jnp.int32)
    m = n * TOP_K
    flat_e = expert.reshape(m)
    onehot = (flat_e[:, None] == jnp.arange(N_EXPERTS, dtype=jnp.int32)[None, :]).astype(jnp.int32)
    csum = jnp.cumsum(onehot, axis=0)
    rank = jnp.take_along_axis(csum, flat_e[:, None], axis=1)[:, 0] - 1
    sizes = csum[-1]
    padded = (sizes + MOE_ROWS - 1) // MOE_ROWS * MOE_ROWS
    pend = jnp.cumsum(padded)
    pstart = pend - padded
    dest = pstart[flat_e] + rank
    n_blocks = -(-m // MOE_ROWS) + N_EXPERTS
    tok_of_row = jnp.zeros((n_blocks * MOE_ROWS,), jnp.int32).at[dest].set(
        jnp.arange(m, dtype=jnp.int32) // TOP_K)
    row_used = jnp.zeros((n_blocks * MOE_ROWS,), jnp.bool_).at[dest].set(True)
    rows = jnp.where(row_used[:, None], h[tok_of_row], jnp.zeros((), h.dtype))
    block_start = jnp.arange(n_blocks, dtype=jnp.int32) * MOE_ROWS
    block_e = jnp.minimum(jnp.searchsorted(pend, block_start, side='right'), N_EXPERTS - 1).astype(jnp.int32)
    n_active = (pend[-1] // MOE_ROWS).astype(jnp.int32).reshape(1)
    y = expert_blocks(rows, block_e, n_active, w_gate, w_up, w_down)
    yk = y[dest].reshape(n, TOP_K, D_MODEL) * gates[:, :, None]
    return yk.sum(axis=1)


def t5_bucket(rel):
    half = NUM_BUCKETS // 2
    max_exact = half // 2
    ret = jnp.where(rel > 0, half, 0)
    n = jnp.abs(rel)
    nf = jnp.maximum(n, 1).astype(jnp.float32)
    large = max_exact + (jnp.log(nf / max_exact) / math.log(MAX_DISTANCE / max_exact)
                         * (half - max_exact)).astype(jnp.int32)
    large = jnp.minimum(large, half - 1)
    return ret + jnp.where(n < max_exact, n, large)


def head_bias(rel, table, n_kv, group):
    b = table[t5_bucket(rel)]
    return jnp.moveaxis(b, -1, 0).reshape((n_kv, group) + rel.shape)


def apply_rope(x, pos):
    inv = jnp.exp(-math.log(ROPE_THETA) * jnp.arange(0, ROPE_DIM, 2, dtype=jnp.float32) / ROPE_DIM)
    ang = pos.astype(jnp.float32)[:, None] * inv[None, :]
    ang = ang.reshape(ang.shape[:1] + (1,) * (x.ndim - 3) + ang.shape[1:])
    cos, sin = jnp.cos(ang), jnp.sin(ang)
    x1, x2 = jnp.split(x.astype(jnp.float32), 2, axis=-1)
    return jnp.concatenate([x1 * cos - x2 * sin, x2 * cos + x1 * sin], axis=-1)


def rmsnorm(x, g):
    return x * lax.rsqrt(jnp.mean(x * x, axis=-1, keepdims=True) + RMS_EPS) * g


def sink_attention(q, k, v, bias, valid, sinks):
    n, tq = q.shape[:2]
    s = jnp.einsum('nqkgd,nskd->nkgqs', q, k) * A_HEAD_DIM ** -0.5 + bias
    s = jnp.where(valid[:, None, None], s, NEG_INF)
    sink = sinks.reshape(1, A_KV_HEADS, A_GROUP, 1, 1)
    mx = jnp.maximum(s.max(-1, keepdims=True), sink)
    e = jnp.exp(s - mx)
    p = e / (e.sum(-1, keepdims=True) + jnp.exp(sink - mx))
    o = jnp.einsum('nkgqs,nskd->nqkgd', p, v)
    return o.reshape(n, tq, A_HEADS * A_HEAD_DIM)


def window_attention_prompt(q, k, v, sinks, rel_bias):
    B, S = q.shape[:2]
    nc = S // CHUNK
    band = (WIN_CHUNKS + 1) * CHUNK

    def banded(t):
        tc = t.reshape(B, nc, CHUNK, A_KV_HEADS, A_HEAD_DIM)
        tp = jnp.concatenate([jnp.zeros_like(tc[:, :WIN_CHUNKS]), tc], axis=1)
        tb = jnp.concatenate([tp[:, j:j + nc] for j in range(WIN_CHUNKS + 1)], axis=2)
        return tb.reshape(B * nc, band, A_KV_HEADS, A_HEAD_DIM)

    qi = jnp.arange(CHUNK)[:, None]
    kr = jnp.arange(band)[None, :]
    rel = kr - WIN_CHUNKS * CHUNK - qi
    key_chunk = jnp.arange(nc)[:, None] - WIN_CHUNKS + kr // CHUNK
    valid = jnp.broadcast_to((key_chunk >= 0)[None, :, None, :], (B, nc, CHUNK, band)).reshape(B * nc, CHUNK, band)
    qb = q.reshape(B * nc, CHUNK, A_KV_HEADS, A_GROUP, A_HEAD_DIM)
    o = sink_attention(qb, banded(k), banded(v), head_bias(rel, rel_bias, A_KV_HEADS, A_GROUP), valid, sinks)
    return o.reshape(B, S, A_HEADS * A_HEAD_DIM)


def window_attention_sample(q, k_all, v_all, pos, W, sinks, rel_bias):
    B, T = q.shape[:2]
    k_pos = jnp.concatenate([PAST_LEN - W + jnp.arange(W), pos])
    qc = (pos // CHUNK)[:, None]
    kc = (k_pos // CHUNK)[None, :]
    valid = (kc <= qc) & (kc >= qc - WIN_CHUNKS)
    valid = jnp.broadcast_to(valid[None], (B,) + valid.shape)
    bias = head_bias(k_pos[None, :] - pos[:, None], rel_bias, A_KV_HEADS, A_GROUP)
    return sink_attention(q, k_all, v_all, bias, valid, sinks)


def pool_mix(u_ext, pos, n_hist, pool_w, pool_scale):
    B, L, P = u_ext.shape
    T = L - n_hist
    cs = jnp.cumsum(u_ext, axis=1)
    cs = jnp.concatenate([jnp.zeros_like(cs[:, :1]), cs], axis=1)
    end = n_hist + jnp.arange(T) + 1
    outs = []
    for g, w in enumerate(POOL_WINDOWS):
        lo, hi = g * POOL_GROUP_WIDTH, (g + 1) * POOL_GROUP_WIDTH
        csg = cs[..., lo:hi]
        start = jnp.maximum(end - w, 0)
        cnt = jnp.minimum(w, pos + 1).astype(jnp.float32)[None, :, None]
        mean = (csg[:, end] - csg[:, start]) / cnt
        outs.append(mean - u_ext[:, n_hist:, lo:hi])
    d = jnp.stack(outs, axis=2)
    y = jnp.einsum('btgc,gcd->btgd', d, pool_w).reshape(B, T, P)
    return y * pool_scale


def even_mix(z, pos, sinks, pool_w, pool_scale, rel_bias, cache):
    B, T, _ = z.shape
    q, k, v, u = split_cols(z, EVEN_SPLITS)
    q = q.reshape(B, T, A_KV_HEADS, A_GROUP, A_HEAD_DIM)
    k = k.reshape(B, T, A_KV_HEADS, A_HEAD_DIM)
    v = v.reshape(B, T, A_KV_HEADS, A_HEAD_DIM)
    if cache is None:
        a = window_attention_prompt(q, k, v, sinks, rel_bias)
        p = pool_mix(u, pos, 0, pool_w, pool_scale)
        new = (k[:, -WINDOW:], v[:, -WINDOW:], u[:, -POOL_HIST:])
    else:
        k_cache, v_cache, u_cache = cache
        W = k_cache.shape[1]
        k_all = jnp.concatenate([k_cache, k], axis=1)
        v_all = jnp.concatenate([v_cache, v], axis=1)
        a = window_attention_sample(q, k_all, v_all, pos, W, sinks, rel_bias)
        u_all = jnp.concatenate([u_cache, u], axis=1)
        p = pool_mix(u_all, pos, POOL_HIST, pool_w, pool_scale)
        new = (k_all[:, -W:], v_all[:, -W:], u_all[:, -POOL_HIST:])
    return jnp.concatenate([a, p], axis=-1), new


def mla_attend(q_nope, q_pe, k_nope, k_pe, v, q_pos):
    L = k_nope.shape[1]
    s = (jnp.einsum('bqhd,bshd->bhqs', q_nope, k_nope)
         + jnp.einsum('bqhr,bsr->bhqs', q_pe, k_pe)) * (NOPE_DIM + ROPE_DIM) ** -0.5
    ok = (jnp.arange(L) // CHUNK)[None, :] <= (q_pos // CHUNK)[:, None]
    p = jax.nn.softmax(jnp.where(ok, s, NEG_INF), axis=-1)
    o = jnp.einsum('bhqs,bshd->bqhd', p, v)
    return o.reshape(o.shape[0], o.shape[1], C_HEADS * V_DIM)


def dsa_attend(q, qi, wi, k, v, ki, q_pos, n_sel, rel_bias):
    B, Tq = q.shape[:2]
    L = k.shape[1]
    sc = jnp.einsum('bqhd,bsd->bqhs', qi, ki) * IDX_DIM ** -0.5
    score = jnp.einsum('bqhs,bqh->bqs', jax.nn.relu(sc), wi * IDX_HEADS ** -0.5)
    adm = (jnp.arange(L) // CHUNK)[None, :] <= (q_pos // CHUNK)[:, None]
    score = jnp.where(adm[None], score, -jnp.inf)
    _, idx = lax.top_k(score, n_sel)
    sel_ok = (idx // CHUNK) <= (q_pos // CHUNK)[None, :, None]
    gather = jax.vmap(lambda t, i: t[i])
    k_sel = gather(k, idx)
    v_sel = gather(v, idx)
    s = jnp.einsum('bqhgd,bqshd->bqhgs', q, k_sel) * D_HEAD_DIM ** -0.5
    bias = rel_bias[t5_bucket(idx - q_pos[None, :, None])]
    bias = bias.reshape(B, Tq, n_sel, D_KV_HEADS, D_GROUP).transpose(0, 1, 3, 4, 2)
    s = jnp.where(sel_ok[:, :, None, None, :], s + bias, NEG_INF)
    p = jax.nn.softmax(s, axis=-1)
    o = jnp.einsum('bqhgs,bqshd->bqhgd', p, v_sel)
    return o.reshape(B, Tq, D_HEADS * D_HEAD_DIM)


def odd_mix(z, pos, g_qa, g_kva, w_qb, w_kvb, rel_bias, cache):
    B, T, _ = z.shape
    q_lat, kv_lat, kpe_raw, dq, dk, dv, iq, ik, iw = split_cols(z, ODD_SPLITS)
    qn = rmsnorm(q_lat, g_qa).reshape(B * T, Q_LORA)
    q = matmul(qn, w_qb, tm=min(ROW_BLOCK, B * T)).reshape(B, T, C_HEADS, NOPE_DIM + ROPE_DIM)
    q_nope = q[..., :NOPE_DIM]
    q_pe = apply_rope(q[..., NOPE_DIM:], pos)
    lat = rmsnorm(kv_lat, g_kva)
    kpe = apply_rope(kpe_raw, pos)
    dq = dq.reshape(B, T, D_KV_HEADS, D_GROUP, D_HEAD_DIM)
    dk = dk.reshape(B, T, D_KV_HEADS, D_HEAD_DIM)
    dv = dv.reshape(B, T, D_KV_HEADS, D_HEAD_DIM)
    iq = iq.reshape(B, T, IDX_HEADS, IDX_DIM)
    if cache is None:
        lat_all, kpe_all, dk_all, dv_all, ik_all = lat, kpe, dk, dv, ik
    else:
        c_lat, c_kpe, c_dk, c_dv, c_ik = cache
        lat_all = jnp.concatenate([c_lat, lat], axis=1)
        kpe_all = jnp.concatenate([c_kpe, kpe], axis=1)
        dk_all = jnp.concatenate([c_dk, dk], axis=1)
        dv_all = jnp.concatenate([c_dv, dv], axis=1)
        ik_all = jnp.concatenate([c_ik, ik], axis=1)
    L = lat_all.shape[1]
    kv = matmul(lat_all.reshape(B * L, KV_LORA), w_kvb, tm=ROW_BLOCK if (B * L) % ROW_BLOCK == 0 else 528)
    kv = kv.reshape(B, L, C_HEADS, NOPE_DIM + V_DIM)
    k_nope, v_c = kv[..., :NOPE_DIM], kv[..., NOPE_DIM:]
    n_sel = min(TOPK_MAX, L // 4)
    if cache is None:
        nb = T // QBLOCK

        def blocks(t):
            return jnp.swapaxes(t.reshape((B, nb, QBLOCK) + t.shape[2:]), 0, 1)

        def body(args):
            qn_, qp, dqb, iqb, iwb, qpos = args
            oc = mla_attend(qn_, qp, k_nope, kpe_all, v_c, qpos)
            od = dsa_attend(dqb, iqb, iwb, dk_all, dv_all, ik_all, qpos, n_sel, rel_bias)
            return oc, od

        oc, od = lax.map(body, (blocks(q_nope), blocks(q_pe), blocks(dq), blocks(iq), blocks(iw),
                                pos.reshape(nb, QBLOCK)))
        oc = jnp.swapaxes(oc, 0, 1).reshape(B, T, C_HEADS * V_DIM)
        od = jnp.swapaxes(od, 0, 1).reshape(B, T, D_HEADS * D_HEAD_DIM)
    else:
        oc = mla_attend(q_nope, q_pe, k_nope, kpe_all, v_c, pos)
        od = dsa_attend(dq, iq, iw, dk_all, dv_all, ik_all, pos, n_sel, rel_bias)
    return jnp.concatenate([oc, od], axis=-1), (lat, kpe, dk, dv, ik)


def _final_norm_kernel(x_ref, g_ref, o_ref):
    x = x_ref[...]
    o_ref[...] = x * lax.rsqrt(jnp.mean(x * x, axis=-1, keepdims=True) + RMS_EPS) * g_ref[...]


def final_norm(x, g):
    row = pl.BlockSpec((ROW_BLOCK, D_MODEL), lambda i: (i, 0))
    return pl.pallas_call(
        _final_norm_kernel, grid=(N_ROW_BLOCKS,),
        in_specs=[row, pl.BlockSpec((1, D_MODEL), lambda i: (0, 0))], out_specs=row,
        out_shape=jax.ShapeDtypeStruct((N_TOK, D_MODEL), jnp.float32),
        compiler_params=_params("parallel"), name="final_norm")(x, g.reshape(1, D_MODEL))


def kernel(x_prompt, x_sample, c_prompt, c_sample, cache_a_k, cache_a_v, state_b_pool, cache_c_latent, cache_c_kpe, cache_d_k, cache_d_v, cache_d_idx, rel_bias, ada_mix_w, ada_mix_b, ada_ffn_w, ada_ffn_b, norm_mix, norm_ffn, norm_final, even_w_in, even_w_out, a_sinks, pool_w, pool_scale, odd_w_in, odd_w_out, c_q_norm, c_kv_norm, c_w_qb, c_w_kvb, moe_router_group, moe_router_expert, moe_w_gate, moe_w_up, moe_w_down):
    bf = jnp.bfloat16
    x = jnp.concatenate([x_prompt.reshape(N_PROMPT, D_MODEL), x_sample.reshape(N_SAMPLE, D_MODEL)], axis=0)
    c = jnp.concatenate([c_prompt, c_sample], axis=0)
    mod_mix = modulation_all(c, ada_mix_w, ada_mix_b)
    mod_ffn = modulation_all(c, ada_ffn_w, ada_ffn_b)
    pos_p = jnp.arange(SEQ)
    pos_s = PAST_LEN + jnp.arange(DEC_SEQ)
    w_router = jnp.concatenate(
        [moe_router_group, moe_router_expert,
         jnp.zeros((DEPTH, D_MODEL, ROUTER_PAD - N_GROUPS - N_EXPERTS), jnp.float32)], axis=-1)
    new_p = [[] for _ in range(8)]
    new_s = [[] for _ in range(8)]
    for l in range(DEPTH):
        j = l // 2
        shift, scale, gate = jnp.split(mod_mix[l], 3, axis=-1)
        h = norm_mod(x, norm_mix[l], scale, shift)
        if l % 2 == 0:
            z = matmul(h, even_w_in[j].astype(bf))
            zp = z[:N_PROMPT].reshape(BATCH, SEQ, -1)
            zs = z[N_PROMPT:].reshape(DEC_BATCH, DEC_SEQ, -1)
            mp, st_p = even_mix(zp, pos_p, a_sinks[j], pool_w[j], pool_scale[j], rel_bias, None)
            ms, st_s = even_mix(zs, pos_s, a_sinks[j], pool_w[j], pool_scale[j], rel_bias,
                                (cache_a_k[j], cache_a_v[j], state_b_pool[j]))
            off = 0
            w_out = even_w_out[j]
        else:
            w_in = jnp.pad(odd_w_in[j].astype(bf), ((0, 0), (0, ODD_IN_PAD - ODD_IN)))
            z = matmul(h, w_in, tn=ODD_IN_PAD // 9)[:, :ODD_IN]
            zp = z[:N_PROMPT].reshape(BATCH, SEQ, -1)
            zs = z[N_PROMPT:].reshape(DEC_BATCH, DEC_SEQ, -1)
            wq, wkv = c_w_qb[j].astype(bf), c_w_kvb[j].astype(bf)
            mp, st_p = odd_mix(zp, pos_p, c_q_norm[j], c_kv_norm[j], wq, wkv, rel_bias, None)
            ms, st_s = odd_mix(zs, pos_s, c_q_norm[j], c_kv_norm[j], wq, wkv, rel_bias,
                               (cache_c_latent[j], cache_c_kpe[j], cache_d_k[j], cache_d_v[j], cache_d_idx[j]))
            off = 3
            w_out = odd_w_out[j]
        for i, s in enumerate(st_p):
            new_p[off + i].append(s)
        for i, s in enumerate(st_s):
            new_s[off + i].append(s)
        mix = jnp.concatenate([mp.reshape(N_PROMPT, -1), ms.reshape(N_SAMPLE, -1)], axis=0).astype(bf)
        x = matmul_resid(mix, w_out.astype(bf), x, gate)
        shift, scale, gate = jnp.split(mod_ffn[l], 3, axis=-1)
        h, logits = norm_mod(x, norm_ffn[l], scale, shift, w_router[l])
        f = moe_ffn(h, logits, moe_w_gate[l].astype(bf), moe_w_up[l].astype(bf), moe_w_down[l].astype(bf))
        gate_tok = jnp.concatenate([jnp.repeat(gate[:BATCH], SEQ, axis=0),
                                    jnp.repeat(gate[BATCH:], DEC_SEQ, axis=0)], axis=0)
        x = x + gate_tok * f
    y = final_norm(x, norm_final)
    st_p = [jnp.stack(s) for s in new_p]
    st_s = [jnp.stack(s) for s in new_s]
    a_k_p, a_v_p, pool_p, lat_p, kpe_p, dk_p, dv_p, didx_p = st_p
    a_k_s, a_v_s, pool_s, lat_s, kpe_s, dk_s, dv_s, didx_s = st_s
    return (y[:N_PROMPT].reshape(BATCH, SEQ, D_MODEL), y[N_PROMPT:].reshape(DEC_BATCH, DEC_SEQ, D_MODEL),
            a_k_p, a_k_s, a_v_p, a_v_s, pool_p, pool_s, lat_p, lat_s,
            kpe_p, kpe_s, dk_p, dk_s, dv_p, dv_s, didx_p, didx_s)
```

```python
import functools
import math

import jax
import jax.numpy as jnp
from jax import lax
from jax.experimental import pallas as pl
from jax.experimental.pallas import tpu as pltpu

D_MODEL = 4096
BATCH = 2
SEQ = 8192
DEPTH = 4
DEC_BATCH = 16
DEC_SEQ = 32
PAST_LEN = 1024
CHUNK = 64
QBLOCK = 128
N_PAIR = DEPTH // 2
A_HEADS = 32
A_KV_HEADS = 8
A_HEAD_DIM = 64
A_GROUP = A_HEADS // A_KV_HEADS
WINDOW = 128
WIN_CHUNKS = WINDOW // CHUNK
POOL_WIDTH = 2048
POOL_GROUPS = 4
POOL_GROUP_WIDTH = POOL_WIDTH // POOL_GROUPS
POOL_WINDOWS = (2, 4, 8, 16)
POOL_HIST = max(POOL_WINDOWS) - 1
C_HEADS = 16
Q_LORA = 1024
KV_LORA = 512
NOPE_DIM = 128
ROPE_DIM = 64
V_DIM = 128
ROPE_THETA = 10000.0
D_HEADS = 32
D_KV_HEADS = 8
D_HEAD_DIM = 64
D_GROUP = D_HEADS // D_KV_HEADS
IDX_HEADS = 32
IDX_DIM = 64
TOPK_MAX = 256
NUM_BUCKETS = 32
MAX_DISTANCE = 128
N_GROUPS = 8
EXPERTS_PER_GROUP = 8
N_EXPERTS = N_GROUPS * EXPERTS_PER_GROUP
TOP_K = 2
D_EXPERT = 512
RMS_EPS = 1e-6
NEG_INF = -1e30

EVEN_SPLITS = (A_HEADS * A_HEAD_DIM, A_KV_HEADS * A_HEAD_DIM, A_KV_HEADS * A_HEAD_DIM, POOL_WIDTH)
ODD_SPLITS = (Q_LORA, KV_LORA, ROPE_DIM, D_HEADS * D_HEAD_DIM, D_KV_HEADS * D_HEAD_DIM,
              D_KV_HEADS * D_HEAD_DIM, IDX_HEADS * IDX_DIM, IDX_DIM, IDX_HEADS)
ODD_IN = sum(ODD_SPLITS)
ODD_IN_PAD = -(-ODD_IN // 128) * 128

N_PROMPT = BATCH * SEQ
N_SAMPLE = DEC_BATCH * DEC_SEQ
N_TOK = N_PROMPT + N_SAMPLE
N_COND = BATCH + DEC_BATCH

ROW_BLOCK = 512
ROW_GROUPS = ROW_BLOCK // DEC_SEQ
N_ROW_BLOCKS = N_TOK // ROW_BLOCK
MOE_ROWS = 256
ROUTER_PAD = 128
VMEM_LIMIT = 56 * 1024 * 1024

assert N_SAMPLE == ROW_BLOCK and SEQ % ROW_BLOCK == 0 and N_TOK % ROW_BLOCK == 0


def _params(*sem):
    return pltpu.CompilerParams(dimension_semantics=sem, vmem_limit_bytes=VMEM_LIMIT)


def split_cols(z, sizes):
    out, o = [], 0
    for s in sizes:
        out.append(z[..., o:o + s])
        o += s
    return out


def _modulation_kernel(c_ref, w_ref, b_ref, o_ref):
    c = c_ref[...]
    a = (c * jax.nn.sigmoid(c)).astype(jnp.bfloat16)
    o_ref[0] = jnp.dot(a, w_ref[0].astype(jnp.bfloat16), preferred_element_type=jnp.float32) + b_ref[0]


def modulation_all(c, w, b):
    tn = 1024
    depth, d, n = w.shape
    return pl.pallas_call(
        _modulation_kernel,
        grid=(depth, n // tn),
        in_specs=[pl.BlockSpec((N_COND, d), lambda l, j: (0, 0)),
                  pl.BlockSpec((1, d, tn), lambda l, j: (l, 0, j)),
                  pl.BlockSpec((1, 1, tn), lambda l, j: (l, 0, j))],
        out_specs=pl.BlockSpec((1, N_COND, tn), lambda l, j: (l, 0, j)),
        out_shape=jax.ShapeDtypeStruct((depth, N_COND, n), jnp.float32),
        compiler_params=_params("parallel", "parallel"),
        name="modulation",
    )(c, w, b.reshape(depth, 1, n))


def expand_rows(m):
    d = m.shape[-1]
    per_batch = SEQ // ROW_BLOCK
    p = jnp.broadcast_to(m[:BATCH, None, None, :], (BATCH, per_batch, ROW_GROUPS, d))
    return jnp.concatenate([p.reshape(BATCH * per_batch, ROW_GROUPS, d), m[None, BATCH:]], axis=0)


def _norm_mod(x_ref, g_ref, scale_ref, shift_ref):
    x = x_ref[...]
    y = x * lax.rsqrt(jnp.mean(x * x, axis=-1, keepdims=True) + RMS_EPS) * g_ref[...]
    y = y.reshape(ROW_GROUPS, DEC_SEQ, D_MODEL)
    y = y * (1.0 + scale_ref[0][:, None, :]) + shift_ref[0][:, None, :]
    return y.reshape(ROW_BLOCK, D_MODEL)


def _norm_mod_kernel(x_ref, g_ref, scale_ref, shift_ref, h_ref):
    h_ref[...] = _norm_mod(x_ref, g_ref, scale_ref, shift_ref).astype(jnp.bfloat16)


def _norm_mod_router_kernel(x_ref, g_ref, scale_ref, shift_ref, whi_ref, wlo_ref, h_ref, lg_ref):
    h = _norm_mod(x_ref, g_ref, scale_ref, shift_ref)
    hi = h.astype(jnp.bfloat16)
    lo = (h - hi.astype(jnp.float32)).astype(jnp.bfloat16)
    h_ref[...] = hi
    lg_ref[...] = (jnp.dot(hi, whi_ref[...], preferred_element_type=jnp.float32)
                   + jnp.dot(lo, whi_ref[...], preferred_element_type=jnp.float32)
                   + jnp.dot(hi, wlo_ref[...], preferred_element_type=jnp.float32))


def norm_mod(x, g, scale, shift, w_router=None):
    row = pl.BlockSpec((ROW_BLOCK, D_MODEL), lambda i: (i, 0))
    grp = pl.BlockSpec((1, ROW_GROUPS, D_MODEL), lambda i: (i, 0, 0))
    in_specs = [row, pl.BlockSpec((1, D_MODEL), lambda i: (0, 0)), grp, grp]
    h_shape = jax.ShapeDtypeStruct((N_TOK, D_MODEL), jnp.bfloat16)
    args = (x, g.reshape(1, D_MODEL), expand_rows(scale), expand_rows(shift))
    if w_router is None:
        return pl.pallas_call(
            _norm_mod_kernel, grid=(N_ROW_BLOCKS,), in_specs=in_specs, out_specs=row, out_shape=h_shape,
            compiler_params=_params("parallel"), name="norm_mod")(*args)
    w_hi = w_router.astype(jnp.bfloat16)
    w_lo = (w_router - w_hi.astype(jnp.float32)).astype(jnp.bfloat16)
    wspec = pl.BlockSpec((D_MODEL, ROUTER_PAD), lambda i: (0, 0))
    return pl.pallas_call(
        _norm_mod_router_kernel, grid=(N_ROW_BLOCKS,), in_specs=in_specs + [wspec, wspec],
        out_specs=(row, pl.BlockSpec((ROW_BLOCK, ROUTER_PAD), lambda i: (i, 0))),
        out_shape=(h_shape, jax.ShapeDtypeStruct((N_TOK, ROUTER_PAD), jnp.float32)),
        compiler_params=_params("parallel"), name="norm_mod_router")(*args, w_hi, w_lo)


def _matmul_kernel(a_ref, b_ref, o_ref):
    o_ref[...] = jnp.dot(a_ref[...].astype(jnp.bfloat16), b_ref[...],
                         preferred_element_type=jnp.float32).astype(o_ref.dtype)


def matmul(a, b, *, tm=ROW_BLOCK, tn=512, out_dtype=jnp.float32):
    m, k = a.shape
    n = b.shape[1]
    tm, tn = min(tm, m), min(tn, n)
    assert m % tm == 0 and n % tn == 0, (a.shape, b.shape, tm, tn)
    return pl.pallas_call(
        _matmul_kernel, grid=(m // tm, n // tn),
        in_specs=[pl.BlockSpec((tm, k), lambda i, j: (i, 0)), pl.BlockSpec((k, tn), lambda i, j: (0, j))],
        out_specs=pl.BlockSpec((tm, tn), lambda i, j: (i, j)),
        out_shape=jax.ShapeDtypeStruct((m, n), out_dtype),
        compiler_params=_params("parallel", "parallel"), name="matmul",
    )(a, b)


def _matmul_resid_kernel(a_ref, b_ref, x_ref, gate_ref, o_ref):
    y = jnp.dot(a_ref[...], b_ref[...], preferred_element_type=jnp.float32)
    tn = y.shape[-1]
    y = y.reshape(ROW_GROUPS, DEC_SEQ, tn) * gate_ref[0][:, None, :]
    o_ref[...] = x_ref[...] + y.reshape(ROW_BLOCK, tn)


def matmul_resid(a, b, x, gate, *, tn=512):
    k = a.shape[1]
    return pl.pallas_call(
        _matmul_resid_kernel, grid=(N_ROW_BLOCKS, D_MODEL // tn),
        in_specs=[pl.BlockSpec((ROW_BLOCK, k), lambda i, j: (i, 0)),
                  pl.BlockSpec((k, tn), lambda i, j: (0, j)),
                  pl.BlockSpec((ROW_BLOCK, tn), lambda i, j: (i, j)),
                  pl.BlockSpec((1, ROW_GROUPS, tn), lambda i, j: (i, 0, j))],
        out_specs=pl.BlockSpec((ROW_BLOCK, tn), lambda i, j: (i, j)),
        out_shape=jax.ShapeDtypeStruct((N_TOK, D_MODEL), jnp.float32),
        compiler_params=_params("parallel", "parallel"), name="matmul_resid",
    )(a, b, x, expand_rows(gate))


def _expert_kernel(block_e_ref, n_active_ref, x_ref, wg_ref, wu_ref, wd_ref, o_ref):
    i = pl.program_id(0)

    @pl.when(i < n_active_ref[0])
    def _():
        x = x_ref[...]
        g = jnp.dot(x, wg_ref[0], preferred_element_type=jnp.float32)
        u = jnp.dot(x, wu_ref[0], preferred_element_type=jnp.float32)
        a = (g * jax.nn.sigmoid(g) * u).astype(jnp.bfloat16)
        o_ref[...] = jnp.dot(a, wd_ref[0], preferred_element_type=jnp.float32)

    @pl.when(i >= n_active_ref[0])
    def _():
        o_ref[...] = jnp.zeros_like(o_ref)


def expert_blocks(rows, block_e, n_active, w_gate, w_up, w_down):
    n_blocks = rows.shape[0] // MOE_ROWS
    return pl.pallas_call(
        _expert_kernel,
        grid_spec=pltpu.PrefetchScalarGridSpec(
            num_scalar_prefetch=2, grid=(n_blocks,),
            in_specs=[pl.BlockSpec((MOE_ROWS, D_MODEL), lambda i, be, na: (i, 0)),
                      pl.BlockSpec((1, D_MODEL, D_EXPERT), lambda i, be, na: (be[i], 0, 0)),
                      pl.BlockSpec((1, D_MODEL, D_EXPERT), lambda i, be, na: (be[i], 0, 0)),
                      pl.BlockSpec((1, D_EXPERT, D_MODEL), lambda i, be, na: (be[i], 0, 0))],
            out_specs=pl.BlockSpec((MOE_ROWS, D_MODEL), lambda i, be, na: (i, 0))),
        out_shape=jax.ShapeDtypeStruct(rows.shape, jnp.float32),
        compiler_params=_params("arbitrary"), name="moe_experts",
    )(block_e, n_active, rows, w_gate, w_up, w_down)


def moe_ffn(h, logits, w_gate, w_up, w_down):
    n = h.shape[0]
    lg = logits[:, :N_GROUPS]
    grp = jnp.argmax(lg, axis=-1)
    p_grp = jnp.take_along_axis(jax.nn.softmax(lg, axis=-1), grp[:, None], axis=-1)
    le = logits[:, N_GROUPS:N_GROUPS + N_EXPERTS].reshape(n, N_GROUPS, EXPERTS_PER_GROUP)
    le = jnp.take_along_axis(le, grp[:, None, None], axis=1)[:, 0]
    pe, ie = lax.top_k(jax.nn.softmax(le, axis=-1), TOP_K)
    gates = p_grp * pe / pe.sum(-1, keepdims=True)
    expert = (grp[:, None] * EXPERTS_PER_GROUP + ie).astype(jnp.int32)
    m = n * TOP_K
    flat_e = expert.reshape(m)
    onehot = (flat_e[:, None] == jnp.arange(N_EXPERTS, dtype=jnp.int32)[None, :]).astype(jnp.int32)
    csum = jnp.cumsum(onehot, axis=0)
    rank = jnp.take_along_axis(csum, flat_e[:, None], axis=1)[:, 0] - 1
    sizes = csum[-1]
    padded = (sizes + MOE_ROWS - 1) // MOE_ROWS * MOE_ROWS
    pend = jnp.cumsum(padded)
    pstart = pend - padded
    dest = pstart[flat_e] + rank
    n_blocks = -(-m // MOE_ROWS) + N_EXPERTS
    tok_of_row = jnp.zeros((n_blocks * MOE_ROWS,), jnp.int32).at[dest].set(
        jnp.arange(m, dtype=jnp.int32) // TOP_K)
    row_used = jnp.zeros((n_blocks * MOE_ROWS,), jnp.bool_).at[dest].set(True)
    rows = jnp.where(row_used[:, None], h[tok_of_row], jnp.zeros((), h.dtype))
    block_start = jnp.arange(n_blocks, dtype=jnp.int32) * MOE_ROWS
    block_e = jnp.minimum(jnp.searchsorted(pend, block_start, side='right'), N_EXPERTS - 1).astype(jnp.int32)
    n_active = (pend[-1] // MOE_ROWS).astype(jnp.int32).reshape(1)
    y = expert_blocks(rows, block_e, n_active, w_gate, w_up, w_down)
    yk = y[dest].reshape(n, TOP_K, D_MODEL) * gates[:, :, None]
    return yk.sum(axis=1)


def t5_bucket(rel):
    half = NUM_BUCKETS // 2
    max_exact = half // 2
    ret = jnp.where(rel > 0, half, 0)
    n = jnp.abs(rel)
    nf = jnp.maximum(n, 1).astype(jnp.float32)
    large = max_exact + (jnp.log(nf / max_exact) / math.log(MAX_DISTANCE / max_exact)
                         * (half - max_exact)).astype(jnp.int32)
    large = jnp.minimum(large, half - 1)
    return ret + jnp.where(n < max_exact, n, large)


def head_bias(rel, table, n_kv, group):
    b = table[t5_bucket(rel)]
    return jnp.moveaxis(b, -1, 0).reshape((n_kv, group) + rel.shape)


def apply_rope(x, pos):
    inv = jnp.exp(-math.log(ROPE_THETA) * jnp.arange(0, ROPE_DIM, 2, dtype=jnp.float32) / ROPE_DIM)
    ang = pos.astype(jnp.float32)[:, None] * inv[None, :]
    ang = ang.reshape(ang.shape[:1] + (1,) * (x.ndim - 3) + ang.shape[1:])
    cos, sin = jnp.cos(ang), jnp.sin(ang)
    x1, x2 = jnp.split(x.astype(jnp.float32), 2, axis=-1)
    return jnp.concatenate([x1 * cos - x2 * sin, x2 * cos + x1 * sin], axis=-1)


def rmsnorm(x, g):
    return x * lax.rsqrt(jnp.mean(x * x, axis=-1, keepdims=True) + RMS_EPS) * g


def sink_attention(q, k, v, bias, valid, sinks):
    n, tq = q.shape[:2]
    s = jnp.einsum('nqkgd,nskd->nkgqs', q, k) * A_HEAD_DIM ** -0.5 + bias
    s = jnp.where(valid[:, None, None], s, NEG_INF)
    sink = sinks.reshape(1, A_KV_HEADS, A_GROUP, 1, 1)
    mx = jnp.maximum(s.max(-1, keepdims=True), sink)
    e = jnp.exp(s - mx)
    p = e / (e.sum(-1, keepdims=True) + jnp.exp(sink - mx))
    o = jnp.einsum('nkgqs,nskd->nqkgd', p, v)
    return o.reshape(n, tq, A_HEADS * A_HEAD_DIM)


def window_attention_prompt(q, k, v, sinks, rel_bias):
    B, S = q.shape[:2]
    nc = S // CHUNK
    band = (WIN_CHUNKS + 1) * CHUNK

    def banded(t):
        tc = t.reshape(B, nc, CHUNK, A_KV_HEADS, A_HEAD_DIM)
        tp = jnp.concatenate([jnp.zeros_like(tc[:, :WIN_CHUNKS]), tc], axis=1)
        tb = jnp.concatenate([tp[:, j:j + nc] for j in range(WIN_CHUNKS + 1)], axis=2)
        return tb.reshape(B * nc, band, A_KV_HEADS, A_HEAD_DIM)

    qi = jnp.arange(CHUNK)[:, None]
    kr = jnp.arange(band)[None, :]
    rel = kr - WIN_CHUNKS * CHUNK - qi
    key_chunk = jnp.arange(nc)[:, None] - WIN_CHUNKS + kr // CHUNK
    valid = jnp.broadcast_to((key_chunk >= 0)[None, :, None, :], (B, nc, CHUNK, band)).reshape(B * nc, CHUNK, band)
    qb = q.reshape(B * nc, CHUNK, A_KV_HEADS, A_GROUP, A_HEAD_DIM)
    o = sink_attention(qb, banded(k), banded(v), head_bias(rel, rel_bias, A_KV_HEADS, A_GROUP), valid, sinks)
    return o.reshape(B, S, A_HEADS * A_HEAD_DIM)


def window_attention_sample(q, k_all, v_all, pos, W, sinks, rel_bias):
    B, T = q.shape[:2]
    k_pos = jnp.concatenate([PAST_LEN - W + jnp.arange(W), pos])
    qc = (pos // CHUNK)[:, None]
    kc = (k_pos // CHUNK)[None, :]
    valid = (kc <= qc) & (kc >= qc - WIN_CHUNKS)
    valid = jnp.broadcast_to(valid[None], (B,) + valid.shape)
    bias = head_bias(k_pos[None, :] - pos[:, None], rel_bias, A_KV_HEADS, A_GROUP)
    return sink_attention(q, k_all, v_all, bias, valid, sinks)


def pool_mix(u_ext, pos, n_hist, pool_w, pool_scale):
    B, L, P = u_ext.shape
    T = L - n_hist
    cs = jnp.cumsum(u_ext, axis=1)
    cs = jnp.concatenate([jnp.zeros_like(cs[:, :1]), cs], axis=1)
    end = n_hist + jnp.arange(T) + 1
    outs = []
    for g, w in enumerate(POOL_WINDOWS):
        lo, hi = g * POOL_GROUP_WIDTH, (g + 1) * POOL_GROUP_WIDTH
        csg = cs[..., lo:hi]
        start = jnp.maximum(end - w, 0)
        cnt = jnp.minimum(w, pos + 1).astype(jnp.float32)[None, :, None]
        mean = (csg[:, end] - csg[:, start]) / cnt
        outs.append(mean - u_ext[:, n_hist:, lo:hi])
    d = jnp.stack(outs, axis=2)
    y = jnp.einsum('btgc,gcd->btgd', d, pool_w).reshape(B, T, P)
    return y * pool_scale


def even_mix(z, pos, sinks, pool_w, pool_scale, rel_bias, cache):
    B, T, _ = z.shape
    q, k, v, u = split_cols(z, EVEN_SPLITS)
    q = q.reshape(B, T, A_KV_HEADS, A_GROUP, A_HEAD_DIM)
    k = k.reshape(B, T, A_KV_HEADS, A_HEAD_DIM)
    v = v.reshape(B, T, A_KV_HEADS, A_HEAD_DIM)
    if cache is None:
        a = window_attention_prompt(q, k, v, sinks, rel_bias)
        p = pool_mix(u, pos, 0, pool_w, pool_scale)
        new = (k[:, -WINDOW:], v[:, -WINDOW:], u[:, -POOL_HIST:])
    else:
        k_cache, v_cache, u_cache = cache
        W = k_cache.shape[1]
        k_all = jnp.concatenate([k_cache, k], axis=1)
        v_all = jnp.concatenate([v_cache, v], axis=1)
        a = window_attention_sample(q, k_all, v_all, pos, W, sinks, rel_bias)
        u_all = jnp.concatenate([u_cache, u], axis=1)
        p = pool_mix(u_all, pos, POOL_HIST, pool_w, pool_scale)
        new = (k_all[:, -W:], v_all[:, -W:], u_all[:, -POOL_HIST:])
    return jnp.concatenate([a, p], axis=-1), new


def mla_attend(q_nope, q_pe, k_nope, k_pe, v, q_pos):
    L = k_nope.shape[1]
    s = (jnp.einsum('bqhd,bshd->bhqs', q_nope, k_nope)
         + jnp.einsum('bqhr,bsr->bhqs', q_pe, k_pe)) * (NOPE_DIM + ROPE_DIM) ** -0.5
    ok = (jnp.arange(L) // CHUNK)[None, :] <= (q_pos // CHUNK)[:, None]
    p = jax.nn.softmax(jnp.where(ok, s, NEG_INF), axis=-1)
    o = jnp.einsum('bhqs,bshd->bqhd', p, v)
    return o.reshape(o.shape[0], o.shape[1], C_HEADS * V_DIM)


MLA_T = 512
MLA_QK = 256
MLA_V = 128


def _mla_kernel(q_ref, k_ref, v_ref, o_ref, m_ref, l_ref, acc_ref):
    i = pl.program_id(2)
    k = pl.program_id(3)

    @pl.when(k == 0)
    def _():
        m_ref[...] = jnp.full_like(m_ref, NEG_INF)
        l_ref[...] = jnp.zeros_like(l_ref)
        acc_ref[...] = jnp.zeros_like(acc_ref)

    def step(diagonal):
        s = lax.dot_general(q_ref[...], k_ref[...], (((1,), (1,)), ((), ())),
                            preferred_element_type=jnp.float32)
        if diagonal:
            row = lax.broadcasted_iota(jnp.int32, s.shape, 0) // CHUNK
            col = lax.broadcasted_iota(jnp.int32, s.shape, 1) // CHUNK
            s = jnp.where(col <= row, s, NEG_INF)
        m_prev = m_ref[...]
        m_new = jnp.maximum(m_prev, s.max(axis=1, keepdims=True))
        alpha = jnp.exp(m_prev - m_new)
        p = jnp.exp(s - m_new)
        l_ref[...] = alpha * l_ref[...] + p.sum(axis=1, keepdims=True)
        acc_ref[...] = alpha * acc_ref[...] + jnp.dot(p.astype(jnp.bfloat16), v_ref[...],
                                                      preferred_element_type=jnp.float32)
        m_ref[...] = m_new

    @pl.when(k < i)
    def _():
        step(False)

    @pl.when(k == i)
    def _():
        step(True)
        o_ref[...] = (acc_ref[...] / l_ref[...]).astype(o_ref.dtype)


def mla_prompt(qc, kc, v):
    B, S, _ = qc.shape
    n = S // MLA_T
    return pl.pallas_call(
        _mla_kernel, grid=(B, C_HEADS, n, n),
        in_specs=[pl.BlockSpec((None, MLA_T, MLA_QK), lambda b, h, i, k: (b, i, h)),
                  pl.BlockSpec((None, MLA_T, MLA_QK), lambda b, h, i, k: (b, jnp.minimum(k, i), h)),
                  pl.BlockSpec((None, MLA_T, MLA_V), lambda b, h, i, k: (b, jnp.minimum(k, i), h))],
        out_specs=pl.BlockSpec((None, MLA_T, MLA_V), lambda b, h, i, k: (b, i, h)),
        out_shape=jax.ShapeDtypeStruct((B, S, C_HEADS * MLA_V), jnp.bfloat16),
        scratch_shapes=[pltpu.VMEM((MLA_T, 1), jnp.float32), pltpu.VMEM((MLA_T, 1), jnp.float32),
                        pltpu.VMEM((MLA_T, MLA_V), jnp.float32)],
        compiler_params=_params("parallel", "parallel", "parallel", "arbitrary"), name="mla_prompt",
    )(qc, kc, v)


IDX_TQ = 128
IDX_TK = 256
LANES = 128
INT_MIN = -2 ** 31


def _indexer_kernel(q_ref, w_ref, ke_ref, ko_ref, mask_ref, key_ref, *, n_sel):
    i = pl.program_id(1)
    q0 = i * IDX_TQ
    n_lane_tiles = mask_ref.shape[0]
    n_kb = (q0 + IDX_TQ + IDX_TK - 1) // IDX_TK
    sub = IDX_TK // LANES
    row_chunk = (q0 + lax.broadcasted_iota(jnp.int32, (IDX_TQ, LANES), 0)) // CHUNK
    lane_iota = lax.broadcasted_iota(jnp.int32, (IDX_TQ, LANES), 1)

    def score_tile(kb, carry):
        ke = ke_ref[kb]
        ko = ko_ref[kb]
        acc = jnp.zeros((IDX_TQ, IDX_TK), jnp.float32)
        for p in range(IDX_HEADS // 2):
            qp = q_ref[:, p * LANES:(p + 1) * LANES]
            se = jnp.dot(qp, ke, preferred_element_type=jnp.float32)
            so = jnp.dot(qp, ko, preferred_element_type=jnp.float32)
            acc = acc + jnp.maximum(se, 0.0) * w_ref[:, 2 * p:2 * p + 1]
            acc = acc + jnp.maximum(so, 0.0) * w_ref[:, 2 * p + 1:2 * p + 2]
        for t in range(sub):
            j = kb * sub + t
            a = acc[:, t * LANES:(t + 1) * LANES]
            adm = (j * LANES + lane_iota) // CHUNK <= row_chunk
            a = jnp.where(adm, a, -jnp.inf)
            bits = pltpu.bitcast(a, jnp.int32)
            key_ref[j] = bits ^ ((bits >> 31) & jnp.int32(0x7FFFFFFF))
        return carry

    lax.fori_loop(0, n_kb, score_tile, 0)
    n_tiles = n_kb * sub

    def count_ge(cand):
        def body(j, cnt):
            return cnt + jnp.where(key_ref[j] >= cand, 1, 0)
        cnt = lax.fori_loop(0, n_tiles, body, jnp.zeros((IDX_TQ, LANES), jnp.int32))
        return cnt.sum(axis=1, keepdims=True)

    thr = jnp.where(count_ge(jnp.zeros((IDX_TQ, 1), jnp.int32)) >= n_sel,
                    jnp.zeros((IDX_TQ, 1), jnp.int32), jnp.full((IDX_TQ, 1), INT_MIN, jnp.int32))

    def bit_step(b, thr):
        cand = thr | (jnp.int32(1) << (30 - b))
        return jnp.where(count_ge(cand) >= n_sel, cand, thr)

    thr = lax.fori_loop(0, 31, bit_step, thr)

    def write_mask(j, carry):
        adm = (j * LANES + lane_iota) // CHUNK <= row_chunk
        sel = jnp.logical_and(key_ref[j] >= thr, adm)
        mask_ref[j] = jnp.where(sel, 0.0, NEG_INF).astype(mask_ref.dtype)
        return carry

    lax.fori_loop(0, n_tiles, write_mask, 0)

    def write_rest(j, carry):
        mask_ref[j] = jnp.full((IDX_TQ, LANES), NEG_INF, mask_ref.dtype)
        return carry

    lax.fori_loop(n_tiles, n_lane_tiles, write_rest, 0)


def indexer_mask(qi, wi, ke, ko, n_sel):
    B, S, _ = qi.shape
    nkb = ke.shape[1]
    L = nkb * IDX_TK
    return pl.pallas_call(
        functools.partial(_indexer_kernel, n_sel=n_sel), grid=(B, S // IDX_TQ),
        in_specs=[pl.BlockSpec((None, IDX_TQ, IDX_HEADS * IDX_DIM), lambda b, i: (b, i, 0)),
                  pl.BlockSpec((None, IDX_TQ, IDX_HEADS), lambda b, i: (b, i, 0)),
                  pl.BlockSpec((None, nkb, LANES, IDX_TK), lambda b, i: (b, 0, 0, 0)),
                  pl.BlockSpec((None, nkb, LANES, IDX_TK), lambda b, i: (b, 0, 0, 0))],
        out_specs=pl.BlockSpec((None, L // LANES, IDX_TQ, LANES), lambda b, i: (b, 0, i, 0)),
        out_shape=jax.ShapeDtypeStruct((B, L // LANES, S, LANES), jnp.bfloat16),
        scratch_shapes=[pltpu.VMEM((L // LANES, IDX_TQ, LANES), jnp.int32)],
        compiler_params=_params("parallel", "parallel"), name="dsa_indexer",
    )(qi, wi, ke, ko)


DSA_TQ = 128
DSA_TK = 512
NEAR = 128


def _dsa_kernel(q_ref, kf_ref, vf_ref, mf_ref, k0_ref, v0_ref, m0_ref, k1_ref, v1_ref, m1_ref, bias_ref,
                o_ref, m_ref, l_ref, acc_ref, *, n_far_max):
    i = pl.program_id(1)
    k = pl.program_id(2)
    q0 = i * DSA_TQ
    n_far = (i + 2) // 4

    @pl.when(k == 0)
    def _():
        m_ref[...] = jnp.full_like(m_ref, NEG_INF)
        l_ref[...] = jnp.zeros_like(l_ref)
        acc_ref[...] = jnp.zeros_like(acc_ref)

    def update(h, s, v):
        m_prev = m_ref[h]
        m_new = jnp.maximum(m_prev, s.max(axis=1, keepdims=True))
        alpha = jnp.exp(m_prev - m_new)
        p = jnp.exp(s - m_new)
        l_ref[h] = alpha * l_ref[h] + p.sum(axis=1, keepdims=True)
        acc_ref[h] = alpha * acc_ref[h] + jnp.dot(p.astype(jnp.bfloat16), v, preferred_element_type=jnp.float32)
        m_ref[h] = m_new

    @pl.when(k < n_far)
    def _():
        col = k * DSA_TK + lax.broadcasted_iota(jnp.int32, (DSA_TQ, DSA_TK), 1)
        mk = jnp.concatenate([mf_ref[t] for t in range(DSA_TK // LANES)], axis=1).astype(jnp.float32)
        mk = jnp.where(col < q0 - NEAR, mk, NEG_INF)

        def head(h, carry):
            g = h // D_GROUP
            s = jnp.dot(q_ref[h], kf_ref[g], preferred_element_type=jnp.float32) + mk
            update(h, s, vf_ref[g])
            return carry

        lax.fori_loop(0, D_HEADS, head, 0)

    @pl.when(k == n_far_max)
    def _():
        mk0 = m0_ref[...].astype(jnp.float32) + jnp.where(i > 0, 0.0, NEG_INF)
        mk1 = m1_ref[...].astype(jnp.float32)

        def head(h, carry):
            g = h // D_GROUP
            q = q_ref[h]
            b = bias_ref[h]
            s0 = jnp.dot(q, k0_ref[g], preferred_element_type=jnp.float32) + (mk0 + b[:, :NEAR])
            update(h, s0, v0_ref[g])
            s1 = jnp.dot(q, k1_ref[g], preferred_element_type=jnp.float32) + (mk1 + b[:, NEAR:])
            update(h, s1, v1_ref[g])
            o_ref[h] = (acc_ref[h] / l_ref[h]).astype(o_ref.dtype)
            return carry

        lax.fori_loop(0, D_HEADS, head, 0)


def dsa_prompt(q, kT, v, mask, near_bias):
    B, _, S, _ = q.shape
    nq = S // DSA_TQ
    n_far_max = S // DSA_TK

    def far_blk(i, k):
        return jnp.minimum(k, jnp.maximum((i + 2) // 4 - 1, 0))

    def prev_blk(i):
        return jnp.maximum(i - 1, 0)

    return pl.pallas_call(
        functools.partial(_dsa_kernel, n_far_max=n_far_max), grid=(B, nq, n_far_max + 1),
        in_specs=[pl.BlockSpec((None, D_HEADS, DSA_TQ, D_HEAD_DIM), lambda b, i, k: (b, 0, i, 0)),
                  pl.BlockSpec((None, D_KV_HEADS, D_HEAD_DIM, DSA_TK), lambda b, i, k: (b, 0, 0, far_blk(i, k))),
                  pl.BlockSpec((None, D_KV_HEADS, DSA_TK, D_HEAD_DIM), lambda b, i, k: (b, 0, far_blk(i, k), 0)),
                  pl.BlockSpec((None, DSA_TK // LANES, DSA_TQ, LANES), lambda b, i, k: (b, far_blk(i, k), i, 0)),
                  pl.BlockSpec((None, D_KV_HEADS, D_HEAD_DIM, NEAR), lambda b, i, k: (b, 0, 0, prev_blk(i))),
                  pl.BlockSpec((None, D_KV_HEADS, NEAR, D_HEAD_DIM), lambda b, i, k: (b, 0, prev_blk(i), 0)),
                  pl.BlockSpec((None, None, DSA_TQ, LANES), lambda b, i, k: (b, prev_blk(i), i, 0)),
                  pl.BlockSpec((None, D_KV_HEADS, D_HEAD_DIM, NEAR), lambda b, i, k: (b, 0, 0, i)),
                  pl.BlockSpec((None, D_KV_HEADS, NEAR, D_HEAD_DIM), lambda b, i, k: (b, 0, i, 0)),
                  pl.BlockSpec((None, None, DSA_TQ, LANES), lambda b, i, k: (b, i, i, 0)),
                  pl.BlockSpec((D_HEADS, DSA_TQ, 2 * NEAR), lambda b, i, k: (0, 0, 0))],
        out_specs=pl.BlockSpec((None, D_HEADS, DSA_TQ, D_HEAD_DIM), lambda b, i, k: (b, 0, i, 0)),
        out_shape=jax.ShapeDtypeStruct((B, D_HEADS, S, D_HEAD_DIM), jnp.bfloat16),
        scratch_shapes=[pltpu.VMEM((D_HEADS, DSA_TQ, 1), jnp.float32), pltpu.VMEM((D_HEADS, DSA_TQ, 1), jnp.float32),
                        pltpu.VMEM((D_HEADS, DSA_TQ, D_HEAD_DIM), jnp.float32)],
        compiler_params=_params("parallel", "parallel", "arbitrary"), name="dsa_attention",
    )(q, kT, v, mask, kT, v, mask, kT, v, mask, near_bias)


def dsa_prompt_mix(dq, dk, dv, iq, ik, iw, rel_bias, n_sel):
    B, S, _ = dq.shape
    bf = jnp.bfloat16
    ikT = jnp.swapaxes(ik, 1, 2).astype(bf)
    z = jnp.zeros_like(ikT)
    ke = jnp.concatenate([ikT, z], axis=1).reshape(B, LANES, S // IDX_TK, IDX_TK).swapaxes(1, 2)
    ko = jnp.concatenate([z, ikT], axis=1).reshape(B, LANES, S // IDX_TK, IDX_TK).swapaxes(1, 2)
    w = iw * (IDX_HEADS ** -0.5) * (IDX_DIM ** -0.5)
    mask = indexer_mask(iq.astype(bf), w, ke, ko, n_sel)
    q = (dq * D_HEAD_DIM ** -0.5).astype(bf).reshape(B, S, D_HEADS, D_HEAD_DIM).transpose(0, 2, 1, 3)
    kT = dk.astype(bf).reshape(B, S, D_KV_HEADS, D_HEAD_DIM).transpose(0, 2, 3, 1)
    v = dv.astype(bf).reshape(B, S, D_KV_HEADS, D_HEAD_DIM).transpose(0, 2, 1, 3)
    rel = (jnp.arange(2 * NEAR)[None, :] - NEAR) - jnp.arange(DSA_TQ)[:, None]
    far_bucket = NUM_BUCKETS // 2 - 1
    near_bias = (rel_bias[t5_bucket(rel)] - rel_bias[far_bucket]).transpose(2, 0, 1)
    o = dsa_prompt(q, kT, v, mask, near_bias)
    return o.transpose(0, 2, 1, 3).reshape(B, S, D_HEADS * D_HEAD_DIM)


def dsa_attend(q, qi, wi, k, v, ki, q_pos, n_sel, rel_bias):
    B, Tq = q.shape[:2]
    L = k.shape[1]
    sc = jnp.einsum('bqhd,bsd->bqhs', qi, ki) * IDX_DIM ** -0.5
    score = jnp.einsum('bqhs,bqh->bqs', jax.nn.relu(sc), wi * IDX_HEADS ** -0.5)
    adm = (jnp.arange(L) // CHUNK)[None, :] <= (q_pos // CHUNK)[:, None]
    score = jnp.where(adm[None], score, -jnp.inf)
    _, idx = lax.top_k(score, n_sel)
    sel_ok = (idx // CHUNK) <= (q_pos // CHUNK)[None, :, None]
    gather = jax.vmap(lambda t, i: t[i])
    k_sel = gather(k, idx)
    v_sel = gather(v, idx)
    s = jnp.einsum('bqhgd,bqshd->bqhgs', q, k_sel) * D_HEAD_DIM ** -0.5
    bias = rel_bias[t5_bucket(idx - q_pos[None, :, None])]
    bias = bias.reshape(B, Tq, n_sel, D_KV_HEADS, D_GROUP).transpose(0, 1, 3, 4, 2)
    s = jnp.where(sel_ok[:, :, None, None, :], s + bias, NEG_INF)
    p = jax.nn.softmax(s, axis=-1)
    o = jnp.einsum('bqhgs,bqshd->bqhgd', p, v_sel)
    return o.reshape(B, Tq, D_HEADS * D_HEAD_DIM)


def odd_mix(z, pos, g_qa, g_kva, w_qb, w_kvb, rel_bias, cache):
    B, T, _ = z.shape
    q_lat, kv_lat, kpe_raw, dq, dk, dv, iq, ik, iw = split_cols(z, ODD_SPLITS)
    qn = rmsnorm(q_lat, g_qa).reshape(B * T, Q_LORA)
    q = matmul(qn, w_qb, tm=min(ROW_BLOCK, B * T)).reshape(B, T, C_HEADS, NOPE_DIM + ROPE_DIM)
    q_nope = q[..., :NOPE_DIM]
    q_pe = apply_rope(q[..., NOPE_DIM:], pos)
    lat = rmsnorm(kv_lat, g_kva)
    kpe = apply_rope(kpe_raw, pos)
    bf = jnp.bfloat16
    new = (lat, kpe, dk.reshape(B, T, D_KV_HEADS, D_HEAD_DIM), dv.reshape(B, T, D_KV_HEADS, D_HEAD_DIM), ik)
    if cache is None:
        kv = matmul(lat.reshape(B * T, KV_LORA), w_kvb).reshape(B, T, C_HEADS, NOPE_DIM + V_DIM)
        k_nope, v_c = kv[..., :NOPE_DIM], kv[..., NOPE_DIM:]
        pad = jnp.zeros((B, T, C_HEADS, MLA_QK - NOPE_DIM - ROPE_DIM), jnp.float32)
        qc = (jnp.concatenate([q_nope, q_pe, pad], axis=-1) * (NOPE_DIM + ROPE_DIM) ** -0.5).astype(bf)
        kc = jnp.concatenate([k_nope, jnp.broadcast_to(kpe[:, :, None, :], (B, T, C_HEADS, ROPE_DIM)), pad],
                             axis=-1).astype(bf)
        oc = mla_prompt(qc.reshape(B, T, C_HEADS * MLA_QK), kc.reshape(B, T, C_HEADS * MLA_QK),
                        v_c.astype(bf).reshape(B, T, C_HEADS * V_DIM))
        od = dsa_prompt_mix(dq, dk, dv, iq, ik, iw, rel_bias, min(TOPK_MAX, T // 4))
        return jnp.concatenate([oc, od], axis=-1), new
    c_lat, c_kpe, c_dk, c_dv, c_ik = cache
    lat_all = jnp.concatenate([c_lat, lat], axis=1)
    kpe_all = jnp.concatenate([c_kpe, kpe], axis=1)
    dk_all = jnp.concatenate([c_dk, new[2]], axis=1)
    dv_all = jnp.concatenate([c_dv, new[3]], axis=1)
    ik_all = jnp.concatenate([c_ik, ik], axis=1)
    L = lat_all.shape[1]
    kv = matmul(lat_all.reshape(B * L, KV_LORA), w_kvb).reshape(B, L, C_HEADS, NOPE_DIM + V_DIM)
    k_nope, v_c = kv[..., :NOPE_DIM], kv[..., NOPE_DIM:]
    oc = mla_attend(q_nope, q_pe, k_nope, kpe_all, v_c, pos)
    od = dsa_attend(dq.reshape(B, T, D_KV_HEADS, D_GROUP, D_HEAD_DIM), iq.reshape(B, T, IDX_HEADS, IDX_DIM), iw,
                    dk_all, dv_all, ik_all, pos, min(TOPK_MAX, L // 4), rel_bias)
    return jnp.concatenate([oc, od], axis=-1).astype(bf), new


def _final_norm_kernel(x_ref, g_ref, o_ref):
    x = x_ref[...]
    o_ref[...] = x * lax.rsqrt(jnp.mean(x * x, axis=-1, keepdims=True) + RMS_EPS) * g_ref[...]


def final_norm(x, g):
    row = pl.BlockSpec((ROW_BLOCK, D_MODEL), lambda i: (i, 0))
    return pl.pallas_call(
        _final_norm_kernel, grid=(N_ROW_BLOCKS,),
        in_specs=[row, pl.BlockSpec((1, D_MODEL), lambda i: (0, 0))], out_specs=row,
        out_shape=jax.ShapeDtypeStruct((N_TOK, D_MODEL), jnp.float32),
        compiler_params=_params("parallel"), name="final_norm")(x, g.reshape(1, D_MODEL))


def kernel(x_prompt, x_sample, c_prompt, c_sample, cache_a_k, cache_a_v, state_b_pool, cache_c_latent, cache_c_kpe, cache_d_k, cache_d_v, cache_d_idx, rel_bias, ada_mix_w, ada_mix_b, ada_ffn_w, ada_ffn_b, norm_mix, norm_ffn, norm_final, even_w_in, even_w_out, a_sinks, pool_w, pool_scale, odd_w_in, odd_w_out, c_q_norm, c_kv_norm, c_w_qb, c_w_kvb, moe_router_group, moe_router_expert, moe_w_gate, moe_w_up, moe_w_down):
    bf = jnp.bfloat16
    x = jnp.concatenate([x_prompt.reshape(N_PROMPT, D_MODEL), x_sample.reshape(N_SAMPLE, D_MODEL)], axis=0)
    c = jnp.concatenate([c_prompt, c_sample], axis=0)
    mod_mix = modulation_all(c, ada_mix_w, ada_mix_b)
    mod_ffn = modulation_all(c, ada_ffn_w, ada_ffn_b)
    pos_p = jnp.arange(SEQ)
    pos_s = PAST_LEN + jnp.arange(DEC_SEQ)
    w_router = jnp.concatenate(
        [moe_router_group, moe_router_expert,
         jnp.zeros((DEPTH, D_MODEL, ROUTER_PAD - N_GROUPS - N_EXPERTS), jnp.float32)], axis=-1)
    new_p = [[] for _ in range(8)]
    new_s = [[] for _ in range(8)]
    for l in range(DEPTH):
        j = l // 2
        shift, scale, gate = jnp.split(mod_mix[l], 3, axis=-1)
        h = norm_mod(x, norm_mix[l], scale, shift)
        if l % 2 == 0:
            z = matmul(h, even_w_in[j].astype(bf))
            zp = z[:N_PROMPT].reshape(BATCH, SEQ, -1)
            zs = z[N_PROMPT:].reshape(DEC_BATCH, DEC_SEQ, -1)
            mp, st_p = even_mix(zp, pos_p, a_sinks[j], pool_w[j], pool_scale[j], rel_bias, None)
            ms, st_s = even_mix(zs, pos_s, a_sinks[j], pool_w[j], pool_scale[j], rel_bias,
                                (cache_a_k[j], cache_a_v[j], state_b_pool[j]))
            off = 0
            w_out = even_w_out[j]
        else:
            w_in = jnp.pad(odd_w_in[j].astype(bf), ((0, 0), (0, ODD_IN_PAD - ODD_IN)))
            z = matmul(h, w_in, tn=ODD_IN_PAD // 9)[:, :ODD_IN]
            zp = z[:N_PROMPT].reshape(BATCH, SEQ, -1)
            zs = z[N_PROMPT:].reshape(DEC_BATCH, DEC_SEQ, -1)
            wq, wkv = c_w_qb[j].astype(bf), c_w_kvb[j].astype(bf)
            mp, st_p = odd_mix(zp, pos_p, c_q_norm[j], c_kv_norm[j], wq, wkv, rel_bias, None)
            ms, st_s = odd_mix(zs, pos_s, c_q_norm[j], c_kv_norm[j], wq, wkv, rel_bias,
                               (cache_c_latent[j], cache_c_kpe[j], cache_d_k[j], cache_d_v[j], cache_d_idx[j]))
            off = 3
            w_out = odd_w_out[j]
        for i, s in enumerate(st_p):
            new_p[off + i].append(s)
        for i, s in enumerate(st_s):
            new_s[off + i].append(s)
        mix = jnp.concatenate([mp.reshape(N_PROMPT, -1), ms.reshape(N_SAMPLE, -1)], axis=0).astype(bf)
        x = matmul_resid(mix, w_out.astype(bf), x, gate)
        shift, scale, gate = jnp.split(mod_ffn[l], 3, axis=-1)
        h, logits = norm_mod(x, norm_ffn[l], scale, shift, w_router[l])
        f = moe_ffn(h, logits, moe_w_gate[l].astype(bf), moe_w_up[l].astype(bf), moe_w_down[l].astype(bf))
        gate_tok = jnp.concatenate([jnp.repeat(gate[:BATCH], SEQ, axis=0),
                                    jnp.repeat(gate[BATCH:], DEC_SEQ, axis=0)], axis=0)
        x = x + gate_tok * f
    y = final_norm(x, norm_final)
    st_p = [jnp.stack(s) for s in new_p]
    st_s = [jnp.stack(s) for s in new_s]
    a_k_p, a_v_p, pool_p, lat_p, kpe_p, dk_p, dv_p, didx_p = st_p
    a_k_s, a_v_s, pool_s, lat_s, kpe_s, dk_s, dv_s, didx_s = st_s
    return (y[:N_PROMPT].reshape(BATCH, SEQ, D_MODEL), y[N_PROMPT:].reshape(DEC_BATCH, DEC_SEQ, D_MODEL),
            a_k_p, a_k_s, a_v_p, a_v_s, pool_p, pool_s, lat_p, lat_s,
            kpe_p, kpe_s, dk_p, dk_s, dv_p, dv_s, didx_p, didx_s)
```

```python
import functools
import math

import jax
import jax.numpy as jnp
from jax import lax
from jax.experimental import pallas as pl
from jax.experimental.pallas import tpu as pltpu

D_MODEL = 4096
BATCH = 2
SEQ = 8192
DEPTH = 4
DEC_BATCH = 16
DEC_SEQ = 32
PAST_LEN = 1024
CHUNK = 64
QBLOCK = 128
N_PAIR = DEPTH // 2
A_HEADS = 32
A_KV_HEADS = 8
A_HEAD_DIM = 64
A_GROUP = A_HEADS // A_KV_HEADS
WINDOW = 128
WIN_CHUNKS = WINDOW // CHUNK
POOL_WIDTH = 2048
POOL_GROUPS = 4
POOL_GROUP_WIDTH = POOL_WIDTH // POOL_GROUPS
POOL_WINDOWS = (2, 4, 8, 16)
POOL_HIST = max(POOL_WINDOWS) - 1
C_HEADS = 16
Q_LORA = 1024
KV_LORA = 512
NOPE_DIM = 128
ROPE_DIM = 64
V_DIM = 128
ROPE_THETA = 10000.0
D_HEADS = 32
D_KV_HEADS = 8
D_HEAD_DIM = 64
D_GROUP = D_HEADS // D_KV_HEADS
IDX_HEADS = 32
IDX_DIM = 64
TOPK_MAX = 256
NUM_BUCKETS = 32
MAX_DISTANCE = 128
N_GROUPS = 8
EXPERTS_PER_GROUP = 8
N_EXPERTS = N_GROUPS * EXPERTS_PER_GROUP
TOP_K = 2
D_EXPERT = 512
RMS_EPS = 1e-6
NEG_INF = -1e30

EVEN_SPLITS = (A_HEADS * A_HEAD_DIM, A_KV_HEADS * A_HEAD_DIM, A_KV_HEADS * A_HEAD_DIM, POOL_WIDTH)
ODD_SPLITS = (Q_LORA, KV_LORA, ROPE_DIM, D_HEADS * D_HEAD_DIM, D_KV_HEADS * D_HEAD_DIM,
              D_KV_HEADS * D_HEAD_DIM, IDX_HEADS * IDX_DIM, IDX_DIM, IDX_HEADS)
ODD_IN = sum(ODD_SPLITS)
ODD_IN_PAD = -(-ODD_IN // 128) * 128
ODD_ORDER = (3, 6, 0, 1, 4, 5, 2, 7, 8)


def odd_in_weight(w):
    segs = split_cols(w, ODD_SPLITS)
    out = jnp.concatenate([segs[s] for s in ODD_ORDER], axis=-1).astype(jnp.bfloat16)
    return jnp.pad(out, ((0, 0), (0, ODD_IN_PAD - ODD_IN)))


def odd_split(z):
    parts = split_cols(z, [ODD_SPLITS[s] for s in ODD_ORDER])
    out = [None] * len(ODD_ORDER)
    for pos, s in enumerate(ODD_ORDER):
        out[s] = parts[pos]
    return out

N_PROMPT = BATCH * SEQ
N_SAMPLE = DEC_BATCH * DEC_SEQ
N_TOK = N_PROMPT + N_SAMPLE
N_COND = BATCH + DEC_BATCH

ROW_BLOCK = 512
ROW_GROUPS = ROW_BLOCK // DEC_SEQ
N_ROW_BLOCKS = N_TOK // ROW_BLOCK
MOE_ROWS = 256
ROUTER_PAD = 128
VMEM_LIMIT = 56 * 1024 * 1024

assert N_SAMPLE == ROW_BLOCK and SEQ % ROW_BLOCK == 0 and N_TOK % ROW_BLOCK == 0


def _params(*sem):
    return pltpu.CompilerParams(dimension_semantics=sem, vmem_limit_bytes=VMEM_LIMIT)


def split_cols(z, sizes):
    out, o = [], 0
    for s in sizes:
        out.append(z[..., o:o + s])
        o += s
    return out


def _modulation_kernel(c_ref, w_ref, b_ref, o_ref):
    c = c_ref[...]
    a = (c * jax.nn.sigmoid(c)).astype(jnp.bfloat16)
    o_ref[0] = jnp.dot(a, w_ref[0].astype(jnp.bfloat16), preferred_element_type=jnp.float32) + b_ref[0]


def modulation_all(c, w, b):
    tn = 1024
    depth, d, n = w.shape
    return pl.pallas_call(
        _modulation_kernel,
        grid=(depth, n // tn),
        in_specs=[pl.BlockSpec((N_COND, d), lambda l, j: (0, 0)),
                  pl.BlockSpec((1, d, tn), lambda l, j: (l, 0, j)),
                  pl.BlockSpec((1, 1, tn), lambda l, j: (l, 0, j))],
        out_specs=pl.BlockSpec((1, N_COND, tn), lambda l, j: (l, 0, j)),
        out_shape=jax.ShapeDtypeStruct((depth, N_COND, n), jnp.float32),
        compiler_params=_params("parallel", "parallel"),
        name="modulation",
    )(c, w, b.reshape(depth, 1, n))


def expand_rows(m, block_rows=ROW_BLOCK):
    d = m.shape[-1]
    groups = block_rows // DEC_SEQ
    p = jnp.broadcast_to(m[:BATCH, None, None, :], (BATCH, SEQ // block_rows, groups, d))
    return jnp.concatenate([p.reshape(N_PROMPT // block_rows, groups, d),
                            m[BATCH:].reshape(N_SAMPLE // block_rows, groups, d)], axis=0)


def _norm_mod(x_ref, g_ref, scale_ref, shift_ref):
    x = x_ref[...]
    y = x * lax.rsqrt(jnp.mean(x * x, axis=-1, keepdims=True) + RMS_EPS) * g_ref[...]
    y = y.reshape(ROW_GROUPS, DEC_SEQ, D_MODEL)
    y = y * (1.0 + scale_ref[0][:, None, :]) + shift_ref[0][:, None, :]
    return y.reshape(ROW_BLOCK, D_MODEL)


def _norm_mod_kernel(x_ref, g_ref, scale_ref, shift_ref, h_ref):
    h_ref[...] = _norm_mod(x_ref, g_ref, scale_ref, shift_ref).astype(jnp.bfloat16)


def _norm_mod_router_kernel(x_ref, g_ref, scale_ref, shift_ref, whi_ref, wlo_ref, h_ref, lg_ref):
    h = _norm_mod(x_ref, g_ref, scale_ref, shift_ref)
    hi = h.astype(jnp.bfloat16)
    lo = (h - hi.astype(jnp.float32)).astype(jnp.bfloat16)
    h_ref[...] = h
    lg_ref[...] = (jnp.dot(hi, whi_ref[...], preferred_element_type=jnp.float32)
                   + jnp.dot(lo, whi_ref[...], preferred_element_type=jnp.float32)
                   + jnp.dot(hi, wlo_ref[...], preferred_element_type=jnp.float32))


def norm_mod(x, g, scale, shift, w_router=None):
    row = pl.BlockSpec((ROW_BLOCK, D_MODEL), lambda i: (i, 0))
    grp = pl.BlockSpec((1, ROW_GROUPS, D_MODEL), lambda i: (i, 0, 0))
    in_specs = [row, pl.BlockSpec((1, D_MODEL), lambda i: (0, 0)), grp, grp]
    h_shape = jax.ShapeDtypeStruct((N_TOK, D_MODEL), jnp.bfloat16)
    args = (x, g.reshape(1, D_MODEL), expand_rows(scale), expand_rows(shift))
    if w_router is None:
        return pl.pallas_call(
            _norm_mod_kernel, grid=(N_ROW_BLOCKS,), in_specs=in_specs, out_specs=row, out_shape=h_shape,
            compiler_params=_params("parallel"), name="norm_mod")(*args)
    w_hi = w_router.astype(jnp.bfloat16)
    w_lo = (w_router - w_hi.astype(jnp.float32)).astype(jnp.bfloat16)
    wspec = pl.BlockSpec((D_MODEL, ROUTER_PAD), lambda i: (0, 0))
    return pl.pallas_call(
        _norm_mod_router_kernel, grid=(N_ROW_BLOCKS,), in_specs=in_specs + [wspec, wspec],
        out_specs=(row, pl.BlockSpec((ROW_BLOCK, ROUTER_PAD), lambda i: (i, 0))),
        out_shape=(jax.ShapeDtypeStruct((N_TOK, D_MODEL), jnp.float32),
                   jax.ShapeDtypeStruct((N_TOK, ROUTER_PAD), jnp.float32)),
        compiler_params=_params("parallel"), name="norm_mod_router")(*args, w_hi, w_lo)


def _matmul_kernel(a_ref, b_ref, o_ref):
    o_ref[...] = jnp.dot(a_ref[...].astype(jnp.bfloat16), b_ref[...],
                         preferred_element_type=jnp.float32).astype(o_ref.dtype)


def matmul(a, b, *, tm=ROW_BLOCK, tn=512, out_dtype=jnp.float32):
    m, k = a.shape
    n = b.shape[1]
    tm, tn = min(tm, m), min(tn, n)
    assert m % tm == 0 and n % tn == 0, (a.shape, b.shape, tm, tn)
    return pl.pallas_call(
        _matmul_kernel, grid=(m // tm, n // tn),
        in_specs=[pl.BlockSpec((tm, k), lambda i, j: (i, 0)), pl.BlockSpec((k, tn), lambda i, j: (0, j))],
        out_specs=pl.BlockSpec((tm, tn), lambda i, j: (i, j)),
        out_shape=jax.ShapeDtypeStruct((m, n), out_dtype),
        compiler_params=_params("parallel", "parallel"), name="matmul",
    )(a, b)


def _matmul_resid_kernel(a_ref, b_ref, x_ref, gate_ref, o_ref):
    y = jnp.dot(a_ref[...], b_ref[...], preferred_element_type=jnp.float32)
    tn = y.shape[-1]
    y = y.reshape(ROW_GROUPS, DEC_SEQ, tn) * gate_ref[0][:, None, :]
    o_ref[...] = x_ref[...] + y.reshape(ROW_BLOCK, tn)


def matmul_resid(a, b, x, gate, *, tn=512):
    k = a.shape[1]
    return pl.pallas_call(
        _matmul_resid_kernel, grid=(N_ROW_BLOCKS, D_MODEL // tn),
        in_specs=[pl.BlockSpec((ROW_BLOCK, k), lambda i, j: (i, 0)),
                  pl.BlockSpec((k, tn), lambda i, j: (0, j)),
                  pl.BlockSpec((ROW_BLOCK, tn), lambda i, j: (i, j)),
                  pl.BlockSpec((1, ROW_GROUPS, tn), lambda i, j: (i, 0, j))],
        out_specs=pl.BlockSpec((ROW_BLOCK, tn), lambda i, j: (i, j)),
        out_shape=jax.ShapeDtypeStruct((N_TOK, D_MODEL), jnp.float32),
        compiler_params=_params("parallel", "parallel"), name="matmul_resid",
    )(a, b, x, expand_rows(gate))


GATHER_UNROLL = 8


def _start_row_gather(idx_ref, src_hbm, dst_buf, sem, n_rows):
    def body(r, carry):
        pltpu.make_async_copy(src_hbm.at[pl.ds(idx_ref[0, r], 1)], dst_buf.at[pl.ds(r, 1)], sem).start()
        return carry

    lax.fori_loop(0, n_rows, body, 0, unroll=GATHER_UNROLL)


def _wait_row_gather(src_hbm, dst_buf, sem, n_rows):
    pltpu.make_async_copy(src_hbm.at[pl.ds(0, n_rows)], dst_buf, sem).wait()


def _expert_kernel(block_e_ref, n_active_ref, tok_ref, tok_next_ref, h_hbm, wg_ref, wu_ref, wd_ref, gate_ref,
                   o_ref, xbuf, sem):
    i = pl.program_id(0)
    n_act = n_active_ref[0]
    slot = i % 2

    @pl.when(i == 0)
    def _():
        _start_row_gather(tok_ref, h_hbm, xbuf.at[0], sem.at[0], MOE_ROWS)

    @pl.when(i + 1 < n_act)
    def _():
        _start_row_gather(tok_next_ref, h_hbm, xbuf.at[1 - slot], sem.at[1 - slot], MOE_ROWS)

    @pl.when(i < n_act)
    def _():
        _wait_row_gather(h_hbm, xbuf.at[slot], sem.at[slot], MOE_ROWS)
        x = xbuf[slot].astype(jnp.bfloat16)
        g = jnp.dot(x, wg_ref[0], preferred_element_type=jnp.float32)
        u = jnp.dot(x, wu_ref[0], preferred_element_type=jnp.float32)
        a = (g * jax.nn.sigmoid(g) * u).astype(jnp.bfloat16)
        o_ref[...] = jnp.dot(a, wd_ref[0], preferred_element_type=jnp.float32) * gate_ref[...]

    @pl.when(i >= n_act)
    def _():
        o_ref[...] = jnp.zeros_like(o_ref)


def expert_blocks(h, tok_of_row, gate_of_row, block_e, n_active, w_gate, w_up, w_down):
    n_blocks = tok_of_row.shape[0]

    def smem_blk(index_map):
        return pl.BlockSpec((None, 1, MOE_ROWS), index_map, memory_space=pltpu.SMEM)

    return pl.pallas_call(
        _expert_kernel,
        grid_spec=pltpu.PrefetchScalarGridSpec(
            num_scalar_prefetch=2, grid=(n_blocks,),
            in_specs=[smem_blk(lambda i, be, na: (i, 0, 0)),
                      smem_blk(lambda i, be, na: (jnp.minimum(i + 1, n_blocks - 1), 0, 0)),
                      pl.BlockSpec(memory_space=pl.ANY),
                      pl.BlockSpec((1, D_MODEL, D_EXPERT), lambda i, be, na: (be[i], 0, 0)),
                      pl.BlockSpec((1, D_MODEL, D_EXPERT), lambda i, be, na: (be[i], 0, 0)),
                      pl.BlockSpec((1, D_EXPERT, D_MODEL), lambda i, be, na: (be[i], 0, 0)),
                      pl.BlockSpec((MOE_ROWS, 1), lambda i, be, na: (i, 0))],
            out_specs=pl.BlockSpec((MOE_ROWS, D_MODEL), lambda i, be, na: (i, 0)),
            scratch_shapes=[pltpu.VMEM((2, MOE_ROWS, D_MODEL), jnp.float32), pltpu.SemaphoreType.DMA((2,))]),
        out_shape=jax.ShapeDtypeStruct((n_blocks * MOE_ROWS, D_MODEL), jnp.float32),
        compiler_params=_params("arbitrary"), name="moe_experts",
    )(block_e, n_active, tok_of_row, tok_of_row, h, w_gate, w_up, w_down, gate_of_row)


COMBINE_ROWS = 256
COMBINE_GROUPS = COMBINE_ROWS // DEC_SEQ


def _combine_kernel(pos_ref, pos_next_ref, y_hbm, x_ref, gate_ref, o_ref, ybuf, sem):
    i = pl.program_id(0)
    n = pl.num_programs(0)
    slot = i % 2
    n_rows = TOP_K * COMBINE_ROWS

    @pl.when(i == 0)
    def _():
        _start_row_gather(pos_ref, y_hbm, ybuf.at[0], sem.at[0], n_rows)

    @pl.when(i + 1 < n)
    def _():
        _start_row_gather(pos_next_ref, y_hbm, ybuf.at[1 - slot], sem.at[1 - slot], n_rows)

    _wait_row_gather(y_hbm, ybuf.at[slot], sem.at[slot], n_rows)
    f = ybuf[slot, :COMBINE_ROWS] + ybuf[slot, COMBINE_ROWS:]
    f = f.reshape(COMBINE_GROUPS, DEC_SEQ, D_MODEL) * gate_ref[0][:, None, :]
    o_ref[...] = x_ref[...] + f.reshape(COMBINE_ROWS, D_MODEL)


def moe_combine(y, pos, x, gate):
    nb = N_TOK // COMBINE_ROWS

    def smem_blk(index_map):
        return pl.BlockSpec((None, 1, TOP_K * COMBINE_ROWS), index_map, memory_space=pltpu.SMEM)

    return pl.pallas_call(
        _combine_kernel, grid=(nb,),
        in_specs=[smem_blk(lambda i: (i, 0, 0)), smem_blk(lambda i: (jnp.minimum(i + 1, nb - 1), 0, 0)),
                  pl.BlockSpec(memory_space=pl.ANY),
                  pl.BlockSpec((COMBINE_ROWS, D_MODEL), lambda i: (i, 0)),
                  pl.BlockSpec((1, COMBINE_GROUPS, D_MODEL), lambda i: (i, 0, 0))],
        out_specs=pl.BlockSpec((COMBINE_ROWS, D_MODEL), lambda i: (i, 0)),
        out_shape=jax.ShapeDtypeStruct((N_TOK, D_MODEL), jnp.float32),
        scratch_shapes=[pltpu.VMEM((2, TOP_K * COMBINE_ROWS, D_MODEL), jnp.float32),
                        pltpu.SemaphoreType.DMA((2,))],
        compiler_params=_params("arbitrary"), name="moe_combine",
    )(pos, pos, y, x, expand_rows(gate, COMBINE_ROWS))


def moe_ffn(x, h, logits, gate, w_gate, w_up, w_down):
    n = h.shape[0]
    lg = logits[:, :N_GROUPS]
    grp = jnp.argmax(lg, axis=-1)
    p_grp = jnp.take_along_axis(jax.nn.softmax(lg, axis=-1), grp[:, None], axis=-1)
    le = logits[:, N_GROUPS:N_GROUPS + N_EXPERTS].reshape(n, N_GROUPS, EXPERTS_PER_GROUP)
    le = jnp.take_along_axis(le, grp[:, None, None], axis=1)[:, 0]
    pe, ie = lax.top_k(jax.nn.softmax(le, axis=-1), TOP_K)
    gates = p_grp * pe / pe.sum(-1, keepdims=True)
    expert = (grp[:, None] * EXPERTS_PER_GROUP + ie).astype(jnp.int32)
    m = n * TOP_K
    flat_e = expert.reshape(m)
    onehot = (flat_e[:, None] == jnp.arange(N_EXPERTS, dtype=jnp.int32)[None, :]).astype(jnp.int32)
    csum = jnp.cumsum(onehot, axis=0)
    rank = jnp.take_along_axis(csum, flat_e[:, None], axis=1)[:, 0] - 1
    sizes = csum[-1]
    padded = (sizes + MOE_ROWS - 1) // MOE_ROWS * MOE_ROWS
    pend = jnp.cumsum(padded)
    pstart = pend - padded
    dest = (pstart[flat_e] + rank).astype(jnp.int32)
    n_blocks = -(-m // MOE_ROWS) + N_EXPERTS
    total = n_blocks * MOE_ROWS
    tok_of_row = jnp.zeros((total,), jnp.int32).at[dest].set(jnp.arange(m, dtype=jnp.int32) // TOP_K)
    gate_of_row = jnp.zeros((total,), jnp.float32).at[dest].set(gates.reshape(m))
    block_start = jnp.arange(n_blocks, dtype=jnp.int32) * MOE_ROWS
    block_e = jnp.minimum(jnp.searchsorted(pend, block_start, side='right'), N_EXPERTS - 1).astype(jnp.int32)
    n_active = (pend[-1] // MOE_ROWS).astype(jnp.int32).reshape(1)
    y = expert_blocks(h, tok_of_row.reshape(n_blocks, 1, MOE_ROWS), gate_of_row.reshape(total, 1), block_e, n_active,
                      w_gate, w_up, w_down)
    nb = n // COMBINE_ROWS
    pos = dest.reshape(nb, COMBINE_ROWS, TOP_K).transpose(0, 2, 1).reshape(nb, 1, TOP_K * COMBINE_ROWS)
    return moe_combine(y, pos, x, gate)


def t5_bucket(rel):
    half = NUM_BUCKETS // 2
    max_exact = half // 2
    ret = jnp.where(rel > 0, half, 0)
    n = jnp.abs(rel)
    nf = jnp.maximum(n, 1).astype(jnp.float32)
    large = max_exact + (jnp.log(nf / max_exact) / math.log(MAX_DISTANCE / max_exact)
                         * (half - max_exact)).astype(jnp.int32)
    large = jnp.minimum(large, half - 1)
    return ret + jnp.where(n < max_exact, n, large)


def head_bias(rel, table, n_kv, group):
    b = table[t5_bucket(rel)]
    return jnp.moveaxis(b, -1, 0).reshape((n_kv, group) + rel.shape)


def apply_rope(x, pos):
    inv = jnp.exp(-math.log(ROPE_THETA) * jnp.arange(0, ROPE_DIM, 2, dtype=jnp.float32) / ROPE_DIM)
    ang = pos.astype(jnp.float32)[:, None] * inv[None, :]
    ang = ang.reshape(ang.shape[:1] + (1,) * (x.ndim - 3) + ang.shape[1:])
    cos, sin = jnp.cos(ang), jnp.sin(ang)
    x1, x2 = jnp.split(x.astype(jnp.float32), 2, axis=-1)
    return jnp.concatenate([x1 * cos - x2 * sin, x2 * cos + x1 * sin], axis=-1)


def rmsnorm(x, g):
    return x * lax.rsqrt(jnp.mean(x * x, axis=-1, keepdims=True) + RMS_EPS) * g


def sink_attention(q, k, v, bias, valid, sinks):
    n, tq = q.shape[:2]
    s = jnp.einsum('nqkgd,nskd->nkgqs', q, k) * A_HEAD_DIM ** -0.5 + bias
    s = jnp.where(valid[:, None, None], s, NEG_INF)
    sink = sinks.reshape(1, A_KV_HEADS, A_GROUP, 1, 1)
    mx = jnp.maximum(s.max(-1, keepdims=True), sink)
    e = jnp.exp(s - mx)
    p = e / (e.sum(-1, keepdims=True) + jnp.exp(sink - mx))
    o = jnp.einsum('nkgqs,nskd->nqkgd', p, v)
    return o.reshape(n, tq, A_HEADS * A_HEAD_DIM)


def window_attention_prompt(q, k, v, sinks, rel_bias):
    B, S = q.shape[:2]
    nc = S // CHUNK
    band = (WIN_CHUNKS + 1) * CHUNK

    def banded(t):
        tc = t.reshape(B, nc, CHUNK, A_KV_HEADS, A_HEAD_DIM)
        tp = jnp.concatenate([jnp.zeros_like(tc[:, :WIN_CHUNKS]), tc], axis=1)
        tb = jnp.concatenate([tp[:, j:j + nc] for j in range(WIN_CHUNKS + 1)], axis=2)
        return tb.reshape(B * nc, band, A_KV_HEADS, A_HEAD_DIM)

    qi = jnp.arange(CHUNK)[:, None]
    kr = jnp.arange(band)[None, :]
    rel = kr - WIN_CHUNKS * CHUNK - qi
    key_chunk = jnp.arange(nc)[:, None] - WIN_CHUNKS + kr // CHUNK
    valid = jnp.broadcast_to((key_chunk >= 0)[None, :, None, :], (B, nc, CHUNK, band)).reshape(B * nc, CHUNK, band)
    qb = q.reshape(B * nc, CHUNK, A_KV_HEADS, A_GROUP, A_HEAD_DIM)
    o = sink_attention(qb, banded(k), banded(v), head_bias(rel, rel_bias, A_KV_HEADS, A_GROUP), valid, sinks)
    return o.reshape(B, S, A_HEADS * A_HEAD_DIM)


def window_attention_sample(q, k_all, v_all, pos, W, sinks, rel_bias):
    B, T = q.shape[:2]
    k_pos = jnp.concatenate([PAST_LEN - W + jnp.arange(W), pos])
    qc = (pos // CHUNK)[:, None]
    kc = (k_pos // CHUNK)[None, :]
    valid = (kc <= qc) & (kc >= qc - WIN_CHUNKS)
    valid = jnp.broadcast_to(valid[None], (B,) + valid.shape)
    bias = head_bias(k_pos[None, :] - pos[:, None], rel_bias, A_KV_HEADS, A_GROUP)
    return sink_attention(q, k_all, v_all, bias, valid, sinks)


def pool_mix(u_ext, pos, n_hist, pool_w, pool_scale):
    B, L, P = u_ext.shape
    T = L - n_hist
    cs = jnp.cumsum(u_ext, axis=1)
    cs = jnp.concatenate([jnp.zeros_like(cs[:, :1]), cs], axis=1)
    end = n_hist + jnp.arange(T) + 1
    outs = []
    for g, w in enumerate(POOL_WINDOWS):
        lo, hi = g * POOL_GROUP_WIDTH, (g + 1) * POOL_GROUP_WIDTH
        csg = cs[..., lo:hi]
        start = jnp.maximum(end - w, 0)
        cnt = jnp.minimum(w, pos + 1).astype(jnp.float32)[None, :, None]
        mean = (csg[:, end] - csg[:, start]) / cnt
        outs.append(mean - u_ext[:, n_hist:, lo:hi])
    d = jnp.stack(outs, axis=2)
    y = jnp.einsum('btgc,gcd->btgd', d, pool_w).reshape(B, T, P)
    return y * pool_scale


def even_mix(z, pos, sinks, pool_w, pool_scale, rel_bias, cache):
    B, T, _ = z.shape
    q, k, v, u = split_cols(z, EVEN_SPLITS)
    q = q.reshape(B, T, A_KV_HEADS, A_GROUP, A_HEAD_DIM)
    k = k.reshape(B, T, A_KV_HEADS, A_HEAD_DIM)
    v = v.reshape(B, T, A_KV_HEADS, A_HEAD_DIM)
    if cache is None:
        a = window_attention_prompt(q, k, v, sinks, rel_bias)
        p = pool_mix(u, pos, 0, pool_w, pool_scale)
        new = (k[:, -WINDOW:], v[:, -WINDOW:], u[:, -POOL_HIST:])
    else:
        k_cache, v_cache, u_cache = cache
        W = k_cache.shape[1]
        k_all = jnp.concatenate([k_cache, k], axis=1)
        v_all = jnp.concatenate([v_cache, v], axis=1)
        a = window_attention_sample(q, k_all, v_all, pos, W, sinks, rel_bias)
        u_all = jnp.concatenate([u_cache, u], axis=1)
        p = pool_mix(u_all, pos, POOL_HIST, pool_w, pool_scale)
        new = (k_all[:, -W:], v_all[:, -W:], u_all[:, -POOL_HIST:])
    return jnp.concatenate([a, p], axis=-1), new


def mla_attend(q_nope, q_pe, k_nope, k_pe, v, q_pos):
    L = k_nope.shape[1]
    s = (jnp.einsum('bqhd,bshd->bhqs', q_nope, k_nope)
         + jnp.einsum('bqhr,bsr->bhqs', q_pe, k_pe)) * (NOPE_DIM + ROPE_DIM) ** -0.5
    ok = (jnp.arange(L) // CHUNK)[None, :] <= (q_pos // CHUNK)[:, None]
    p = jax.nn.softmax(jnp.where(ok, s, NEG_INF), axis=-1)
    o = jnp.einsum('bhqs,bshd->bqhd', p, v)
    return o.reshape(o.shape[0], o.shape[1], C_HEADS * V_DIM)


MLA_T = 512
MLA_QK = 256
MLA_V = 128
MLA_HP = 2


def _mla_kernel(q_ref, k_ref, v_ref, o_ref, m_ref, l_ref, acc_ref):
    i = pl.program_id(2)
    k = pl.program_id(3)

    @pl.when(k == 0)
    def _():
        m_ref[...] = jnp.full_like(m_ref, NEG_INF)
        l_ref[...] = jnp.zeros_like(l_ref)
        acc_ref[...] = jnp.zeros_like(acc_ref)

    def step(diagonal):
        for hh in range(MLA_HP):
            qs = slice(hh * MLA_QK, (hh + 1) * MLA_QK)
            vs = slice(hh * MLA_V, (hh + 1) * MLA_V)
            s = lax.dot_general(q_ref[:, qs], k_ref[:, qs], (((1,), (1,)), ((), ())),
                                preferred_element_type=jnp.float32)
            if diagonal:
                row = lax.broadcasted_iota(jnp.int32, s.shape, 0) // CHUNK
                col = lax.broadcasted_iota(jnp.int32, s.shape, 1) // CHUNK
                s = jnp.where(col <= row, s, NEG_INF)
            m_prev = m_ref[hh]
            m_new = jnp.maximum(m_prev, s.max(axis=1, keepdims=True))
            alpha = jnp.exp(m_prev - m_new)
            p = jnp.exp(s - m_new)
            l_ref[hh] = alpha * l_ref[hh] + p.sum(axis=1, keepdims=True)
            acc_ref[:, vs] = alpha * acc_ref[:, vs] + jnp.dot(p.astype(jnp.bfloat16), v_ref[:, vs],
                                                              preferred_element_type=jnp.float32)
            m_ref[hh] = m_new

    @pl.when(k < i)
    def _():
        step(False)

    @pl.when(k == i)
    def _():
        step(True)
        for hh in range(MLA_HP):
            vs = slice(hh * MLA_V, (hh + 1) * MLA_V)
            o_ref[:, vs] = (acc_ref[:, vs] / l_ref[hh]).astype(o_ref.dtype)


def mla_prompt(qc, kc, v):
    B, S, _ = qc.shape
    n = S // MLA_T
    qk_w, v_w = MLA_HP * MLA_QK, MLA_HP * MLA_V
    return pl.pallas_call(
        _mla_kernel, grid=(B, C_HEADS // MLA_HP, n, n),
        in_specs=[pl.BlockSpec((None, MLA_T, qk_w), lambda b, h, i, k: (b, i, h)),
                  pl.BlockSpec((None, MLA_T, qk_w), lambda b, h, i, k: (b, jnp.minimum(k, i), h)),
                  pl.BlockSpec((None, MLA_T, v_w), lambda b, h, i, k: (b, jnp.minimum(k, i), h))],
        out_specs=pl.BlockSpec((None, MLA_T, v_w), lambda b, h, i, k: (b, i, h)),
        out_shape=jax.ShapeDtypeStruct((B, S, C_HEADS * MLA_V), jnp.bfloat16),
        scratch_shapes=[pltpu.VMEM((MLA_HP, MLA_T, 1), jnp.float32), pltpu.VMEM((MLA_HP, MLA_T, 1), jnp.float32),
                        pltpu.VMEM((MLA_T, v_w), jnp.float32)],
        compiler_params=_params("parallel", "parallel", "parallel", "arbitrary"), name="mla_prompt",
    )(qc, kc, v)


IDX_TQ = 128
IDX_TK = 256
LANES = 128
INT_MIN = -2 ** 31


def _indexer_kernel(q_ref, w_ref, ke_ref, ko_ref, mask_ref, key_ref, *, n_sel):
    i = pl.program_id(1)
    q0 = i * IDX_TQ
    n_lane_tiles = mask_ref.shape[0]
    n_kb = (q0 + IDX_TQ + IDX_TK - 1) // IDX_TK
    sub = IDX_TK // LANES
    row_chunk = (q0 + lax.broadcasted_iota(jnp.int32, (IDX_TQ, LANES), 0)) // CHUNK
    lane_iota = lax.broadcasted_iota(jnp.int32, (IDX_TQ, LANES), 1)

    def score_tile(kb, carry):
        ke = ke_ref[kb]
        ko = ko_ref[kb]
        acc = jnp.zeros((IDX_TQ, IDX_TK), jnp.float32)
        for p in range(IDX_HEADS // 2):
            qp = q_ref[:, p * LANES:(p + 1) * LANES]
            se = jnp.dot(qp, ke, preferred_element_type=jnp.float32)
            so = jnp.dot(qp, ko, preferred_element_type=jnp.float32)
            acc = acc + jnp.maximum(se, 0.0) * w_ref[:, 2 * p:2 * p + 1]
            acc = acc + jnp.maximum(so, 0.0) * w_ref[:, 2 * p + 1:2 * p + 2]
        for t in range(sub):
            j = kb * sub + t
            a = acc[:, t * LANES:(t + 1) * LANES]
            adm = (j * LANES + lane_iota) // CHUNK <= row_chunk
            a = jnp.where(adm, a, -jnp.inf)
            bits = pltpu.bitcast(a, jnp.int32)
            key_ref[j] = bits ^ ((bits >> 31) & jnp.int32(0x7FFFFFFF))
        return carry

    lax.fori_loop(0, n_kb, score_tile, 0)
    n_tiles = n_kb * sub

    def count_ge(cand):
        def body(j, cnt):
            return cnt + jnp.where(key_ref[j] >= cand, 1, 0)
        cnt = lax.fori_loop(0, n_tiles, body, jnp.zeros((IDX_TQ, LANES), jnp.int32))
        return cnt.sum(axis=1, keepdims=True)

    thr = jnp.where(count_ge(jnp.zeros((IDX_TQ, 1), jnp.int32)) >= n_sel,
                    jnp.zeros((IDX_TQ, 1), jnp.int32), jnp.full((IDX_TQ, 1), INT_MIN, jnp.int32))

    def bit_step(b, thr):
        cand = thr | (jnp.int32(1) << (30 - b))
        return jnp.where(count_ge(cand) >= n_sel, cand, thr)

    thr = lax.fori_loop(0, 31, bit_step, thr)

    def write_mask(j, carry):
        adm = (j * LANES + lane_iota) // CHUNK <= row_chunk
        sel = jnp.logical_and(key_ref[j] >= thr, adm)
        mask_ref[j] = jnp.where(sel, 0.0, NEG_INF).astype(mask_ref.dtype)
        return carry

    lax.fori_loop(0, n_tiles, write_mask, 0)

    def write_rest(j, carry):
        mask_ref[j] = jnp.full((IDX_TQ, LANES), NEG_INF, mask_ref.dtype)
        return carry

    lax.fori_loop(n_tiles, n_lane_tiles, write_rest, 0)


def indexer_mask(qi, wi, ke, ko, n_sel):
    B, S, _ = qi.shape
    nkb = ke.shape[1]
    L = nkb * IDX_TK
    return pl.pallas_call(
        functools.partial(_indexer_kernel, n_sel=n_sel), grid=(B, S // IDX_TQ),
        in_specs=[pl.BlockSpec((None, IDX_TQ, IDX_HEADS * IDX_DIM), lambda b, i: (b, i, 0)),
                  pl.BlockSpec((None, IDX_TQ, IDX_HEADS), lambda b, i: (b, i, 0)),
                  pl.BlockSpec((None, nkb, LANES, IDX_TK), lambda b, i: (b, 0, 0, 0)),
                  pl.BlockSpec((None, nkb, LANES, IDX_TK), lambda b, i: (b, 0, 0, 0))],
        out_specs=pl.BlockSpec((None, L // LANES, IDX_TQ, LANES), lambda b, i: (b, 0, i, 0)),
        out_shape=jax.ShapeDtypeStruct((B, L // LANES, S, LANES), jnp.bfloat16),
        scratch_shapes=[pltpu.VMEM((L // LANES, IDX_TQ, LANES), jnp.int32)],
        compiler_params=_params("parallel", "parallel"), name="dsa_indexer",
    )(qi, wi, ke, ko)


DSA_TQ = 128
DSA_TK = 512
NEAR = 128


GROUP_ROWS = D_GROUP * DSA_TQ


def _dsa_kernel(q_ref, kf_ref, vf_ref, mf_ref, k0_ref, v0_ref, m0_ref, k1_ref, v1_ref, m1_ref, bias_ref,
                o_ref, m_ref, l_ref, acc_ref, *, n_far_max):
    i = pl.program_id(1)
    k = pl.program_id(2)
    q0 = i * DSA_TQ
    n_far = (i + 2) // 4

    @pl.when(k == 0)
    def _():
        m_ref[...] = jnp.full_like(m_ref, NEG_INF)
        l_ref[...] = jnp.zeros_like(l_ref)
        acc_ref[...] = jnp.zeros_like(acc_ref)

    def update(g, s, v):
        m_prev = m_ref[g]
        m_new = jnp.maximum(m_prev, s.max(axis=1, keepdims=True))
        alpha = jnp.exp(m_prev - m_new)
        p = jnp.exp(s - m_new)
        l_ref[g] = alpha * l_ref[g] + p.sum(axis=1, keepdims=True)
        acc_ref[g] = alpha * acc_ref[g] + jnp.dot(p.astype(jnp.bfloat16), v, preferred_element_type=jnp.float32)
        m_ref[g] = m_new

    def stack(mk):
        return jnp.concatenate([mk] * D_GROUP, axis=0)

    @pl.when(k < n_far)
    def _():
        col = k * DSA_TK + lax.broadcasted_iota(jnp.int32, (DSA_TQ, DSA_TK), 1)
        mk = jnp.concatenate([mf_ref[t] for t in range(DSA_TK // LANES)], axis=1).astype(jnp.float32)
        mk = stack(jnp.where(col < q0 - NEAR, mk, NEG_INF))

        def group(g, carry):
            s = jnp.dot(q_ref[g], kf_ref[g], preferred_element_type=jnp.float32) + mk
            update(g, s, vf_ref[g])
            return carry

        lax.fori_loop(0, D_KV_HEADS, group, 0, unroll=2)

    @pl.when(k == n_far_max)
    def _():
        mk0 = stack(m0_ref[...].astype(jnp.float32) + jnp.where(i > 0, 0.0, NEG_INF))
        mk1 = stack(m1_ref[...].astype(jnp.float32))

        def group(g, carry):
            q = q_ref[g]
            b = bias_ref[g]
            s0 = jnp.dot(q, k0_ref[g], preferred_element_type=jnp.float32) + (mk0 + b[:, :NEAR])
            update(g, s0, v0_ref[g])
            s1 = jnp.dot(q, k1_ref[g], preferred_element_type=jnp.float32) + (mk1 + b[:, NEAR:])
            update(g, s1, v1_ref[g])
            o_ref[g] = (acc_ref[g] / l_ref[g]).astype(o_ref.dtype)
            return carry

        lax.fori_loop(0, D_KV_HEADS, group, 0, unroll=2)


def dsa_prompt(q, kT, v, mask, near_bias):
    B, _, nq, _, _ = q.shape
    S = nq * DSA_TQ
    n_far_max = S // DSA_TK

    def far_blk(i, k):
        return jnp.minimum(k, jnp.maximum((i + 2) // 4 - 1, 0))

    def prev_blk(i):
        return jnp.maximum(i - 1, 0)

    return pl.pallas_call(
        functools.partial(_dsa_kernel, n_far_max=n_far_max), grid=(B, nq, n_far_max + 1),
        in_specs=[pl.BlockSpec((None, D_KV_HEADS, None, GROUP_ROWS, D_HEAD_DIM), lambda b, i, k: (b, 0, i, 0, 0)),
                  pl.BlockSpec((None, D_KV_HEADS, D_HEAD_DIM, DSA_TK), lambda b, i, k: (b, 0, 0, far_blk(i, k))),
                  pl.BlockSpec((None, D_KV_HEADS, DSA_TK, D_HEAD_DIM), lambda b, i, k: (b, 0, far_blk(i, k), 0)),
                  pl.BlockSpec((None, DSA_TK // LANES, DSA_TQ, LANES), lambda b, i, k: (b, far_blk(i, k), i, 0)),
                  pl.BlockSpec((None, D_KV_HEADS, D_HEAD_DIM, NEAR), lambda b, i, k: (b, 0, 0, prev_blk(i))),
                  pl.BlockSpec((None, D_KV_HEADS, NEAR, D_HEAD_DIM), lambda b, i, k: (b, 0, prev_blk(i), 0)),
                  pl.BlockSpec((None, None, DSA_TQ, LANES), lambda b, i, k: (b, prev_blk(i), i, 0)),
                  pl.BlockSpec((None, D_KV_HEADS, D_HEAD_DIM, NEAR), lambda b, i, k: (b, 0, 0, i)),
                  pl.BlockSpec((None, D_KV_HEADS, NEAR, D_HEAD_DIM), lambda b, i, k: (b, 0, i, 0)),
                  pl.BlockSpec((None, None, DSA_TQ, LANES), lambda b, i, k: (b, i, i, 0)),
                  pl.BlockSpec((D_KV_HEADS, GROUP_ROWS, 2 * NEAR), lambda b, i, k: (0, 0, 0))],
        out_specs=pl.BlockSpec((None, D_KV_HEADS, None, GROUP_ROWS, D_HEAD_DIM), lambda b, i, k: (b, 0, i, 0, 0)),
        out_shape=jax.ShapeDtypeStruct((B, D_KV_HEADS, nq, GROUP_ROWS, D_HEAD_DIM), jnp.bfloat16),
        scratch_shapes=[pltpu.VMEM((D_KV_HEADS, GROUP_ROWS, 1), jnp.float32),
                        pltpu.VMEM((D_KV_HEADS, GROUP_ROWS, 1), jnp.float32),
                        pltpu.VMEM((D_KV_HEADS, GROUP_ROWS, D_HEAD_DIM), jnp.float32)],
        compiler_params=_params("parallel", "parallel", "arbitrary"), name="dsa_attention",
    )(q, kT, v, mask, kT, v, mask, kT, v, mask, near_bias)


def dsa_prompt_mix(dq, dk, dv, iq, ik, iw, rel_bias, n_sel):
    B, S, _ = dq.shape
    bf = jnp.bfloat16
    ikT = jnp.swapaxes(ik, 1, 2).astype(bf)
    z = jnp.zeros_like(ikT)
    ke = jnp.concatenate([ikT, z], axis=1).reshape(B, LANES, S // IDX_TK, IDX_TK).swapaxes(1, 2)
    ko = jnp.concatenate([z, ikT], axis=1).reshape(B, LANES, S // IDX_TK, IDX_TK).swapaxes(1, 2)
    w = iw * (IDX_HEADS ** -0.5) * (IDX_DIM ** -0.5)
    mask = indexer_mask(iq.astype(bf), w, ke, ko, n_sel)
    nq = S // DSA_TQ
    q = (dq * D_HEAD_DIM ** -0.5).astype(bf).reshape(B, nq, DSA_TQ, D_KV_HEADS, D_GROUP, D_HEAD_DIM)
    q = q.transpose(0, 3, 1, 4, 2, 5).reshape(B, D_KV_HEADS, nq, GROUP_ROWS, D_HEAD_DIM)
    kT = dk.astype(bf).reshape(B, S, D_KV_HEADS, D_HEAD_DIM).transpose(0, 2, 3, 1)
    v = dv.astype(bf).reshape(B, S, D_KV_HEADS, D_HEAD_DIM).transpose(0, 2, 1, 3)
    rel = (jnp.arange(2 * NEAR)[None, :] - NEAR) - jnp.arange(DSA_TQ)[:, None]
    far_bucket = NUM_BUCKETS // 2 - 1
    near_bias = (rel_bias[t5_bucket(rel)] - rel_bias[far_bucket]).transpose(2, 0, 1)
    o = dsa_prompt(q, kT, v, mask, near_bias.reshape(D_KV_HEADS, GROUP_ROWS, 2 * NEAR))
    o = o.reshape(B, D_KV_HEADS, nq, D_GROUP, DSA_TQ, D_HEAD_DIM).transpose(0, 2, 4, 1, 3, 5)
    return o.reshape(B, S, D_HEADS * D_HEAD_DIM)


def dsa_attend(q, qi, wi, k, v, ki, q_pos, n_sel, rel_bias):
    B, Tq = q.shape[:2]
    L = k.shape[1]
    sc = jnp.einsum('bqhd,bsd->bqhs', qi, ki) * IDX_DIM ** -0.5
    score = jnp.einsum('bqhs,bqh->bqs', jax.nn.relu(sc), wi * IDX_HEADS ** -0.5)
    adm = (jnp.arange(L) // CHUNK)[None, :] <= (q_pos // CHUNK)[:, None]
    score = jnp.where(adm[None], score, -jnp.inf)
    _, idx = lax.top_k(score, n_sel)
    sel_ok = (idx // CHUNK) <= (q_pos // CHUNK)[None, :, None]
    gather = jax.vmap(lambda t, i: t[i])
    k_sel = gather(k, idx)
    v_sel = gather(v, idx)
    s = jnp.einsum('bqhgd,bqshd->bqhgs', q, k_sel) * D_HEAD_DIM ** -0.5
    bias = rel_bias[t5_bucket(idx - q_pos[None, :, None])]
    bias = bias.reshape(B, Tq, n_sel, D_KV_HEADS, D_GROUP).transpose(0, 1, 3, 4, 2)
    s = jnp.where(sel_ok[:, :, None, None, :], s + bias, NEG_INF)
    p = jax.nn.softmax(s, axis=-1)
    o = jnp.einsum('bqhgs,bqshd->bqhgd', p, v_sel)
    return o.reshape(B, Tq, D_HEADS * D_HEAD_DIM)


def odd_mix(z, pos, g_qa, g_kva, w_qb, w_kvb, rel_bias, cache):
    B, T, _ = z.shape
    q_lat, kv_lat, kpe_raw, dq, dk, dv, iq, ik, iw = odd_split(z)
    qn = rmsnorm(q_lat, g_qa).reshape(B * T, Q_LORA)
    q = matmul(qn, w_qb, tm=min(ROW_BLOCK, B * T)).reshape(B, T, C_HEADS, NOPE_DIM + ROPE_DIM)
    q_nope = q[..., :NOPE_DIM]
    q_pe = apply_rope(q[..., NOPE_DIM:], pos)
    lat = rmsnorm(kv_lat, g_kva)
    kpe = apply_rope(kpe_raw, pos)
    bf = jnp.bfloat16
    new = (lat, kpe, dk.reshape(B, T, D_KV_HEADS, D_HEAD_DIM), dv.reshape(B, T, D_KV_HEADS, D_HEAD_DIM), ik)
    if cache is None:
        kv = matmul(lat.reshape(B * T, KV_LORA), w_kvb).reshape(B, T, C_HEADS, NOPE_DIM + V_DIM)
        k_nope, v_c = kv[..., :NOPE_DIM], kv[..., NOPE_DIM:]
        pad = jnp.zeros((B, T, C_HEADS, MLA_QK - NOPE_DIM - ROPE_DIM), jnp.float32)
        qc = (jnp.concatenate([q_nope, q_pe, pad], axis=-1) * (NOPE_DIM + ROPE_DIM) ** -0.5).astype(bf)
        kc = jnp.concatenate([k_nope, jnp.broadcast_to(kpe[:, :, None, :], (B, T, C_HEADS, ROPE_DIM)), pad],
                             axis=-1).astype(bf)
        oc = mla_prompt(qc.reshape(B, T, C_HEADS * MLA_QK), kc.reshape(B, T, C_HEADS * MLA_QK),
                        v_c.astype(bf).reshape(B, T, C_HEADS * V_DIM))
        od = dsa_prompt_mix(dq, dk, dv, iq, ik, iw, rel_bias, min(TOPK_MAX, T // 4))
        return jnp.concatenate([oc, od], axis=-1), new
    c_lat, c_kpe, c_dk, c_dv, c_ik = cache
    lat_all = jnp.concatenate([c_lat, lat], axis=1)
    kpe_all = jnp.concatenate([c_kpe, kpe], axis=1)
    dk_all = jnp.concatenate([c_dk, new[2]], axis=1)
    dv_all = jnp.concatenate([c_dv, new[3]], axis=1)
    ik_all = jnp.concatenate([c_ik, ik], axis=1)
    L = lat_all.shape[1]
    kv = matmul(lat_all.reshape(B * L, KV_LORA), w_kvb).reshape(B, L, C_HEADS, NOPE_DIM + V_DIM)
    k_nope, v_c = kv[..., :NOPE_DIM], kv[..., NOPE_DIM:]
    oc = mla_attend(q_nope, q_pe, k_nope, kpe_all, v_c, pos)
    od = dsa_attend(dq.reshape(B, T, D_KV_HEADS, D_GROUP, D_HEAD_DIM), iq.reshape(B, T, IDX_HEADS, IDX_DIM), iw,
                    dk_all, dv_all, ik_all, pos, min(TOPK_MAX, L // 4), rel_bias)
    return jnp.concatenate([oc, od], axis=-1).astype(bf), new


def _final_norm_kernel(x_ref, g_ref, o_ref):
    x = x_ref[...]
    o_ref[...] = x * lax.rsqrt(jnp.mean(x * x, axis=-1, keepdims=True) + RMS_EPS) * g_ref[...]


def final_norm(x, g):
    row = pl.BlockSpec((ROW_BLOCK, D_MODEL), lambda i: (i, 0))
    return pl.pallas_call(
        _final_norm_kernel, grid=(N_ROW_BLOCKS,),
        in_specs=[row, pl.BlockSpec((1, D_MODEL), lambda i: (0, 0))], out_specs=row,
        out_shape=jax.ShapeDtypeStruct((N_TOK, D_MODEL), jnp.float32),
        compiler_params=_params("parallel"), name="final_norm")(x, g.reshape(1, D_MODEL))


def kernel(x_prompt, x_sample, c_prompt, c_sample, cache_a_k, cache_a_v, state_b_pool, cache_c_latent, cache_c_kpe, cache_d_k, cache_d_v, cache_d_idx, rel_bias, ada_mix_w, ada_mix_b, ada_ffn_w, ada_ffn_b, norm_mix, norm_ffn, norm_final, even_w_in, even_w_out, a_sinks, pool_w, pool_scale, odd_w_in, odd_w_out, c_q_norm, c_kv_norm, c_w_qb, c_w_kvb, moe_router_group, moe_router_expert, moe_w_gate, moe_w_up, moe_w_down):
    bf = jnp.bfloat16
    x = jnp.concatenate([x_prompt.reshape(N_PROMPT, D_MODEL), x_sample.reshape(N_SAMPLE, D_MODEL)], axis=0)
    c = jnp.concatenate([c_prompt, c_sample], axis=0)
    mod_mix = modulation_all(c, ada_mix_w, ada_mix_b)
    mod_ffn = modulation_all(c, ada_ffn_w, ada_ffn_b)
    pos_p = jnp.arange(SEQ)
    pos_s = PAST_LEN + jnp.arange(DEC_SEQ)
    w_router = jnp.concatenate(
        [moe_router_group, moe_router_expert,
         jnp.zeros((DEPTH, D_MODEL, ROUTER_PAD - N_GROUPS - N_EXPERTS), jnp.float32)], axis=-1)
    new_p = [[] for _ in range(8)]
    new_s = [[] for _ in range(8)]
    for l in range(DEPTH):
        j = l // 2
        shift, scale, gate = jnp.split(mod_mix[l], 3, axis=-1)
        h = norm_mod(x, norm_mix[l], scale, shift)
        if l % 2 == 0:
            z = matmul(h, even_w_in[j].astype(bf))
            zp = z[:N_PROMPT].reshape(BATCH, SEQ, -1)
            zs = z[N_PROMPT:].reshape(DEC_BATCH, DEC_SEQ, -1)
            mp, st_p = even_mix(zp, pos_p, a_sinks[j], pool_w[j], pool_scale[j], rel_bias, None)
            ms, st_s = even_mix(zs, pos_s, a_sinks[j], pool_w[j], pool_scale[j], rel_bias,
                                (cache_a_k[j], cache_a_v[j], state_b_pool[j]))
            off = 0
            w_out = even_w_out[j]
        else:
            z = matmul(h, odd_in_weight(odd_w_in[j]), tn=ODD_IN_PAD // 9)
            zp = z[:N_PROMPT].reshape(BATCH, SEQ, -1)
            zs = z[N_PROMPT:].reshape(DEC_BATCH, DEC_SEQ, -1)
            wq, wkv = c_w_qb[j].astype(bf), c_w_kvb[j].astype(bf)
            mp, st_p = odd_mix(zp, pos_p, c_q_norm[j], c_kv_norm[j], wq, wkv, rel_bias, None)
            ms, st_s = odd_mix(zs, pos_s, c_q_norm[j], c_kv_norm[j], wq, wkv, rel_bias,
                               (cache_c_latent[j], cache_c_kpe[j], cache_d_k[j], cache_d_v[j], cache_d_idx[j]))
            off = 3
            w_out = odd_w_out[j]
        for i, s in enumerate(st_p):
            new_p[off + i].append(s)
        for i, s in enumerate(st_s):
            new_s[off + i].append(s)
        mix = jnp.concatenate([mp.reshape(N_PROMPT, -1), ms.reshape(N_SAMPLE, -1)], axis=0).astype(bf)
        x = matmul_resid(mix, w_out.astype(bf), x, gate)
        shift, scale, gate = jnp.split(mod_ffn[l], 3, axis=-1)
        h, logits = norm_mod(x, norm_ffn[l], scale, shift, w_router[l])
        x = moe_ffn(x, h, logits, gate, moe_w_gate[l].astype(bf), moe_w_up[l].astype(bf), moe_w_down[l].astype(bf))
    y = final_norm(x, norm_final)
    st_p = [jnp.stack(s) for s in new_p]
    st_s = [jnp.stack(s) for s in new_s]
    a_k_p, a_v_p, pool_p, lat_p, kpe_p, dk_p, dv_p, didx_p = st_p
    a_k_s, a_v_s, pool_s, lat_s, kpe_s, dk_s, dv_s, didx_s = st_s
    return (y[:N_PROMPT].reshape(BATCH, SEQ, D_MODEL), y[N_PROMPT:].reshape(DEC_BATCH, DEC_SEQ, D_MODEL),
            a_k_p, a_k_s, a_v_p, a_v_s, pool_p, pool_s, lat_p, lat_s,
            kpe_p, kpe_s, dk_p, dk_s, dv_p, dv_s, didx_p, didx_s)
```

```python
import functools
import math

import jax
import jax.numpy as jnp
from jax import lax
from jax.experimental import pallas as pl
from jax.experimental.pallas import tpu as pltpu

D_MODEL = 4096
BATCH = 2
SEQ = 8192
DEPTH = 4
DEC_BATCH = 16
DEC_SEQ = 32
PAST_LEN = 1024
CHUNK = 64
QBLOCK = 128
N_PAIR = DEPTH // 2
A_HEADS = 32
A_KV_HEADS = 8
A_HEAD_DIM = 64
A_GROUP = A_HEADS // A_KV_HEADS
WINDOW = 128
WIN_CHUNKS = WINDOW // CHUNK
POOL_WIDTH = 2048
POOL_GROUPS = 4
POOL_GROUP_WIDTH = POOL_WIDTH // POOL_GROUPS
POOL_WINDOWS = (2, 4, 8, 16)
POOL_HIST = max(POOL_WINDOWS) - 1
C_HEADS = 16
Q_LORA = 1024
KV_LORA = 512
NOPE_DIM = 128
ROPE_DIM = 64
V_DIM = 128
ROPE_THETA = 10000.0
D_HEADS = 32
D_KV_HEADS = 8
D_HEAD_DIM = 64
D_GROUP = D_HEADS // D_KV_HEADS
IDX_HEADS = 32
IDX_DIM = 64
TOPK_MAX = 256
NUM_BUCKETS = 32
MAX_DISTANCE = 128
N_GROUPS = 8
EXPERTS_PER_GROUP = 8
N_EXPERTS = N_GROUPS * EXPERTS_PER_GROUP
TOP_K = 2
D_EXPERT = 512
RMS_EPS = 1e-6
NEG_INF = -1e30

EVEN_SPLITS = (A_HEADS * A_HEAD_DIM, A_KV_HEADS * A_HEAD_DIM, A_KV_HEADS * A_HEAD_DIM, POOL_WIDTH)
ODD_SPLITS = (Q_LORA, KV_LORA, ROPE_DIM, D_HEADS * D_HEAD_DIM, D_KV_HEADS * D_HEAD_DIM,
              D_KV_HEADS * D_HEAD_DIM, IDX_HEADS * IDX_DIM, IDX_DIM, IDX_HEADS)
ODD_IN = sum(ODD_SPLITS)
ODD_IN_PAD = -(-ODD_IN // 128) * 128
ODD_ORDER = (3, 6, 0, 1, 4, 5, 2, 7, 8)


def odd_in_weight(w):
    segs = split_cols(w, ODD_SPLITS)
    out = jnp.concatenate([segs[s] for s in ODD_ORDER], axis=-1).astype(jnp.bfloat16)
    return jnp.pad(out, ((0, 0), (0, ODD_IN_PAD - ODD_IN)))


def odd_split(z):
    parts = split_cols(z, [ODD_SPLITS[s] for s in ODD_ORDER])
    out = [None] * len(ODD_ORDER)
    for pos, s in enumerate(ODD_ORDER):
        out[s] = parts[pos]
    return out

N_PROMPT = BATCH * SEQ
N_SAMPLE = DEC_BATCH * DEC_SEQ
N_TOK = N_PROMPT + N_SAMPLE
N_COND = BATCH + DEC_BATCH

ROW_BLOCK = 512
ROW_GROUPS = ROW_BLOCK // DEC_SEQ
N_ROW_BLOCKS = N_TOK // ROW_BLOCK
MOE_ROWS = 256
ROUTER_PAD = 128
VMEM_LIMIT = 56 * 1024 * 1024

assert N_SAMPLE == ROW_BLOCK and SEQ % ROW_BLOCK == 0 and N_TOK % ROW_BLOCK == 0


def _params(*sem):
    return pltpu.CompilerParams(dimension_semantics=sem, vmem_limit_bytes=VMEM_LIMIT)


def split_cols(z, sizes):
    out, o = [], 0
    for s in sizes:
        out.append(z[..., o:o + s])
        o += s
    return out


def _modulation_kernel(c_ref, w_ref, b_ref, o_ref):
    c = c_ref[...]
    a = (c * jax.nn.sigmoid(c)).astype(jnp.bfloat16)
    o_ref[0] = jnp.dot(a, w_ref[0].astype(jnp.bfloat16), preferred_element_type=jnp.float32) + b_ref[0]


def modulation_all(c, w, b):
    tn = 1024
    depth, d, n = w.shape
    return pl.pallas_call(
        _modulation_kernel,
        grid=(depth, n // tn),
        in_specs=[pl.BlockSpec((N_COND, d), lambda l, j: (0, 0)),
                  pl.BlockSpec((1, d, tn), lambda l, j: (l, 0, j)),
                  pl.BlockSpec((1, 1, tn), lambda l, j: (l, 0, j))],
        out_specs=pl.BlockSpec((1, N_COND, tn), lambda l, j: (l, 0, j)),
        out_shape=jax.ShapeDtypeStruct((depth, N_COND, n), jnp.float32),
        compiler_params=_params("parallel", "parallel"),
        name="modulation",
    )(c, w, b.reshape(depth, 1, n))


def expand_rows(m, block_rows=ROW_BLOCK):
    d = m.shape[-1]
    groups = block_rows // DEC_SEQ
    p = jnp.broadcast_to(m[:BATCH, None, None, :], (BATCH, SEQ // block_rows, groups, d))
    return jnp.concatenate([p.reshape(N_PROMPT // block_rows, groups, d),
                            m[BATCH:].reshape(N_SAMPLE // block_rows, groups, d)], axis=0)


def _norm_mod(x_ref, g_ref, scale_ref, shift_ref):
    x = x_ref[...]
    y = x * lax.rsqrt(jnp.mean(x * x, axis=-1, keepdims=True) + RMS_EPS) * g_ref[...]
    y = y.reshape(ROW_GROUPS, DEC_SEQ, D_MODEL)
    y = y * (1.0 + scale_ref[0][:, None, :]) + shift_ref[0][:, None, :]
    return y.reshape(ROW_BLOCK, D_MODEL)


def _norm_mod_kernel(x_ref, g_ref, scale_ref, shift_ref, h_ref):
    h_ref[...] = _norm_mod(x_ref, g_ref, scale_ref, shift_ref).astype(jnp.bfloat16)


def _norm_mod_router_kernel(x_ref, g_ref, scale_ref, shift_ref, whi_ref, wlo_ref, h_ref, lg_ref):
    h = _norm_mod(x_ref, g_ref, scale_ref, shift_ref)
    hi = h.astype(jnp.bfloat16)
    lo = (h - hi.astype(jnp.float32)).astype(jnp.bfloat16)
    h_ref[...] = h
    lg_ref[...] = (jnp.dot(hi, whi_ref[...], preferred_element_type=jnp.float32)
                   + jnp.dot(lo, whi_ref[...], preferred_element_type=jnp.float32)
                   + jnp.dot(hi, wlo_ref[...], preferred_element_type=jnp.float32))


def norm_mod(x, g, scale, shift, w_router=None):
    row = pl.BlockSpec((ROW_BLOCK, D_MODEL), lambda i: (i, 0))
    grp = pl.BlockSpec((1, ROW_GROUPS, D_MODEL), lambda i: (i, 0, 0))
    in_specs = [row, pl.BlockSpec((1, D_MODEL), lambda i: (0, 0)), grp, grp]
    h_shape = jax.ShapeDtypeStruct((N_TOK, D_MODEL), jnp.bfloat16)
    args = (x, g.reshape(1, D_MODEL), expand_rows(scale), expand_rows(shift))
    if w_router is None:
        return pl.pallas_call(
            _norm_mod_kernel, grid=(N_ROW_BLOCKS,), in_specs=in_specs, out_specs=row, out_shape=h_shape,
            compiler_params=_params("parallel"), name="norm_mod")(*args)
    w_hi = w_router.astype(jnp.bfloat16)
    w_lo = (w_router - w_hi.astype(jnp.float32)).astype(jnp.bfloat16)
    wspec = pl.BlockSpec((D_MODEL, ROUTER_PAD), lambda i: (0, 0))
    return pl.pallas_call(
        _norm_mod_router_kernel, grid=(N_ROW_BLOCKS,), in_specs=in_specs + [wspec, wspec],
        out_specs=(row, pl.BlockSpec((ROW_BLOCK, ROUTER_PAD), lambda i: (i, 0))),
        out_shape=(jax.ShapeDtypeStruct((N_TOK, D_MODEL), jnp.float32),
                   jax.ShapeDtypeStruct((N_TOK, ROUTER_PAD), jnp.float32)),
        compiler_params=_params("parallel"), name="norm_mod_router")(*args, w_hi, w_lo)


def _matmul_kernel(a_ref, b_ref, o_ref):
    o_ref[...] = jnp.dot(a_ref[...].astype(jnp.bfloat16), b_ref[...],
                         preferred_element_type=jnp.float32).astype(o_ref.dtype)


def matmul(a, b, *, tm=ROW_BLOCK, tn=512, out_dtype=jnp.float32):
    m, k = a.shape
    n = b.shape[1]
    tm, tn = min(tm, m), min(tn, n)
    assert m % tm == 0 and n % tn == 0, (a.shape, b.shape, tm, tn)
    return pl.pallas_call(
        _matmul_kernel, grid=(m // tm, n // tn),
        in_specs=[pl.BlockSpec((tm, k), lambda i, j: (i, 0)), pl.BlockSpec((k, tn), lambda i, j: (0, j))],
        out_specs=pl.BlockSpec((tm, tn), lambda i, j: (i, j)),
        out_shape=jax.ShapeDtypeStruct((m, n), out_dtype),
        compiler_params=_params("parallel", "parallel"), name="matmul",
    )(a, b)


def _matmul_resid_kernel(a_ref, b_ref, x_ref, gate_ref, o_ref):
    y = jnp.dot(a_ref[...], b_ref[...], preferred_element_type=jnp.float32)
    tn = y.shape[-1]
    y = y.reshape(ROW_GROUPS, DEC_SEQ, tn) * gate_ref[0][:, None, :]
    o_ref[...] = x_ref[...] + y.reshape(ROW_BLOCK, tn)


def matmul_resid(a, b, x, gate, *, tn=512):
    k = a.shape[1]
    return pl.pallas_call(
        _matmul_resid_kernel, grid=(N_ROW_BLOCKS, D_MODEL // tn),
        in_specs=[pl.BlockSpec((ROW_BLOCK, k), lambda i, j: (i, 0)),
                  pl.BlockSpec((k, tn), lambda i, j: (0, j)),
                  pl.BlockSpec((ROW_BLOCK, tn), lambda i, j: (i, j)),
                  pl.BlockSpec((1, ROW_GROUPS, tn), lambda i, j: (i, 0, j))],
        out_specs=pl.BlockSpec((ROW_BLOCK, tn), lambda i, j: (i, j)),
        out_shape=jax.ShapeDtypeStruct((N_TOK, D_MODEL), jnp.float32),
        compiler_params=_params("parallel", "parallel"), name="matmul_resid",
    )(a, b, x, expand_rows(gate))


GATHER_UNROLL = 8


def _start_row_gather(idx_ref, src_hbm, dst_buf, sem, n_rows):
    def body(r, carry):
        pltpu.make_async_copy(src_hbm.at[pl.ds(idx_ref[0, r], 1)], dst_buf.at[pl.ds(r, 1)], sem).start()
        return carry

    lax.fori_loop(0, n_rows, body, 0, unroll=GATHER_UNROLL)


def _wait_row_gather(src_hbm, dst_buf, sem, n_rows):
    pltpu.make_async_copy(src_hbm.at[pl.ds(0, n_rows)], dst_buf, sem).wait()


def _expert_kernel(block_e_ref, n_active_ref, tok_ref, tok_next_ref, h_hbm, wg_ref, wu_ref, wd_ref, gate_ref,
                   o_ref, xbuf, sem):
    i = pl.program_id(0)
    n_act = n_active_ref[0]
    slot = i % 2

    @pl.when(i == 0)
    def _():
        _start_row_gather(tok_ref, h_hbm, xbuf.at[0], sem.at[0], MOE_ROWS)

    @pl.when(i + 1 < n_act)
    def _():
        _start_row_gather(tok_next_ref, h_hbm, xbuf.at[1 - slot], sem.at[1 - slot], MOE_ROWS)

    @pl.when(i < n_act)
    def _():
        _wait_row_gather(h_hbm, xbuf.at[slot], sem.at[slot], MOE_ROWS)
        x = xbuf[slot].astype(jnp.bfloat16)
        g = jnp.dot(x, wg_ref[0], preferred_element_type=jnp.float32)
        u = jnp.dot(x, wu_ref[0], preferred_element_type=jnp.float32)
        a = (g * jax.nn.sigmoid(g) * u).astype(jnp.bfloat16)
        o_ref[...] = jnp.dot(a, wd_ref[0], preferred_element_type=jnp.float32) * gate_ref[...]

    @pl.when(i >= n_act)
    def _():
        o_ref[...] = jnp.zeros_like(o_ref)


def expert_blocks(h, tok_of_row, gate_of_row, block_e, n_active, w_gate, w_up, w_down):
    n_blocks = tok_of_row.shape[0]

    def smem_blk(index_map):
        return pl.BlockSpec((None, 1, MOE_ROWS), index_map, memory_space=pltpu.SMEM)

    return pl.pallas_call(
        _expert_kernel,
        grid_spec=pltpu.PrefetchScalarGridSpec(
            num_scalar_prefetch=2, grid=(n_blocks,),
            in_specs=[smem_blk(lambda i, be, na: (i, 0, 0)),
                      smem_blk(lambda i, be, na: (jnp.minimum(i + 1, n_blocks - 1), 0, 0)),
                      pl.BlockSpec(memory_space=pl.ANY),
                      pl.BlockSpec((1, D_MODEL, D_EXPERT), lambda i, be, na: (be[i], 0, 0)),
                      pl.BlockSpec((1, D_MODEL, D_EXPERT), lambda i, be, na: (be[i], 0, 0)),
                      pl.BlockSpec((1, D_EXPERT, D_MODEL), lambda i, be, na: (be[i], 0, 0)),
                      pl.BlockSpec((MOE_ROWS, 1), lambda i, be, na: (i, 0))],
            out_specs=pl.BlockSpec((MOE_ROWS, D_MODEL), lambda i, be, na: (i, 0)),
            scratch_shapes=[pltpu.VMEM((2, MOE_ROWS, D_MODEL), jnp.float32), pltpu.SemaphoreType.DMA((2,))]),
        out_shape=jax.ShapeDtypeStruct((n_blocks * MOE_ROWS, D_MODEL), jnp.float32),
        compiler_params=_params("arbitrary"), name="moe_experts",
    )(block_e, n_active, tok_of_row, tok_of_row, h, w_gate, w_up, w_down, gate_of_row)


COMBINE_ROWS = 256
COMBINE_GROUPS = COMBINE_ROWS // DEC_SEQ


def _combine_kernel(pos_ref, pos_next_ref, y_hbm, x_ref, gate_ref, o_ref, ybuf, sem):
    i = pl.program_id(0)
    n = pl.num_programs(0)
    slot = i % 2
    n_rows = TOP_K * COMBINE_ROWS

    @pl.when(i == 0)
    def _():
        _start_row_gather(pos_ref, y_hbm, ybuf.at[0], sem.at[0], n_rows)

    @pl.when(i + 1 < n)
    def _():
        _start_row_gather(pos_next_ref, y_hbm, ybuf.at[1 - slot], sem.at[1 - slot], n_rows)

    _wait_row_gather(y_hbm, ybuf.at[slot], sem.at[slot], n_rows)
    f = ybuf[slot, :COMBINE_ROWS] + ybuf[slot, COMBINE_ROWS:]
    f = f.reshape(COMBINE_GROUPS, DEC_SEQ, D_MODEL) * gate_ref[0][:, None, :]
    o_ref[...] = x_ref[...] + f.reshape(COMBINE_ROWS, D_MODEL)


def moe_combine(y, pos, x, gate):
    nb = N_TOK // COMBINE_ROWS

    def smem_blk(index_map):
        return pl.BlockSpec((None, 1, TOP_K * COMBINE_ROWS), index_map, memory_space=pltpu.SMEM)

    return pl.pallas_call(
        _combine_kernel, grid=(nb,),
        in_specs=[smem_blk(lambda i: (i, 0, 0)), smem_blk(lambda i: (jnp.minimum(i + 1, nb - 1), 0, 0)),
                  pl.BlockSpec(memory_space=pl.ANY),
                  pl.BlockSpec((COMBINE_ROWS, D_MODEL), lambda i: (i, 0)),
                  pl.BlockSpec((1, COMBINE_GROUPS, D_MODEL), lambda i: (i, 0, 0))],
        out_specs=pl.BlockSpec((COMBINE_ROWS, D_MODEL), lambda i: (i, 0)),
        out_shape=jax.ShapeDtypeStruct((N_TOK, D_MODEL), jnp.float32),
        scratch_shapes=[pltpu.VMEM((2, TOP_K * COMBINE_ROWS, D_MODEL), jnp.float32),
                        pltpu.SemaphoreType.DMA((2,))],
        compiler_params=_params("arbitrary"), name="moe_combine",
    )(pos, pos, y, x, expand_rows(gate, COMBINE_ROWS))


def moe_ffn(x, h, logits, gate, w_gate, w_up, w_down):
    n = h.shape[0]
    lg = logits[:, :N_GROUPS]
    grp = jnp.argmax(lg, axis=-1)
    p_grp = jnp.take_along_axis(jax.nn.softmax(lg, axis=-1), grp[:, None], axis=-1)
    le = logits[:, N_GROUPS:N_GROUPS + N_EXPERTS].reshape(n, N_GROUPS, EXPERTS_PER_GROUP)
    le = jnp.take_along_axis(le, grp[:, None, None], axis=1)[:, 0]
    pe, ie = lax.top_k(jax.nn.softmax(le, axis=-1), TOP_K)
    gates = p_grp * pe / pe.sum(-1, keepdims=True)
    expert = (grp[:, None] * EXPERTS_PER_GROUP + ie).astype(jnp.int32)
    m = n * TOP_K
    flat_e = expert.reshape(m)
    onehot = (flat_e[:, None] == jnp.arange(N_EXPERTS, dtype=jnp.int32)[None, :]).astype(jnp.int32)
    csum = jnp.cumsum(onehot, axis=0)
    rank = jnp.take_along_axis(csum, flat_e[:, None], axis=1)[:, 0] - 1
    sizes = csum[-1]
    padded = (sizes + MOE_ROWS - 1) // MOE_ROWS * MOE_ROWS
    pend = jnp.cumsum(padded)
    pstart = pend - padded
    dest = (pstart[flat_e] + rank).astype(jnp.int32)
    n_blocks = -(-m // MOE_ROWS) + N_EXPERTS
    total = n_blocks * MOE_ROWS
    tok_of_row = jnp.zeros((total,), jnp.int32).at[dest].set(jnp.arange(m, dtype=jnp.int32) // TOP_K)
    gate_of_row = jnp.zeros((total,), jnp.float32).at[dest].set(gates.reshape(m))
    block_start = jnp.arange(n_blocks, dtype=jnp.int32) * MOE_ROWS
    block_e = jnp.minimum(jnp.searchsorted(pend, block_start, side='right'), N_EXPERTS - 1).astype(jnp.int32)
    n_active = (pend[-1] // MOE_ROWS).astype(jnp.int32).reshape(1)
    y = expert_blocks(h, tok_of_row.reshape(n_blocks, 1, MOE_ROWS), gate_of_row.reshape(total, 1), block_e, n_active,
                      w_gate, w_up, w_down)
    nb = n // COMBINE_ROWS
    pos = dest.reshape(nb, COMBINE_ROWS, TOP_K).transpose(0, 2, 1).reshape(nb, 1, TOP_K * COMBINE_ROWS)
    return moe_combine(y, pos, x, gate)


def t5_bucket(rel):
    half = NUM_BUCKETS // 2
    max_exact = half // 2
    ret = jnp.where(rel > 0, half, 0)
    n = jnp.abs(rel)
    nf = jnp.maximum(n, 1).astype(jnp.float32)
    large = max_exact + (jnp.log(nf / max_exact) / math.log(MAX_DISTANCE / max_exact)
                         * (half - max_exact)).astype(jnp.int32)
    large = jnp.minimum(large, half - 1)
    return ret + jnp.where(n < max_exact, n, large)


def head_bias(rel, table, n_kv, group):
    b = table[t5_bucket(rel)]
    return jnp.moveaxis(b, -1, 0).reshape((n_kv, group) + rel.shape)


def apply_rope(x, pos):
    inv = jnp.exp(-math.log(ROPE_THETA) * jnp.arange(0, ROPE_DIM, 2, dtype=jnp.float32) / ROPE_DIM)
    ang = pos.astype(jnp.float32)[:, None] * inv[None, :]
    ang = ang.reshape(ang.shape[:1] + (1,) * (x.ndim - 3) + ang.shape[1:])
    cos, sin = jnp.cos(ang), jnp.sin(ang)
    x1, x2 = jnp.split(x.astype(jnp.float32), 2, axis=-1)
    return jnp.concatenate([x1 * cos - x2 * sin, x2 * cos + x1 * sin], axis=-1)


def rmsnorm(x, g):
    return x * lax.rsqrt(jnp.mean(x * x, axis=-1, keepdims=True) + RMS_EPS) * g


def sink_attention(q, k, v, bias, valid, sinks):
    n, tq = q.shape[:2]
    s = jnp.einsum('nqkgd,nskd->nkgqs', q, k) * A_HEAD_DIM ** -0.5 + bias
    s = jnp.where(valid[:, None, None], s, NEG_INF)
    sink = sinks.reshape(1, A_KV_HEADS, A_GROUP, 1, 1)
    mx = jnp.maximum(s.max(-1, keepdims=True), sink)
    e = jnp.exp(s - mx)
    p = e / (e.sum(-1, keepdims=True) + jnp.exp(sink - mx))
    o = jnp.einsum('nkgqs,nskd->nqkgd', p, v)
    return o.reshape(n, tq, A_HEADS * A_HEAD_DIM)


def window_attention_prompt(q, k, v, sinks, rel_bias):
    B, S = q.shape[:2]
    nc = S // CHUNK
    band = (WIN_CHUNKS + 1) * CHUNK

    def banded(t):
        tc = t.reshape(B, nc, CHUNK, A_KV_HEADS, A_HEAD_DIM)
        tp = jnp.concatenate([jnp.zeros_like(tc[:, :WIN_CHUNKS]), tc], axis=1)
        tb = jnp.concatenate([tp[:, j:j + nc] for j in range(WIN_CHUNKS + 1)], axis=2)
        return tb.reshape(B * nc, band, A_KV_HEADS, A_HEAD_DIM)

    qi = jnp.arange(CHUNK)[:, None]
    kr = jnp.arange(band)[None, :]
    rel = kr - WIN_CHUNKS * CHUNK - qi
    key_chunk = jnp.arange(nc)[:, None] - WIN_CHUNKS + kr // CHUNK
    valid = jnp.broadcast_to((key_chunk >= 0)[None, :, None, :], (B, nc, CHUNK, band)).reshape(B * nc, CHUNK, band)
    qb = q.reshape(B * nc, CHUNK, A_KV_HEADS, A_GROUP, A_HEAD_DIM)
    o = sink_attention(qb, banded(k), banded(v), head_bias(rel, rel_bias, A_KV_HEADS, A_GROUP), valid, sinks)
    return o.reshape(B, S, A_HEADS * A_HEAD_DIM)


def window_attention_sample(q, k_all, v_all, pos, W, sinks, rel_bias):
    B, T = q.shape[:2]
    k_pos = jnp.concatenate([PAST_LEN - W + jnp.arange(W), pos])
    qc = (pos // CHUNK)[:, None]
    kc = (k_pos // CHUNK)[None, :]
    valid = (kc <= qc) & (kc >= qc - WIN_CHUNKS)
    valid = jnp.broadcast_to(valid[None], (B,) + valid.shape)
    bias = head_bias(k_pos[None, :] - pos[:, None], rel_bias, A_KV_HEADS, A_GROUP)
    return sink_attention(q, k_all, v_all, bias, valid, sinks)


def pool_mix(u_ext, pos, n_hist, pool_w, pool_scale):
    B, L, P = u_ext.shape
    T = L - n_hist
    cs = jnp.cumsum(u_ext, axis=1)
    cs = jnp.concatenate([jnp.zeros_like(cs[:, :1]), cs], axis=1)
    end = n_hist + jnp.arange(T) + 1
    outs = []
    for g, w in enumerate(POOL_WINDOWS):
        lo, hi = g * POOL_GROUP_WIDTH, (g + 1) * POOL_GROUP_WIDTH
        csg = cs[..., lo:hi]
        start = jnp.maximum(end - w, 0)
        cnt = jnp.minimum(w, pos + 1).astype(jnp.float32)[None, :, None]
        mean = (csg[:, end] - csg[:, start]) / cnt
        outs.append(mean - u_ext[:, n_hist:, lo:hi])
    d = jnp.stack(outs, axis=2)
    y = jnp.einsum('btgc,gcd->btgd', d, pool_w).reshape(B, T, P)
    return y * pool_scale


def even_mix(z, pos, sinks, pool_w, pool_scale, rel_bias, cache):
    B, T, _ = z.shape
    q, k, v, u = split_cols(z, EVEN_SPLITS)
    q = q.reshape(B, T, A_KV_HEADS, A_GROUP, A_HEAD_DIM)
    k = k.reshape(B, T, A_KV_HEADS, A_HEAD_DIM)
    v = v.reshape(B, T, A_KV_HEADS, A_HEAD_DIM)
    if cache is None:
        a = window_attention_prompt(q, k, v, sinks, rel_bias)
        p = pool_mix(u, pos, 0, pool_w, pool_scale)
        new = (k[:, -WINDOW:], v[:, -WINDOW:], u[:, -POOL_HIST:])
    else:
        k_cache, v_cache, u_cache = cache
        W = k_cache.shape[1]
        k_all = jnp.concatenate([k_cache, k], axis=1)
        v_all = jnp.concatenate([v_cache, v], axis=1)
        a = window_attention_sample(q, k_all, v_all, pos, W, sinks, rel_bias)
        u_all = jnp.concatenate([u_cache, u], axis=1)
        p = pool_mix(u_all, pos, POOL_HIST, pool_w, pool_scale)
        new = (k_all[:, -W:], v_all[:, -W:], u_all[:, -POOL_HIST:])
    return jnp.concatenate([a, p], axis=-1), new


MLA_T = 512
MLA_QK = 256
MLA_V = 128
MLA_HP = 2


def _mla_kernel(q_ref, k_ref, v_ref, o_ref, m_ref, l_ref, acc_ref):
    i = pl.program_id(2)
    k = pl.program_id(3)

    @pl.when(k == 0)
    def _():
        m_ref[...] = jnp.full_like(m_ref, NEG_INF)
        l_ref[...] = jnp.zeros_like(l_ref)
        acc_ref[...] = jnp.zeros_like(acc_ref)

    def step(diagonal):
        for hh in range(MLA_HP):
            qs = slice(hh * MLA_QK, (hh + 1) * MLA_QK)
            vs = slice(hh * MLA_V, (hh + 1) * MLA_V)
            s = lax.dot_general(q_ref[:, qs], k_ref[:, qs], (((1,), (1,)), ((), ())),
                                preferred_element_type=jnp.float32)
            if diagonal:
                row = lax.broadcasted_iota(jnp.int32, s.shape, 0) // CHUNK
                col = lax.broadcasted_iota(jnp.int32, s.shape, 1) // CHUNK
                s = jnp.where(col <= row, s, NEG_INF)
            m_prev = m_ref[hh]
            m_new = jnp.maximum(m_prev, s.max(axis=1, keepdims=True))
            alpha = jnp.exp(m_prev - m_new)
            p = jnp.exp(s - m_new)
            l_ref[hh] = alpha * l_ref[hh] + p.sum(axis=1, keepdims=True)
            acc_ref[:, vs] = alpha * acc_ref[:, vs] + jnp.dot(p.astype(jnp.bfloat16), v_ref[:, vs],
                                                              preferred_element_type=jnp.float32)
            m_ref[hh] = m_new

    @pl.when(k < i)
    def _():
        step(False)

    @pl.when(k == i)
    def _():
        step(True)
        for hh in range(MLA_HP):
            vs = slice(hh * MLA_V, (hh + 1) * MLA_V)
            o_ref[:, vs] = (acc_ref[:, vs] / l_ref[hh]).astype(o_ref.dtype)


def mla_prompt(qc, kc, v):
    B, S, _ = qc.shape
    n = S // MLA_T
    qk_w, v_w = MLA_HP * MLA_QK, MLA_HP * MLA_V
    return pl.pallas_call(
        _mla_kernel, grid=(B, C_HEADS // MLA_HP, n, n),
        in_specs=[pl.BlockSpec((None, MLA_T, qk_w), lambda b, h, i, k: (b, i, h)),
                  pl.BlockSpec((None, MLA_T, qk_w), lambda b, h, i, k: (b, jnp.minimum(k, i), h)),
                  pl.BlockSpec((None, MLA_T, v_w), lambda b, h, i, k: (b, jnp.minimum(k, i), h))],
        out_specs=pl.BlockSpec((None, MLA_T, v_w), lambda b, h, i, k: (b, i, h)),
        out_shape=jax.ShapeDtypeStruct((B, S, C_HEADS * MLA_V), jnp.bfloat16),
        scratch_shapes=[pltpu.VMEM((MLA_HP, MLA_T, 1), jnp.float32), pltpu.VMEM((MLA_HP, MLA_T, 1), jnp.float32),
                        pltpu.VMEM((MLA_T, v_w), jnp.float32)],
        compiler_params=_params("parallel", "parallel", "parallel", "arbitrary"), name="mla_prompt",
    )(qc, kc, v)


IDX_TQ = 128
IDX_TK = 256
LANES = 128
INT_MIN = -2 ** 31


def _indexer_kernel(q_ref, w_ref, ke_ref, ko_ref, mask_ref, key_ref, *, n_sel, tq, pos0, n_keys):
    i = pl.program_id(1)
    q0 = pos0 + i * tq
    n_lane_tiles = mask_ref.shape[0]
    sub = IDX_TK // LANES
    n_kb = jnp.minimum((q0 + tq + IDX_TK - 1) // IDX_TK, n_lane_tiles // sub)
    row_chunk = (q0 + lax.broadcasted_iota(jnp.int32, (tq, LANES), 0)) // CHUNK
    lane_iota = lax.broadcasted_iota(jnp.int32, (tq, LANES), 1)
    padded_keys = n_keys < n_lane_tiles * LANES

    def admissible(j):
        col = j * LANES + lane_iota
        adm = col // CHUNK <= row_chunk
        if padded_keys:
            adm = jnp.logical_and(adm, col < n_keys)
        return adm

    def score_tile(kb, carry):
        ke = ke_ref[kb]
        ko = ko_ref[kb]
        acc = jnp.zeros((tq, IDX_TK), jnp.float32)
        for p in range(IDX_HEADS // 2):
            qp = q_ref[:, p * LANES:(p + 1) * LANES]
            se = jnp.dot(qp, ke, preferred_element_type=jnp.float32)
            so = jnp.dot(qp, ko, preferred_element_type=jnp.float32)
            acc = acc + jnp.maximum(se, 0.0) * w_ref[:, 2 * p:2 * p + 1]
            acc = acc + jnp.maximum(so, 0.0) * w_ref[:, 2 * p + 1:2 * p + 2]
        for t in range(sub):
            j = kb * sub + t
            a = jnp.where(admissible(j), acc[:, t * LANES:(t + 1) * LANES], -jnp.inf)
            bits = pltpu.bitcast(a, jnp.int32)
            key_ref[j] = bits ^ ((bits >> 31) & jnp.int32(0x7FFFFFFF))
        return carry

    lax.fori_loop(0, n_kb, score_tile, 0)
    n_tiles = n_kb * sub

    def count_ge(cand):
        def body(j, cnt):
            return cnt + jnp.where(key_ref[j] >= cand, 1, 0)
        cnt = lax.fori_loop(0, n_tiles, body, jnp.zeros((tq, LANES), jnp.int32))
        return cnt.sum(axis=1, keepdims=True)

    thr = jnp.where(count_ge(jnp.zeros((tq, 1), jnp.int32)) >= n_sel,
                    jnp.zeros((tq, 1), jnp.int32), jnp.full((tq, 1), INT_MIN, jnp.int32))

    def bit_step(b, thr):
        cand = thr | (jnp.int32(1) << (30 - b))
        return jnp.where(count_ge(cand) >= n_sel, cand, thr)

    thr = lax.fori_loop(0, 31, bit_step, thr)

    def write_mask(j, carry):
        sel = jnp.logical_and(key_ref[j] >= thr, admissible(j))
        mask_ref[j] = jnp.where(sel, 0.0, NEG_INF).astype(mask_ref.dtype)
        return carry

    lax.fori_loop(0, n_tiles, write_mask, 0)

    def write_rest(j, carry):
        mask_ref[j] = jnp.full((tq, LANES), NEG_INF, mask_ref.dtype)
        return carry

    lax.fori_loop(n_tiles, n_lane_tiles, write_rest, 0)


def indexer_mask(qi, wi, ke, ko, n_sel, *, tq, pos0, n_keys):
    B, T, _ = qi.shape
    nkb = ke.shape[1]
    L = nkb * IDX_TK
    return pl.pallas_call(
        functools.partial(_indexer_kernel, n_sel=n_sel, tq=tq, pos0=pos0, n_keys=n_keys), grid=(B, T // tq),
        in_specs=[pl.BlockSpec((None, tq, IDX_HEADS * IDX_DIM), lambda b, i: (b, i, 0)),
                  pl.BlockSpec((None, tq, IDX_HEADS), lambda b, i: (b, i, 0)),
                  pl.BlockSpec((None, nkb, LANES, IDX_TK), lambda b, i: (b, 0, 0, 0)),
                  pl.BlockSpec((None, nkb, LANES, IDX_TK), lambda b, i: (b, 0, 0, 0))],
        out_specs=pl.BlockSpec((None, L // LANES, tq, LANES), lambda b, i: (b, 0, i, 0)),
        out_shape=jax.ShapeDtypeStruct((B, L // LANES, T, LANES), jnp.bfloat16),
        scratch_shapes=[pltpu.VMEM((L // LANES, tq, LANES), jnp.int32)],
        compiler_params=_params("parallel", "parallel"), name="dsa_indexer",
    )(qi, wi, ke, ko)


def indexer_keys(ik, l_pad):
    B, L, _ = ik.shape
    ikT = jnp.pad(jnp.swapaxes(ik, 1, 2).astype(jnp.bfloat16), ((0, 0), (0, 0), (0, l_pad - L)))
    z = jnp.zeros_like(ikT)
    ke = jnp.concatenate([ikT, z], axis=1).reshape(B, LANES, l_pad // IDX_TK, IDX_TK).swapaxes(1, 2)
    ko = jnp.concatenate([z, ikT], axis=1).reshape(B, LANES, l_pad // IDX_TK, IDX_TK).swapaxes(1, 2)
    return ke, ko


DSA_TQ = 128
DSA_TK = 512
NEAR = 128


GROUP_ROWS = D_GROUP * DSA_TQ


def _dsa_kernel(q_ref, kf_ref, vf_ref, mf_ref, k0_ref, v0_ref, m0_ref, k1_ref, v1_ref, m1_ref, bias_ref,
                o_ref, m_ref, l_ref, acc_ref, *, n_far_max):
    i = pl.program_id(1)
    k = pl.program_id(2)
    q0 = i * DSA_TQ
    n_far = (i + 2) // 4

    @pl.when(k == 0)
    def _():
        m_ref[...] = jnp.full_like(m_ref, NEG_INF)
        l_ref[...] = jnp.zeros_like(l_ref)
        acc_ref[...] = jnp.zeros_like(acc_ref)

    def update(g, s, v):
        m_prev = m_ref[g]
        m_new = jnp.maximum(m_prev, s.max(axis=1, keepdims=True))
        alpha = jnp.exp(m_prev - m_new)
        p = jnp.exp(s - m_new)
        l_ref[g] = alpha * l_ref[g] + p.sum(axis=1, keepdims=True)
        acc_ref[g] = alpha * acc_ref[g] + jnp.dot(p.astype(jnp.bfloat16), v, preferred_element_type=jnp.float32)
        m_ref[g] = m_new

    def stack(mk):
        return jnp.concatenate([mk] * D_GROUP, axis=0)

    @pl.when(k < n_far)
    def _():
        col = k * DSA_TK + lax.broadcasted_iota(jnp.int32, (DSA_TQ, DSA_TK), 1)
        mk = jnp.concatenate([mf_ref[t] for t in range(DSA_TK // LANES)], axis=1).astype(jnp.float32)
        mk = stack(jnp.where(col < q0 - NEAR, mk, NEG_INF))

        def group(g, carry):
            s = jnp.dot(q_ref[g], kf_ref[g], preferred_element_type=jnp.float32) + mk
            update(g, s, vf_ref[g])
            return carry

        lax.fori_loop(0, D_KV_HEADS, group, 0, unroll=2)

    @pl.when(k == n_far_max)
    def _():
        mk0 = stack(m0_ref[...].astype(jnp.float32) + jnp.where(i > 0, 0.0, NEG_INF))
        mk1 = stack(m1_ref[...].astype(jnp.float32))

        def group(g, carry):
            q = q_ref[g]
            b = bias_ref[g]
            s0 = jnp.dot(q, k0_ref[g], preferred_element_type=jnp.float32) + (mk0 + b[:, :NEAR])
            update(g, s0, v0_ref[g])
            s1 = jnp.dot(q, k1_ref[g], preferred_element_type=jnp.float32) + (mk1 + b[:, NEAR:])
            update(g, s1, v1_ref[g])
            o_ref[g] = (acc_ref[g] / l_ref[g]).astype(o_ref.dtype)
            return carry

        lax.fori_loop(0, D_KV_HEADS, group, 0, unroll=2)


def dsa_prompt(q, kT, v, mask, near_bias):
    B, _, nq, _, _ = q.shape
    S = nq * DSA_TQ
    n_far_max = S // DSA_TK

    def far_blk(i, k):
        return jnp.minimum(k, jnp.maximum((i + 2) // 4 - 1, 0))

    def prev_blk(i):
        return jnp.maximum(i - 1, 0)

    return pl.pallas_call(
        functools.partial(_dsa_kernel, n_far_max=n_far_max), grid=(B, nq, n_far_max + 1),
        in_specs=[pl.BlockSpec((None, D_KV_HEADS, None, GROUP_ROWS, D_HEAD_DIM), lambda b, i, k: (b, 0, i, 0, 0)),
                  pl.BlockSpec((None, D_KV_HEADS, D_HEAD_DIM, DSA_TK), lambda b, i, k: (b, 0, 0, far_blk(i, k))),
                  pl.BlockSpec((None, D_KV_HEADS, DSA_TK, D_HEAD_DIM), lambda b, i, k: (b, 0, far_blk(i, k), 0)),
                  pl.BlockSpec((None, DSA_TK // LANES, DSA_TQ, LANES), lambda b, i, k: (b, far_blk(i, k), i, 0)),
                  pl.BlockSpec((None, D_KV_HEADS, D_HEAD_DIM, NEAR), lambda b, i, k: (b, 0, 0, prev_blk(i))),
                  pl.BlockSpec((None, D_KV_HEADS, NEAR, D_HEAD_DIM), lambda b, i, k: (b, 0, prev_blk(i), 0)),
                  pl.BlockSpec((None, None, DSA_TQ, LANES), lambda b, i, k: (b, prev_blk(i), i, 0)),
                  pl.BlockSpec((None, D_KV_HEADS, D_HEAD_DIM, NEAR), lambda b, i, k: (b, 0, 0, i)),
                  pl.BlockSpec((None, D_KV_HEADS, NEAR, D_HEAD_DIM), lambda b, i, k: (b, 0, i, 0)),
                  pl.BlockSpec((None, None, DSA_TQ, LANES), lambda b, i, k: (b, i, i, 0)),
                  pl.BlockSpec((D_KV_HEADS, GROUP_ROWS, 2 * NEAR), lambda b, i, k: (0, 0, 0))],
        out_specs=pl.BlockSpec((None, D_KV_HEADS, None, GROUP_ROWS, D_HEAD_DIM), lambda b, i, k: (b, 0, i, 0, 0)),
        out_shape=jax.ShapeDtypeStruct((B, D_KV_HEADS, nq, GROUP_ROWS, D_HEAD_DIM), jnp.bfloat16),
        scratch_shapes=[pltpu.VMEM((D_KV_HEADS, GROUP_ROWS, 1), jnp.float32),
                        pltpu.VMEM((D_KV_HEADS, GROUP_ROWS, 1), jnp.float32),
                        pltpu.VMEM((D_KV_HEADS, GROUP_ROWS, D_HEAD_DIM), jnp.float32)],
        compiler_params=_params("parallel", "parallel", "arbitrary"), name="dsa_attention",
    )(q, kT, v, mask, kT, v, mask, kT, v, mask, near_bias)


def dsa_prompt_mix(dq, dk, dv, iq, ik, iw, rel_bias, n_sel):
    B, S, _ = dq.shape
    bf = jnp.bfloat16
    ke, ko = indexer_keys(ik, S)
    w = iw * (IDX_HEADS ** -0.5) * (IDX_DIM ** -0.5)
    mask = indexer_mask(iq.astype(bf), w, ke, ko, n_sel, tq=IDX_TQ, pos0=0, n_keys=S)
    nq = S // DSA_TQ
    q = (dq * D_HEAD_DIM ** -0.5).astype(bf).reshape(B, nq, DSA_TQ, D_KV_HEADS, D_GROUP, D_HEAD_DIM)
    q = q.transpose(0, 3, 1, 4, 2, 5).reshape(B, D_KV_HEADS, nq, GROUP_ROWS, D_HEAD_DIM)
    kT = dk.astype(bf).reshape(B, S, D_KV_HEADS, D_HEAD_DIM).transpose(0, 2, 3, 1)
    v = dv.astype(bf).reshape(B, S, D_KV_HEADS, D_HEAD_DIM).transpose(0, 2, 1, 3)
    rel = (jnp.arange(2 * NEAR)[None, :] - NEAR) - jnp.arange(DSA_TQ)[:, None]
    far_bucket = NUM_BUCKETS // 2 - 1
    near_bias = (rel_bias[t5_bucket(rel)] - rel_bias[far_bucket]).transpose(2, 0, 1)
    o = dsa_prompt(q, kT, v, mask, near_bias.reshape(D_KV_HEADS, GROUP_ROWS, 2 * NEAR))
    o = o.reshape(B, D_KV_HEADS, nq, D_GROUP, DSA_TQ, D_HEAD_DIM).transpose(0, 2, 4, 1, 3, 5)
    return o.reshape(B, S, D_HEADS * D_HEAD_DIM)


SAMPLE_KEYS = PAST_LEN + DEC_SEQ
SAMPLE_KEYS_PAD = -(-SAMPLE_KEYS // IDX_TK) * IDX_TK
assert (SAMPLE_KEYS - 1) // CHUNK == PAST_LEN // CHUNK


def _dsa_sample_kernel(q_ref, kT_ref, v_ref, mask_ref, bias_ref, o_ref):
    mk = jnp.concatenate([mask_ref[t] for t in range(mask_ref.shape[0])], axis=1).astype(jnp.float32)
    mk = jnp.concatenate([mk] * D_GROUP, axis=0)

    def group(g, carry):
        s = jnp.dot(q_ref[g], kT_ref[g], preferred_element_type=jnp.float32) + (mk + bias_ref[g])
        p = jnp.exp(s - s.max(axis=1, keepdims=True))
        o = jnp.dot(p.astype(jnp.bfloat16), v_ref[g], preferred_element_type=jnp.float32)
        o_ref[g] = (o / p.sum(axis=1, keepdims=True)).astype(o_ref.dtype)
        return carry

    lax.fori_loop(0, D_KV_HEADS, group, 0, unroll=2)


def dsa_sample(q, kT, v, mask, bias):
    B, _, rows, _ = q.shape
    lp = kT.shape[-1]
    T = rows // D_GROUP
    return pl.pallas_call(
        _dsa_sample_kernel, grid=(B,),
        in_specs=[pl.BlockSpec((None, D_KV_HEADS, rows, D_HEAD_DIM), lambda b: (b, 0, 0, 0)),
                  pl.BlockSpec((None, D_KV_HEADS, D_HEAD_DIM, lp), lambda b: (b, 0, 0, 0)),
                  pl.BlockSpec((None, D_KV_HEADS, lp, D_HEAD_DIM), lambda b: (b, 0, 0, 0)),
                  pl.BlockSpec((None, lp // LANES, T, LANES), lambda b: (b, 0, 0, 0)),
                  pl.BlockSpec((D_KV_HEADS, rows, lp), lambda b: (0, 0, 0))],
        out_specs=pl.BlockSpec((None, D_KV_HEADS, rows, D_HEAD_DIM), lambda b: (b, 0, 0, 0)),
        out_shape=jax.ShapeDtypeStruct((B, D_KV_HEADS, rows, D_HEAD_DIM), jnp.bfloat16),
        compiler_params=_params("parallel"), name="dsa_sample",
    )(q, kT, v, mask, bias)


def _mla_sample_kernel(q_ref, k_ref, v_ref, o_ref, *, n_keys):
    for hh in range(MLA_HP):
        qs = slice(hh * MLA_QK, (hh + 1) * MLA_QK)
        vs = slice(hh * MLA_V, (hh + 1) * MLA_V)
        s = lax.dot_general(q_ref[:, qs], k_ref[:, qs], (((1,), (1,)), ((), ())),
                            preferred_element_type=jnp.float32)
        col = lax.broadcasted_iota(jnp.int32, s.shape, 1)
        s = jnp.where(col < n_keys, s, NEG_INF)
        p = jnp.exp(s - s.max(axis=1, keepdims=True))
        o = jnp.dot(p.astype(jnp.bfloat16), v_ref[:, vs], preferred_element_type=jnp.float32)
        o_ref[:, vs] = (o / p.sum(axis=1, keepdims=True)).astype(o_ref.dtype)


def mla_sample(qc, kc, v, n_keys):
    B, T, _ = qc.shape
    lp = kc.shape[1]
    qk_w, v_w = MLA_HP * MLA_QK, MLA_HP * MLA_V
    return pl.pallas_call(
        functools.partial(_mla_sample_kernel, n_keys=n_keys), grid=(B, C_HEADS // MLA_HP),
        in_specs=[pl.BlockSpec((None, T, qk_w), lambda b, h: (b, 0, h)),
                  pl.BlockSpec((None, lp, qk_w), lambda b, h: (b, 0, h)),
                  pl.BlockSpec((None, lp, v_w), lambda b, h: (b, 0, h))],
        out_specs=pl.BlockSpec((None, T, v_w), lambda b, h: (b, 0, h)),
        out_shape=jax.ShapeDtypeStruct((B, T, C_HEADS * MLA_V), jnp.bfloat16),
        compiler_params=_params("parallel", "parallel"), name="mla_sample",
    )(qc, kc, v)


def dsa_sample_mix(dq, dk_all, dv_all, iq, ik_all, iw, rel_bias, n_sel):
    B, T, _ = dq.shape
    L = dk_all.shape[1]
    lp = SAMPLE_KEYS_PAD
    bf = jnp.bfloat16
    ke, ko = indexer_keys(ik_all, lp)
    w = iw * (IDX_HEADS ** -0.5) * (IDX_DIM ** -0.5)
    mask = indexer_mask(iq.astype(bf), w, ke, ko, n_sel, tq=T, pos0=L - T, n_keys=L)
    q = (dq * D_HEAD_DIM ** -0.5).astype(bf).reshape(B, T, D_KV_HEADS, D_GROUP, D_HEAD_DIM)
    q = q.transpose(0, 2, 3, 1, 4).reshape(B, D_KV_HEADS, D_GROUP * T, D_HEAD_DIM)
    kp = jnp.pad(dk_all.astype(bf), ((0, 0), (0, lp - L), (0, 0))).reshape(B, lp, D_KV_HEADS, D_HEAD_DIM)
    vp = jnp.pad(dv_all.astype(bf), ((0, 0), (0, lp - L), (0, 0))).reshape(B, lp, D_KV_HEADS, D_HEAD_DIM)
    rel = jnp.arange(lp)[None, :] - (L - T + jnp.arange(T))[:, None]
    bias = rel_bias[t5_bucket(rel)].transpose(2, 0, 1).reshape(D_KV_HEADS, D_GROUP * T, lp)
    o = dsa_sample(q, kp.transpose(0, 2, 3, 1), vp.transpose(0, 2, 1, 3), mask, bias)
    o = o.reshape(B, D_KV_HEADS, D_GROUP, T, D_HEAD_DIM).transpose(0, 3, 1, 2, 4)
    return o.reshape(B, T, D_HEADS * D_HEAD_DIM)


def odd_mix(z, pos, g_qa, g_kva, w_qb, w_kvb, rel_bias, cache):
    B, T, _ = z.shape
    q_lat, kv_lat, kpe_raw, dq, dk, dv, iq, ik, iw = odd_split(z)
    qn = rmsnorm(q_lat, g_qa).reshape(B * T, Q_LORA)
    q = matmul(qn, w_qb, tm=min(ROW_BLOCK, B * T)).reshape(B, T, C_HEADS, NOPE_DIM + ROPE_DIM)
    q_nope = q[..., :NOPE_DIM]
    q_pe = apply_rope(q[..., NOPE_DIM:], pos)
    lat = rmsnorm(kv_lat, g_kva)
    kpe = apply_rope(kpe_raw, pos)
    bf = jnp.bfloat16
    new = (lat, kpe, dk.reshape(B, T, D_KV_HEADS, D_HEAD_DIM), dv.reshape(B, T, D_KV_HEADS, D_HEAD_DIM), ik)
    qpad = jnp.zeros((B, T, C_HEADS, MLA_QK - NOPE_DIM - ROPE_DIM), jnp.float32)
    qc = (jnp.concatenate([q_nope, q_pe, qpad], axis=-1) * (NOPE_DIM + ROPE_DIM) ** -0.5).astype(bf)
    qc = qc.reshape(B, T, C_HEADS * MLA_QK)
    if cache is None:
        lat_all, kpe_all, dk_all, dv_all, ik_all = lat, kpe, dk, dv, ik
    else:
        c_lat, c_kpe, c_dk, c_dv, c_ik = cache
        lat_all = jnp.concatenate([c_lat, lat], axis=1)
        kpe_all = jnp.concatenate([c_kpe, kpe], axis=1)
        dk_all = jnp.concatenate([c_dk.reshape(B, -1, D_KV_HEADS * D_HEAD_DIM), dk], axis=1)
        dv_all = jnp.concatenate([c_dv.reshape(B, -1, D_KV_HEADS * D_HEAD_DIM), dv], axis=1)
        ik_all = jnp.concatenate([c_ik, ik], axis=1)
    L = lat_all.shape[1]
    kv = matmul(lat_all.reshape(B * L, KV_LORA), w_kvb).reshape(B, L, C_HEADS, NOPE_DIM + V_DIM)
    k_nope, v_c = kv[..., :NOPE_DIM], kv[..., NOPE_DIM:]
    kpad = jnp.zeros((B, L, C_HEADS, MLA_QK - NOPE_DIM - ROPE_DIM), jnp.float32)
    kc = jnp.concatenate([k_nope, jnp.broadcast_to(kpe_all[:, :, None, :], (B, L, C_HEADS, ROPE_DIM)), kpad],
                         axis=-1).astype(bf).reshape(B, L, C_HEADS * MLA_QK)
    vc = v_c.astype(bf).reshape(B, L, C_HEADS * V_DIM)
    n_sel = min(TOPK_MAX, L // 4)
    if cache is None:
        oc = mla_prompt(qc, kc, vc)
        od = dsa_prompt_mix(dq, dk, dv, iq, ik, iw, rel_bias, n_sel)
    else:
        rows = ((0, 0), (0, SAMPLE_KEYS_PAD - L), (0, 0))
        oc = mla_sample(qc, jnp.pad(kc, rows), jnp.pad(vc, rows), L)
        od = dsa_sample_mix(dq, dk_all, dv_all, iq, ik_all, iw, rel_bias, n_sel)
    return jnp.concatenate([oc, od], axis=-1), new


def _final_norm_kernel(x_ref, g_ref, o_ref):
    x = x_ref[...]
    o_ref[...] = x * lax.rsqrt(jnp.mean(x * x, axis=-1, keepdims=True) + RMS_EPS) * g_ref[...]


def final_norm(x, g):
    row = pl.BlockSpec((ROW_BLOCK, D_MODEL), lambda i: (i, 0))
    return pl.pallas_call(
        _final_norm_kernel, grid=(N_ROW_BLOCKS,),
        in_specs=[row, pl.BlockSpec((1, D_MODEL), lambda i: (0, 0))], out_specs=row,
        out_shape=jax.ShapeDtypeStruct((N_TOK, D_MODEL), jnp.float32),
        compiler_params=_params("parallel"), name="final_norm")(x, g.reshape(1, D_MODEL))


def kernel(x_prompt, x_sample, c_prompt, c_sample, cache_a_k, cache_a_v, state_b_pool, cache_c_latent, cache_c_kpe, cache_d_k, cache_d_v, cache_d_idx, rel_bias, ada_mix_w, ada_mix_b, ada_ffn_w, ada_ffn_b, norm_mix, norm_ffn, norm_final, even_w_in, even_w_out, a_sinks, pool_w, pool_scale, odd_w_in, odd_w_out, c_q_norm, c_kv_norm, c_w_qb, c_w_kvb, moe_router_group, moe_router_expert, moe_w_gate, moe_w_up, moe_w_down):
    bf = jnp.bfloat16
    x = jnp.concatenate([x_prompt.reshape(N_PROMPT, D_MODEL), x_sample.reshape(N_SAMPLE, D_MODEL)], axis=0)
    c = jnp.concatenate([c_prompt, c_sample], axis=0)
    mod_mix = modulation_all(c, ada_mix_w, ada_mix_b)
    mod_ffn = modulation_all(c, ada_ffn_w, ada_ffn_b)
    pos_p = jnp.arange(SEQ)
    pos_s = PAST_LEN + jnp.arange(DEC_SEQ)
    w_router = jnp.concatenate(
        [moe_router_group, moe_router_expert,
         jnp.zeros((DEPTH, D_MODEL, ROUTER_PAD - N_GROUPS - N_EXPERTS), jnp.float32)], axis=-1)
    new_p = [[] for _ in range(8)]
    new_s = [[] for _ in range(8)]
    for l in range(DEPTH):
        j = l // 2
        shift, scale, gate = jnp.split(mod_mix[l], 3, axis=-1)
        h = norm_mod(x, norm_mix[l], scale, shift)
        if l % 2 == 0:
            z = matmul(h, even_w_in[j].astype(bf))
            zp = z[:N_PROMPT].reshape(BATCH, SEQ, -1)
            zs = z[N_PROMPT:].reshape(DEC_BATCH, DEC_SEQ, -1)
            mp, st_p = even_mix(zp, pos_p, a_sinks[j], pool_w[j], pool_scale[j], rel_bias, None)
            ms, st_s = even_mix(zs, pos_s, a_sinks[j], pool_w[j], pool_scale[j], rel_bias,
                                (cache_a_k[j], cache_a_v[j], state_b_pool[j]))
            off = 0
            w_out = even_w_out[j]
        else:
            z = matmul(h, odd_in_weight(odd_w_in[j]), tn=ODD_IN_PAD // 9)
            zp = z[:N_PROMPT].reshape(BATCH, SEQ, -1)
            zs = z[N_PROMPT:].reshape(DEC_BATCH, DEC_SEQ, -1)
            wq, wkv = c_w_qb[j].astype(bf), c_w_kvb[j].astype(bf)
            mp, st_p = odd_mix(zp, pos_p, c_q_norm[j], c_kv_norm[j], wq, wkv, rel_bias, None)
            ms, st_s = odd_mix(zs, pos_s, c_q_norm[j], c_kv_norm[j], wq, wkv, rel_bias,
                               (cache_c_latent[j], cache_c_kpe[j], cache_d_k[j], cache_d_v[j], cache_d_idx[j]))
            off = 3
            w_out = odd_w_out[j]
        for i, s in enumerate(st_p):
            new_p[off + i].append(s)
        for i, s in enumerate(st_s):
            new_s[off + i].append(s)
        mix = jnp.concatenate([mp.reshape(N_PROMPT, -1), ms.reshape(N_SAMPLE, -1)], axis=0).astype(bf)
        x = matmul_resid(mix, w_out.astype(bf), x, gate)
        shift, scale, gate = jnp.split(mod_ffn[l], 3, axis=-1)
        h, logits = norm_mod(x, norm_ffn[l], scale, shift, w_router[l])
        x = moe_ffn(x, h, logits, gate, moe_w_gate[l].astype(bf), moe_w_up[l].astype(bf), moe_w_down[l].astype(bf))
    y = final_norm(x, norm_final)
    st_p = [jnp.stack(s) for s in new_p]
    st_s = [jnp.stack(s) for s in new_s]
    a_k_p, a_v_p, pool_p, lat_p, kpe_p, dk_p, dv_p, didx_p = st_p
    a_k_s, a_v_s, pool_s, lat_s, kpe_s, dk_s, dv_s, didx_s = st_s
    return (y[:N_PROMPT].reshape(BATCH, SEQ, D_MODEL), y[N_PROMPT:].reshape(DEC_BATCH, DEC_SEQ, D_MODEL),
            a_k_p, a_k_s, a_v_p, a_v_s, pool_p, pool_s, lat_p, lat_s,
            kpe_p, kpe_s, dk_p, dk_s, dv_p, dv_s, didx_p, didx_s)
```

```python
import functools
import math

import jax
import jax.numpy as jnp
from jax import lax
from jax.experimental import pallas as pl
from jax.experimental.pallas import tpu as pltpu

D_MODEL = 4096
BATCH = 2
SEQ = 8192
DEPTH = 4
DEC_BATCH = 16
DEC_SEQ = 32
PAST_LEN = 1024
CHUNK = 64
QBLOCK = 128
N_PAIR = DEPTH // 2
A_HEADS = 32
A_KV_HEADS = 8
A_HEAD_DIM = 64
A_GROUP = A_HEADS // A_KV_HEADS
WINDOW = 128
WIN_CHUNKS = WINDOW // CHUNK
POOL_WIDTH = 2048
POOL_GROUPS = 4
POOL_GROUP_WIDTH = POOL_WIDTH // POOL_GROUPS
POOL_WINDOWS = (2, 4, 8, 16)
POOL_HIST = max(POOL_WINDOWS) - 1
C_HEADS = 16
Q_LORA = 1024
KV_LORA = 512
NOPE_DIM = 128
ROPE_DIM = 64
V_DIM = 128
ROPE_THETA = 10000.0
D_HEADS = 32
D_KV_HEADS = 8
D_HEAD_DIM = 64
D_GROUP = D_HEADS // D_KV_HEADS
IDX_HEADS = 32
IDX_DIM = 64
TOPK_MAX = 256
NUM_BUCKETS = 32
MAX_DISTANCE = 128
N_GROUPS = 8
EXPERTS_PER_GROUP = 8
N_EXPERTS = N_GROUPS * EXPERTS_PER_GROUP
TOP_K = 2
D_EXPERT = 512
RMS_EPS = 1e-6
NEG_INF = -1e30

EVEN_SPLITS = (A_HEADS * A_HEAD_DIM, A_KV_HEADS * A_HEAD_DIM, A_KV_HEADS * A_HEAD_DIM, POOL_WIDTH)
ODD_SPLITS = (Q_LORA, KV_LORA, ROPE_DIM, D_HEADS * D_HEAD_DIM, D_KV_HEADS * D_HEAD_DIM,
              D_KV_HEADS * D_HEAD_DIM, IDX_HEADS * IDX_DIM, IDX_DIM, IDX_HEADS)
ODD_IN = sum(ODD_SPLITS)
ODD_IN_PAD = -(-ODD_IN // 128) * 128
ODD_ORDER = (3, 6, 0, 1, 4, 5, 2, 7, 8)


def odd_in_weight(w):
    segs = split_cols(w, ODD_SPLITS)
    out = jnp.concatenate([segs[s] for s in ODD_ORDER], axis=-1).astype(jnp.bfloat16)
    return jnp.pad(out, ((0, 0), (0, ODD_IN_PAD - ODD_IN)))


def odd_split(z):
    parts = split_cols(z, [ODD_SPLITS[s] for s in ODD_ORDER])
    out = [None] * len(ODD_ORDER)
    for pos, s in enumerate(ODD_ORDER):
        out[s] = parts[pos]
    return out

N_PROMPT = BATCH * SEQ
N_SAMPLE = DEC_BATCH * DEC_SEQ
N_TOK = N_PROMPT + N_SAMPLE
N_COND = BATCH + DEC_BATCH

ROW_BLOCK = 512
ROW_GROUPS = ROW_BLOCK // DEC_SEQ
N_ROW_BLOCKS = N_TOK // ROW_BLOCK
MOE_ROWS = 256
ROUTER_PAD = 128
VMEM_LIMIT = 56 * 1024 * 1024

assert N_SAMPLE == ROW_BLOCK and SEQ % ROW_BLOCK == 0 and N_TOK % ROW_BLOCK == 0


def _params(*sem):
    return pltpu.CompilerParams(dimension_semantics=sem, vmem_limit_bytes=VMEM_LIMIT)


def split_cols(z, sizes):
    out, o = [], 0
    for s in sizes:
        out.append(z[..., o:o + s])
        o += s
    return out


def _modulation_kernel(c_ref, w_ref, b_ref, o_ref):
    c = c_ref[...]
    a = (c * jax.nn.sigmoid(c)).astype(jnp.bfloat16)
    o_ref[0] = jnp.dot(a, w_ref[0].astype(jnp.bfloat16), preferred_element_type=jnp.float32) + b_ref[0]


def modulation_all(c, w, b):
    tn = 1024
    depth, d, n = w.shape
    return pl.pallas_call(
        _modulation_kernel,
        grid=(depth, n // tn),
        in_specs=[pl.BlockSpec((N_COND, d), lambda l, j: (0, 0)),
                  pl.BlockSpec((1, d, tn), lambda l, j: (l, 0, j)),
                  pl.BlockSpec((1, 1, tn), lambda l, j: (l, 0, j))],
        out_specs=pl.BlockSpec((1, N_COND, tn), lambda l, j: (l, 0, j)),
        out_shape=jax.ShapeDtypeStruct((depth, N_COND, n), jnp.float32),
        compiler_params=_params("parallel", "parallel"),
        name="modulation",
    )(c, w, b.reshape(depth, 1, n))


def expand_rows(m, block_rows=ROW_BLOCK):
    d = m.shape[-1]
    groups = block_rows // DEC_SEQ
    p = jnp.broadcast_to(m[:BATCH, None, None, :], (BATCH, SEQ // block_rows, groups, d))
    return jnp.concatenate([p.reshape(N_PROMPT // block_rows, groups, d),
                            m[BATCH:].reshape(N_SAMPLE // block_rows, groups, d)], axis=0)


def _norm_mod(x_ref, g_ref, scale_ref, shift_ref):
    x = x_ref[...]
    y = x * lax.rsqrt(jnp.mean(x * x, axis=-1, keepdims=True) + RMS_EPS) * g_ref[...]
    y = y.reshape(ROW_GROUPS, DEC_SEQ, D_MODEL)
    y = y * (1.0 + scale_ref[0][:, None, :]) + shift_ref[0][:, None, :]
    return y.reshape(ROW_BLOCK, D_MODEL)


def _norm_mod_kernel(x_ref, g_ref, scale_ref, shift_ref, h_ref):
    h_ref[...] = _norm_mod(x_ref, g_ref, scale_ref, shift_ref).astype(jnp.bfloat16)


def _norm_mod_router_kernel(x_ref, g_ref, scale_ref, shift_ref, whi_ref, wlo_ref, h_ref, lg_ref):
    h = _norm_mod(x_ref, g_ref, scale_ref, shift_ref)
    hi = h.astype(jnp.bfloat16)
    lo = (h - hi.astype(jnp.float32)).astype(jnp.bfloat16)
    h_ref[...] = h
    lg_ref[...] = (jnp.dot(hi, whi_ref[...], preferred_element_type=jnp.float32)
                   + jnp.dot(lo, whi_ref[...], preferred_element_type=jnp.float32)
                   + jnp.dot(hi, wlo_ref[...], preferred_element_type=jnp.float32))


def norm_mod(x, g, scale, shift, w_router=None):
    row = pl.BlockSpec((ROW_BLOCK, D_MODEL), lambda i: (i, 0))
    grp = pl.BlockSpec((1, ROW_GROUPS, D_MODEL), lambda i: (i, 0, 0))
    in_specs = [row, pl.BlockSpec((1, D_MODEL), lambda i: (0, 0)), grp, grp]
    h_shape = jax.ShapeDtypeStruct((N_TOK, D_MODEL), jnp.bfloat16)
    args = (x, g.reshape(1, D_MODEL), expand_rows(scale), expand_rows(shift))
    if w_router is None:
        return pl.pallas_call(
            _norm_mod_kernel, grid=(N_ROW_BLOCKS,), in_specs=in_specs, out_specs=row, out_shape=h_shape,
            compiler_params=_params("parallel"), name="norm_mod")(*args)
    w_hi = w_router.astype(jnp.bfloat16)
    w_lo = (w_router - w_hi.astype(jnp.float32)).astype(jnp.bfloat16)
    wspec = pl.BlockSpec((D_MODEL, ROUTER_PAD), lambda i: (0, 0))
    return pl.pallas_call(
        _norm_mod_router_kernel, grid=(N_ROW_BLOCKS,), in_specs=in_specs + [wspec, wspec],
        out_specs=(row, pl.BlockSpec((ROW_BLOCK, ROUTER_PAD), lambda i: (i, 0))),
        out_shape=(jax.ShapeDtypeStruct((N_TOK, D_MODEL), jnp.float32),
                   jax.ShapeDtypeStruct((N_TOK, ROUTER_PAD), jnp.float32)),
        compiler_params=_params("parallel"), name="norm_mod_router")(*args, w_hi, w_lo)


def _matmul_kernel(a_ref, b_ref, o_ref):
    o_ref[...] = jnp.dot(a_ref[...].astype(jnp.bfloat16), b_ref[...],
                         preferred_element_type=jnp.float32).astype(o_ref.dtype)


def matmul(a, b, *, tm=ROW_BLOCK, tn=512, out_dtype=jnp.float32):
    m, k = a.shape
    n = b.shape[1]
    tm, tn = min(tm, m), min(tn, n)
    assert m % tm == 0 and n % tn == 0, (a.shape, b.shape, tm, tn)
    return pl.pallas_call(
        _matmul_kernel, grid=(m // tm, n // tn),
        in_specs=[pl.BlockSpec((tm, k), lambda i, j: (i, 0)), pl.BlockSpec((k, tn), lambda i, j: (0, j))],
        out_specs=pl.BlockSpec((tm, tn), lambda i, j: (i, j)),
        out_shape=jax.ShapeDtypeStruct((m, n), out_dtype),
        compiler_params=_params("parallel", "parallel"), name="matmul",
    )(a, b)


def _matmul_resid_kernel(a_ref, b_ref, x_ref, gate_ref, o_ref):
    y = jnp.dot(a_ref[...], b_ref[...], preferred_element_type=jnp.float32)
    tn = y.shape[-1]
    y = y.reshape(ROW_GROUPS, DEC_SEQ, tn) * gate_ref[0][:, None, :]
    o_ref[...] = x_ref[...] + y.reshape(ROW_BLOCK, tn)


def matmul_resid(a, b, x, gate, *, tn=512):
    k = a.shape[1]
    return pl.pallas_call(
        _matmul_resid_kernel, grid=(N_ROW_BLOCKS, D_MODEL // tn),
        in_specs=[pl.BlockSpec((ROW_BLOCK, k), lambda i, j: (i, 0)),
                  pl.BlockSpec((k, tn), lambda i, j: (0, j)),
                  pl.BlockSpec((ROW_BLOCK, tn), lambda i, j: (i, j)),
                  pl.BlockSpec((1, ROW_GROUPS, tn), lambda i, j: (i, 0, j))],
        out_specs=pl.BlockSpec((ROW_BLOCK, tn), lambda i, j: (i, j)),
        out_shape=jax.ShapeDtypeStruct((N_TOK, D_MODEL), jnp.float32),
        compiler_params=_params("parallel", "parallel"), name="matmul_resid",
    )(a, b, x, expand_rows(gate))


GATHER_UNROLL = 8


def _start_row_gather(idx_ref, src_hbm, dst_buf, sem, n_rows):
    def body(r, carry):
        pltpu.make_async_copy(src_hbm.at[pl.ds(idx_ref[0, r], 1)], dst_buf.at[pl.ds(r, 1)], sem).start()
        return carry

    lax.fori_loop(0, n_rows, body, 0, unroll=GATHER_UNROLL)


def _wait_row_gather(src_hbm, dst_buf, sem, n_rows):
    pltpu.make_async_copy(src_hbm.at[pl.ds(0, n_rows)], dst_buf, sem).wait()


def _expert_kernel(block_e_ref, n_active_ref, tok_ref, tok_next_ref, h_hbm, wg_ref, wu_ref, wd_ref, gate_ref,
                   o_ref, xbuf, sem):
    i = pl.program_id(0)
    n_act = n_active_ref[0]
    slot = i % 2

    @pl.when(i == 0)
    def _():
        _start_row_gather(tok_ref, h_hbm, xbuf.at[0], sem.at[0], MOE_ROWS)

    @pl.when(i + 1 < n_act)
    def _():
        _start_row_gather(tok_next_ref, h_hbm, xbuf.at[1 - slot], sem.at[1 - slot], MOE_ROWS)

    @pl.when(i < n_act)
    def _():
        _wait_row_gather(h_hbm, xbuf.at[slot], sem.at[slot], MOE_ROWS)
        x = xbuf[slot].astype(jnp.bfloat16)
        g = jnp.dot(x, wg_ref[0], preferred_element_type=jnp.float32)
        u = jnp.dot(x, wu_ref[0], preferred_element_type=jnp.float32)
        a = (g * jax.nn.sigmoid(g) * u).astype(jnp.bfloat16)
        o_ref[...] = jnp.dot(a, wd_ref[0], preferred_element_type=jnp.float32) * gate_ref[...]

    @pl.when(i >= n_act)
    def _():
        o_ref[...] = jnp.zeros_like(o_ref)


def expert_blocks(h, tok_of_row, gate_of_row, block_e, n_active, w_gate, w_up, w_down):
    n_blocks = tok_of_row.shape[0]

    def smem_blk(index_map):
        return pl.BlockSpec((None, 1, MOE_ROWS), index_map, memory_space=pltpu.SMEM)

    return pl.pallas_call(
        _expert_kernel,
        grid_spec=pltpu.PrefetchScalarGridSpec(
            num_scalar_prefetch=2, grid=(n_blocks,),
            in_specs=[smem_blk(lambda i, be, na: (i, 0, 0)),
                      smem_blk(lambda i, be, na: (jnp.minimum(i + 1, n_blocks - 1), 0, 0)),
                      pl.BlockSpec(memory_space=pl.ANY),
                      pl.BlockSpec((1, D_MODEL, D_EXPERT), lambda i, be, na: (be[i], 0, 0)),
                      pl.BlockSpec((1, D_MODEL, D_EXPERT), lambda i, be, na: (be[i], 0, 0)),
                      pl.BlockSpec((1, D_EXPERT, D_MODEL), lambda i, be, na: (be[i], 0, 0)),
                      pl.BlockSpec((MOE_ROWS, 1), lambda i, be, na: (i, 0))],
            out_specs=pl.BlockSpec((MOE_ROWS, D_MODEL), lambda i, be, na: (i, 0)),
            scratch_shapes=[pltpu.VMEM((2, MOE_ROWS, D_MODEL), jnp.float32), pltpu.SemaphoreType.DMA((2,))]),
        out_shape=jax.ShapeDtypeStruct((n_blocks * MOE_ROWS, D_MODEL), jnp.float32),
        compiler_params=_params("arbitrary"), name="moe_experts",
    )(block_e, n_active, tok_of_row, tok_of_row, h, w_gate, w_up, w_down, gate_of_row)


COMBINE_ROWS = 256
COMBINE_GROUPS = COMBINE_ROWS // DEC_SEQ


def _combine_kernel(pos_ref, pos_next_ref, y_hbm, x_ref, gate_ref, o_ref, ybuf, sem):
    i = pl.program_id(0)
    n = pl.num_programs(0)
    slot = i % 2
    n_rows = TOP_K * COMBINE_ROWS

    @pl.when(i == 0)
    def _():
        _start_row_gather(pos_ref, y_hbm, ybuf.at[0], sem.at[0], n_rows)

    @pl.when(i + 1 < n)
    def _():
        _start_row_gather(pos_next_ref, y_hbm, ybuf.at[1 - slot], sem.at[1 - slot], n_rows)

    _wait_row_gather(y_hbm, ybuf.at[slot], sem.at[slot], n_rows)
    f = ybuf[slot, :COMBINE_ROWS] + ybuf[slot, COMBINE_ROWS:]
    f = f.reshape(COMBINE_GROUPS, DEC_SEQ, D_MODEL) * gate_ref[0][:, None, :]
    o_ref[...] = x_ref[...] + f.reshape(COMBINE_ROWS, D_MODEL)


def moe_combine(y, pos, x, gate):
    nb = N_TOK // COMBINE_ROWS

    def smem_blk(index_map):
        return pl.BlockSpec((None, 1, TOP_K * COMBINE_ROWS), index_map, memory_space=pltpu.SMEM)

    return pl.pallas_call(
        _combine_kernel, grid=(nb,),
        in_specs=[smem_blk(lambda i: (i, 0, 0)), smem_blk(lambda i: (jnp.minimum(i + 1, nb - 1), 0, 0)),
                  pl.BlockSpec(memory_space=pl.ANY),
                  pl.BlockSpec((COMBINE_ROWS, D_MODEL), lambda i: (i, 0)),
                  pl.BlockSpec((1, COMBINE_GROUPS, D_MODEL), lambda i: (i, 0, 0))],
        out_specs=pl.BlockSpec((COMBINE_ROWS, D_MODEL), lambda i: (i, 0)),
        out_shape=jax.ShapeDtypeStruct((N_TOK, D_MODEL), jnp.float32),
        scratch_shapes=[pltpu.VMEM((2, TOP_K * COMBINE_ROWS, D_MODEL), jnp.float32),
                        pltpu.SemaphoreType.DMA((2,))],
        compiler_params=_params("arbitrary"), name="moe_combine",
    )(pos, pos, y, x, expand_rows(gate, COMBINE_ROWS))


def moe_ffn(x, h, logits, gate, w_gate, w_up, w_down):
    n = h.shape[0]
    lg = logits[:, :N_GROUPS]
    grp = jnp.argmax(lg, axis=-1)
    p_grp = jnp.take_along_axis(jax.nn.softmax(lg, axis=-1), grp[:, None], axis=-1)
    le = logits[:, N_GROUPS:N_GROUPS + N_EXPERTS].reshape(n, N_GROUPS, EXPERTS_PER_GROUP)
    le = jnp.take_along_axis(le, grp[:, None, None], axis=1)[:, 0]
    pe, ie = lax.top_k(jax.nn.softmax(le, axis=-1), TOP_K)
    gates = p_grp * pe / pe.sum(-1, keepdims=True)
    expert = (grp[:, None] * EXPERTS_PER_GROUP + ie).astype(jnp.int32)
    m = n * TOP_K
    flat_e = expert.reshape(m)
    onehot = (flat_e[:, None] == jnp.arange(N_EXPERTS, dtype=jnp.int32)[None, :]).astype(jnp.int32)
    csum = jnp.cumsum(onehot, axis=0)
    rank = jnp.take_along_axis(csum, flat_e[:, None], axis=1)[:, 0] - 1
    sizes = csum[-1]
    padded = (sizes + MOE_ROWS - 1) // MOE_ROWS * MOE_ROWS
    pend = jnp.cumsum(padded)
    pstart = pend - padded
    dest = (pstart[flat_e] + rank).astype(jnp.int32)
    n_blocks = -(-m // MOE_ROWS) + N_EXPERTS
    total = n_blocks * MOE_ROWS
    tok_of_row = jnp.zeros((total,), jnp.int32).at[dest].set(jnp.arange(m, dtype=jnp.int32) // TOP_K)
    gate_of_row = jnp.zeros((total,), jnp.float32).at[dest].set(gates.reshape(m))
    block_start = jnp.arange(n_blocks, dtype=jnp.int32) * MOE_ROWS
    block_e = jnp.minimum(jnp.searchsorted(pend, block_start, side='right'), N_EXPERTS - 1).astype(jnp.int32)
    n_active = (pend[-1] // MOE_ROWS).astype(jnp.int32).reshape(1)
    y = expert_blocks(h, tok_of_row.reshape(n_blocks, 1, MOE_ROWS), gate_of_row.reshape(total, 1), block_e, n_active,
                      w_gate, w_up, w_down)
    nb = n // COMBINE_ROWS
    pos = dest.reshape(nb, COMBINE_ROWS, TOP_K).transpose(0, 2, 1).reshape(nb, 1, TOP_K * COMBINE_ROWS)
    return moe_combine(y, pos, x, gate)


def t5_bucket(rel):
    half = NUM_BUCKETS // 2
    max_exact = half // 2
    ret = jnp.where(rel > 0, half, 0)
    n = jnp.abs(rel)
    nf = jnp.maximum(n, 1).astype(jnp.float32)
    large = max_exact + (jnp.log(nf / max_exact) / math.log(MAX_DISTANCE / max_exact)
                         * (half - max_exact)).astype(jnp.int32)
    large = jnp.minimum(large, half - 1)
    return ret + jnp.where(n < max_exact, n, large)


def head_bias(rel, table, n_kv, group):
    b = table[t5_bucket(rel)]
    return jnp.moveaxis(b, -1, 0).reshape((n_kv, group) + rel.shape)


def apply_rope(x, pos):
    inv = jnp.exp(-math.log(ROPE_THETA) * jnp.arange(0, ROPE_DIM, 2, dtype=jnp.float32) / ROPE_DIM)
    ang = pos.astype(jnp.float32)[:, None] * inv[None, :]
    ang = ang.reshape(ang.shape[:1] + (1,) * (x.ndim - 3) + ang.shape[1:])
    cos, sin = jnp.cos(ang), jnp.sin(ang)
    x1, x2 = jnp.split(x.astype(jnp.float32), 2, axis=-1)
    return jnp.concatenate([x1 * cos - x2 * sin, x2 * cos + x1 * sin], axis=-1)


def rmsnorm(x, g):
    return x * lax.rsqrt(jnp.mean(x * x, axis=-1, keepdims=True) + RMS_EPS) * g


def sink_attention(q, k, v, bias, valid, sinks):
    n, tq = q.shape[:2]
    s = jnp.einsum('nqkgd,nskd->nkgqs', q, k) * A_HEAD_DIM ** -0.5 + bias
    s = jnp.where(valid[:, None, None], s, NEG_INF)
    sink = sinks.reshape(1, A_KV_HEADS, A_GROUP, 1, 1)
    mx = jnp.maximum(s.max(-1, keepdims=True), sink)
    e = jnp.exp(s - mx)
    p = e / (e.sum(-1, keepdims=True) + jnp.exp(sink - mx))
    o = jnp.einsum('nkgqs,nskd->nqkgd', p, v)
    return o.reshape(n, tq, A_HEADS * A_HEAD_DIM)


def window_attention_prompt(q, k, v, sinks, rel_bias):
    B, S = q.shape[:2]
    nc = S // CHUNK
    band = (WIN_CHUNKS + 1) * CHUNK

    def banded(t):
        tc = t.reshape(B, nc, CHUNK, A_KV_HEADS, A_HEAD_DIM)
        tp = jnp.concatenate([jnp.zeros_like(tc[:, :WIN_CHUNKS]), tc], axis=1)
        tb = jnp.concatenate([tp[:, j:j + nc] for j in range(WIN_CHUNKS + 1)], axis=2)
        return tb.reshape(B * nc, band, A_KV_HEADS, A_HEAD_DIM)

    qi = jnp.arange(CHUNK)[:, None]
    kr = jnp.arange(band)[None, :]
    rel = kr - WIN_CHUNKS * CHUNK - qi
    key_chunk = jnp.arange(nc)[:, None] - WIN_CHUNKS + kr // CHUNK
    valid = jnp.broadcast_to((key_chunk >= 0)[None, :, None, :], (B, nc, CHUNK, band)).reshape(B * nc, CHUNK, band)
    qb = q.reshape(B * nc, CHUNK, A_KV_HEADS, A_GROUP, A_HEAD_DIM)
    o = sink_attention(qb, banded(k), banded(v), head_bias(rel, rel_bias, A_KV_HEADS, A_GROUP), valid, sinks)
    return o.reshape(B, S, A_HEADS * A_HEAD_DIM)


def window_attention_sample(q, k_all, v_all, pos, W, sinks, rel_bias):
    B, T = q.shape[:2]
    k_pos = jnp.concatenate([PAST_LEN - W + jnp.arange(W), pos])
    qc = (pos // CHUNK)[:, None]
    kc = (k_pos // CHUNK)[None, :]
    valid = (kc <= qc) & (kc >= qc - WIN_CHUNKS)
    valid = jnp.broadcast_to(valid[None], (B,) + valid.shape)
    bias = head_bias(k_pos[None, :] - pos[:, None], rel_bias, A_KV_HEADS, A_GROUP)
    return sink_attention(q, k_all, v_all, bias, valid, sinks)


def pool_mix(u_ext, pos, n_hist, pool_w, pool_scale):
    B, L, P = u_ext.shape
    T = L - n_hist
    cs = jnp.cumsum(u_ext, axis=1)
    cs = jnp.concatenate([jnp.zeros_like(cs[:, :1]), cs], axis=1)
    end = n_hist + jnp.arange(T) + 1
    outs = []
    for g, w in enumerate(POOL_WINDOWS):
        lo, hi = g * POOL_GROUP_WIDTH, (g + 1) * POOL_GROUP_WIDTH
        csg = cs[..., lo:hi]
        start = jnp.maximum(end - w, 0)
        cnt = jnp.minimum(w, pos + 1).astype(jnp.float32)[None, :, None]
        mean = (csg[:, end] - csg[:, start]) / cnt
        outs.append(mean - u_ext[:, n_hist:, lo:hi])
    d = jnp.stack(outs, axis=2)
    y = jnp.einsum('btgc,gcd->btgd', d, pool_w).reshape(B, T, P)
    return y * pool_scale


def even_mix(z, pos, sinks, pool_w, pool_scale, rel_bias, cache):
    B, T, _ = z.shape
    q, k, v, u = split_cols(z, EVEN_SPLITS)
    q = q.reshape(B, T, A_KV_HEADS, A_GROUP, A_HEAD_DIM)
    k = k.reshape(B, T, A_KV_HEADS, A_HEAD_DIM)
    v = v.reshape(B, T, A_KV_HEADS, A_HEAD_DIM)
    if cache is None:
        a = window_attention_prompt(q, k, v, sinks, rel_bias)
        p = pool_mix(u, pos, 0, pool_w, pool_scale)
        new = (k[:, -WINDOW:], v[:, -WINDOW:], u[:, -POOL_HIST:])
    else:
        k_cache, v_cache, u_cache = cache
        W = k_cache.shape[1]
        k_all = jnp.concatenate([k_cache, k], axis=1)
        v_all = jnp.concatenate([v_cache, v], axis=1)
        a = window_attention_sample(q, k_all, v_all, pos, W, sinks, rel_bias)
        u_all = jnp.concatenate([u_cache, u], axis=1)
        p = pool_mix(u_all, pos, POOL_HIST, pool_w, pool_scale)
        new = (k_all[:, -W:], v_all[:, -W:], u_all[:, -POOL_HIST:])
    return jnp.concatenate([a, p], axis=-1), new


MLA_T = 512
MLA_QK = 256
MLA_V = 128
MLA_HP = 2


MLA_VW = 2 * MLA_V


def _mla_kernel(q_ref, k_ref, v_ref, o_ref, m_ref, acc_ref):
    i = pl.program_id(2)
    k = pl.program_id(3)

    @pl.when(k == 0)
    def _():
        m_ref[...] = jnp.full_like(m_ref, NEG_INF)
        acc_ref[...] = jnp.zeros_like(acc_ref)

    def step(diagonal):
        for hh in range(MLA_HP):
            qs = slice(hh * MLA_QK, (hh + 1) * MLA_QK)
            vs = slice(hh * MLA_VW, (hh + 1) * MLA_VW)
            s = lax.dot_general(q_ref[:, qs], k_ref[:, qs], (((1,), (1,)), ((), ())),
                                preferred_element_type=jnp.float32)
            if diagonal:
                row = lax.broadcasted_iota(jnp.int32, s.shape, 0) // CHUNK
                col = lax.broadcasted_iota(jnp.int32, s.shape, 1) // CHUNK
                s = jnp.where(col <= row, s, NEG_INF)
            m_prev = m_ref[hh]
            m_new = jnp.maximum(m_prev, s.max(axis=1, keepdims=True))
            alpha = jnp.exp(m_prev - m_new)
            p = jnp.exp(s - m_new)
            acc_ref[:, vs] = alpha * acc_ref[:, vs] + jnp.dot(p.astype(jnp.bfloat16), v_ref[:, vs],
                                                              preferred_element_type=jnp.float32)
            m_ref[hh] = m_new

    @pl.when(k < i)
    def _():
        step(False)

    @pl.when(k == i)
    def _():
        step(True)
        for hh in range(MLA_HP):
            a0 = hh * MLA_VW
            o_ref[:, hh * MLA_V:(hh + 1) * MLA_V] = (
                acc_ref[:, a0:a0 + MLA_V] / acc_ref[:, a0 + MLA_V:a0 + MLA_V + 1]).astype(o_ref.dtype)


def mla_prompt(qc, kc, v):
    B, S, _ = qc.shape
    n = S // MLA_T
    qk_w, v_w, o_w = MLA_HP * MLA_QK, MLA_HP * MLA_VW, MLA_HP * MLA_V
    return pl.pallas_call(
        _mla_kernel, grid=(B, C_HEADS // MLA_HP, n, n),
        in_specs=[pl.BlockSpec((None, MLA_T, qk_w), lambda b, h, i, k: (b, i, h)),
                  pl.BlockSpec((None, MLA_T, qk_w), lambda b, h, i, k: (b, jnp.minimum(k, i), h)),
                  pl.BlockSpec((None, MLA_T, v_w), lambda b, h, i, k: (b, jnp.minimum(k, i), h))],
        out_specs=pl.BlockSpec((None, MLA_T, o_w), lambda b, h, i, k: (b, i, h)),
        out_shape=jax.ShapeDtypeStruct((B, S, C_HEADS * MLA_V), jnp.bfloat16),
        scratch_shapes=[pltpu.VMEM((MLA_HP, MLA_T, 1), jnp.float32), pltpu.VMEM((MLA_T, v_w), jnp.float32)],
        compiler_params=_params("parallel", "parallel", "parallel", "arbitrary"), name="mla_prompt",
    )(qc, kc, v)


IDX_TQ = 128
IDX_TK = 256
LANES = 128
INT_MIN = -2 ** 31


def _indexer_kernel(q_ref, w_ref, ke_ref, ko_ref, mask_ref, key_ref, *, n_sel, tq, pos0, n_keys):
    i = pl.program_id(1)
    q0 = pos0 + i * tq
    n_lane_tiles = mask_ref.shape[0]
    sub = IDX_TK // LANES
    n_kb = jnp.minimum((q0 + tq + IDX_TK - 1) // IDX_TK, n_lane_tiles // sub)
    row_chunk = (q0 + lax.broadcasted_iota(jnp.int32, (tq, LANES), 0)) // CHUNK
    lane_iota = lax.broadcasted_iota(jnp.int32, (tq, LANES), 1)
    padded_keys = n_keys < n_lane_tiles * LANES

    def admissible(j):
        col = j * LANES + lane_iota
        adm = col // CHUNK <= row_chunk
        if padded_keys:
            adm = jnp.logical_and(adm, col < n_keys)
        return adm

    def score_tile(kb, carry):
        ke = ke_ref[kb]
        ko = ko_ref[kb]
        acc = jnp.zeros((tq, IDX_TK), jnp.float32)
        for p in range(IDX_HEADS // 2):
            qp = q_ref[:, p * LANES:(p + 1) * LANES]
            se = jnp.dot(qp, ke, preferred_element_type=jnp.float32)
            so = jnp.dot(qp, ko, preferred_element_type=jnp.float32)
            acc = acc + jnp.maximum(se, 0.0) * w_ref[:, 2 * p:2 * p + 1]
            acc = acc + jnp.maximum(so, 0.0) * w_ref[:, 2 * p + 1:2 * p + 2]
        for t in range(sub):
            j = kb * sub + t
            a = jnp.where(admissible(j), acc[:, t * LANES:(t + 1) * LANES], -jnp.inf)
            bits = pltpu.bitcast(a, jnp.int32)
            key_ref[j] = bits ^ ((bits >> 31) & jnp.int32(0x7FFFFFFF))
        return carry

    lax.fori_loop(0, n_kb, score_tile, 0)
    n_tiles = n_kb * sub

    def count_ge(cand):
        def body(j, cnt):
            return cnt + jnp.where(key_ref[j] >= cand, 1, 0)
        cnt = lax.fori_loop(0, n_tiles, body, jnp.zeros((tq, LANES), jnp.int32))
        return cnt.sum(axis=1, keepdims=True)

    thr = jnp.where(count_ge(jnp.zeros((tq, 1), jnp.int32)) >= n_sel,
                    jnp.zeros((tq, 1), jnp.int32), jnp.full((tq, 1), INT_MIN, jnp.int32))

    def bit_step(b, thr):
        cand = thr | (jnp.int32(1) << (30 - b))
        return jnp.where(count_ge(cand) >= n_sel, cand, thr)

    thr = lax.fori_loop(0, 31, bit_step, thr)

    def write_mask(j, carry):
        sel = jnp.logical_and(key_ref[j] >= thr, admissible(j))
        mask_ref[j] = jnp.where(sel, 0.0, NEG_INF).astype(mask_ref.dtype)
        return carry

    lax.fori_loop(0, n_tiles, write_mask, 0)

    def write_rest(j, carry):
        mask_ref[j] = jnp.full((tq, LANES), NEG_INF, mask_ref.dtype)
        return carry

    lax.fori_loop(n_tiles, n_lane_tiles, write_rest, 0)


def indexer_mask(qi, wi, ke, ko, n_sel, *, tq, pos0, n_keys):
    B, T, _ = qi.shape
    nkb = ke.shape[1]
    L = nkb * IDX_TK
    return pl.pallas_call(
        functools.partial(_indexer_kernel, n_sel=n_sel, tq=tq, pos0=pos0, n_keys=n_keys), grid=(B, T // tq),
        in_specs=[pl.BlockSpec((None, tq, IDX_HEADS * IDX_DIM), lambda b, i: (b, i, 0)),
                  pl.BlockSpec((None, tq, IDX_HEADS), lambda b, i: (b, i, 0)),
                  pl.BlockSpec((None, nkb, LANES, IDX_TK), lambda b, i: (b, 0, 0, 0)),
                  pl.BlockSpec((None, nkb, LANES, IDX_TK), lambda b, i: (b, 0, 0, 0))],
        out_specs=pl.BlockSpec((None, L // LANES, tq, LANES), lambda b, i: (b, 0, i, 0)),
        out_shape=jax.ShapeDtypeStruct((B, L // LANES, T, LANES), jnp.bfloat16),
        scratch_shapes=[pltpu.VMEM((L // LANES, tq, LANES), jnp.int32)],
        compiler_params=_params("parallel", "parallel"), name="dsa_indexer",
    )(qi, wi, ke, ko)


def indexer_keys(ik, l_pad):
    B, L, _ = ik.shape
    ikT = jnp.pad(jnp.swapaxes(ik, 1, 2).astype(jnp.bfloat16), ((0, 0), (0, 0), (0, l_pad - L)))
    z = jnp.zeros_like(ikT)
    ke = jnp.concatenate([ikT, z], axis=1).reshape(B, LANES, l_pad // IDX_TK, IDX_TK).swapaxes(1, 2)
    ko = jnp.concatenate([z, ikT], axis=1).reshape(B, LANES, l_pad // IDX_TK, IDX_TK).swapaxes(1, 2)
    return ke, ko


DSA_TQ = 128
DSA_TK = 512
NEAR = 128


GROUP_ROWS = D_GROUP * DSA_TQ


DSA_VW = LANES


def _dsa_kernel(q_ref, kf_ref, vf_ref, mf_ref, k0_ref, v0_ref, m0_ref, k1_ref, v1_ref, m1_ref, bias_ref,
                o_ref, m_ref, acc_ref, *, n_far_max):
    i = pl.program_id(1)
    k = pl.program_id(2)
    q0 = i * DSA_TQ
    n_far = (i + 2) // 4

    @pl.when(k == 0)
    def _():
        m_ref[...] = jnp.full_like(m_ref, NEG_INF)
        acc_ref[...] = jnp.zeros_like(acc_ref)

    def update(g, s, v):
        m_prev = m_ref[g]
        m_new = jnp.maximum(m_prev, s.max(axis=1, keepdims=True))
        alpha = jnp.exp(m_prev - m_new)
        p = jnp.exp(s - m_new)
        acc_ref[g] = alpha * acc_ref[g] + jnp.dot(p.astype(jnp.bfloat16), v, preferred_element_type=jnp.float32)
        m_ref[g] = m_new

    def stack(mk):
        return jnp.concatenate([mk] * D_GROUP, axis=0)

    @pl.when(k < n_far)
    def _():
        col = k * DSA_TK + lax.broadcasted_iota(jnp.int32, (DSA_TQ, DSA_TK), 1)
        mk = jnp.concatenate([mf_ref[t] for t in range(DSA_TK // LANES)], axis=1).astype(jnp.float32)
        mk = stack(jnp.where(col < q0 - NEAR, mk, NEG_INF))

        def group(g, carry):
            s = jnp.dot(q_ref[g], kf_ref[g], preferred_element_type=jnp.float32) + mk
            update(g, s, vf_ref[g])
            return carry

        lax.fori_loop(0, D_KV_HEADS, group, 0, unroll=2)

    @pl.when(k == n_far_max)
    def _():
        mk0 = stack(m0_ref[...].astype(jnp.float32) + jnp.where(i > 0, 0.0, NEG_INF))
        mk1 = stack(m1_ref[...].astype(jnp.float32))

        def group(g, carry):
            q = q_ref[g]
            b = bias_ref[g]
            s0 = jnp.dot(q, k0_ref[g], preferred_element_type=jnp.float32) + (mk0 + b[:, :NEAR])
            update(g, s0, v0_ref[g])
            s1 = jnp.dot(q, k1_ref[g], preferred_element_type=jnp.float32) + (mk1 + b[:, NEAR:])
            update(g, s1, v1_ref[g])
            acc = acc_ref[g]
            o_ref[g] = (acc[:, :D_HEAD_DIM] / acc[:, D_HEAD_DIM:D_HEAD_DIM + 1]).astype(o_ref.dtype)
            return carry

        lax.fori_loop(0, D_KV_HEADS, group, 0, unroll=2)


def dsa_prompt(q, kT, v, mask, near_bias):
    B, _, nq, _, _ = q.shape
    S = nq * DSA_TQ
    n_far_max = S // DSA_TK

    def far_blk(i, k):
        return jnp.minimum(k, jnp.maximum((i + 2) // 4 - 1, 0))

    def prev_blk(i):
        return jnp.maximum(i - 1, 0)

    return pl.pallas_call(
        functools.partial(_dsa_kernel, n_far_max=n_far_max), grid=(B, nq, n_far_max + 1),
        in_specs=[pl.BlockSpec((None, D_KV_HEADS, None, GROUP_ROWS, D_HEAD_DIM), lambda b, i, k: (b, 0, i, 0, 0)),
                  pl.BlockSpec((None, D_KV_HEADS, D_HEAD_DIM, DSA_TK), lambda b, i, k: (b, 0, 0, far_blk(i, k))),
                  pl.BlockSpec((None, D_KV_HEADS, DSA_TK, DSA_VW), lambda b, i, k: (b, 0, far_blk(i, k), 0)),
                  pl.BlockSpec((None, DSA_TK // LANES, DSA_TQ, LANES), lambda b, i, k: (b, far_blk(i, k), i, 0)),
                  pl.BlockSpec((None, D_KV_HEADS, D_HEAD_DIM, NEAR), lambda b, i, k: (b, 0, 0, prev_blk(i))),
                  pl.BlockSpec((None, D_KV_HEADS, NEAR, DSA_VW), lambda b, i, k: (b, 0, prev_blk(i), 0)),
                  pl.BlockSpec((None, None, DSA_TQ, LANES), lambda b, i, k: (b, prev_blk(i), i, 0)),
                  pl.BlockSpec((None, D_KV_HEADS, D_HEAD_DIM, NEAR), lambda b, i, k: (b, 0, 0, i)),
                  pl.BlockSpec((None, D_KV_HEADS, NEAR, DSA_VW), lambda b, i, k: (b, 0, i, 0)),
                  pl.BlockSpec((None, None, DSA_TQ, LANES), lambda b, i, k: (b, i, i, 0)),
                  pl.BlockSpec((D_KV_HEADS, GROUP_ROWS, 2 * NEAR), lambda b, i, k: (0, 0, 0))],
        out_specs=pl.BlockSpec((None, D_KV_HEADS, None, GROUP_ROWS, D_HEAD_DIM), lambda b, i, k: (b, 0, i, 0, 0)),
        out_shape=jax.ShapeDtypeStruct((B, D_KV_HEADS, nq, GROUP_ROWS, D_HEAD_DIM), jnp.bfloat16),
        scratch_shapes=[pltpu.VMEM((D_KV_HEADS, GROUP_ROWS, 1), jnp.float32),
                        pltpu.VMEM((D_KV_HEADS, GROUP_ROWS, DSA_VW), jnp.float32)],
        compiler_params=_params("parallel", "parallel", "arbitrary"), name="dsa_attention",
    )(q, kT, v, mask, kT, v, mask, kT, v, mask, near_bias)


def with_ones_column(v, width):
    ones = jnp.ones(v.shape[:-1] + (1,), v.dtype)
    zeros = jnp.zeros(v.shape[:-1] + (width - v.shape[-1] - 1,), v.dtype)
    return jnp.concatenate([v, ones, zeros], axis=-1)


def dsa_prompt_mix(dq, dk, dv, iq, ik, iw, rel_bias, n_sel):
    B, S, _ = dq.shape
    bf = jnp.bfloat16
    ke, ko = indexer_keys(ik, S)
    w = iw * (IDX_HEADS ** -0.5) * (IDX_DIM ** -0.5)
    mask = indexer_mask(iq.astype(bf), w, ke, ko, n_sel, tq=IDX_TQ, pos0=0, n_keys=S)
    nq = S // DSA_TQ
    q = (dq * D_HEAD_DIM ** -0.5).astype(bf).reshape(B, nq, DSA_TQ, D_KV_HEADS, D_GROUP, D_HEAD_DIM)
    q = q.transpose(0, 3, 1, 4, 2, 5).reshape(B, D_KV_HEADS, nq, GROUP_ROWS, D_HEAD_DIM)
    kT = dk.astype(bf).reshape(B, S, D_KV_HEADS, D_HEAD_DIM).transpose(0, 2, 3, 1)
    v = with_ones_column(dv.astype(bf).reshape(B, S, D_KV_HEADS, D_HEAD_DIM), DSA_VW).transpose(0, 2, 1, 3)
    rel = (jnp.arange(2 * NEAR)[None, :] - NEAR) - jnp.arange(DSA_TQ)[:, None]
    far_bucket = NUM_BUCKETS // 2 - 1
    near_bias = (rel_bias[t5_bucket(rel)] - rel_bias[far_bucket]).transpose(2, 0, 1)
    o = dsa_prompt(q, kT, v, mask, near_bias.reshape(D_KV_HEADS, GROUP_ROWS, 2 * NEAR))
    o = o.reshape(B, D_KV_HEADS, nq, D_GROUP, DSA_TQ, D_HEAD_DIM).transpose(0, 2, 4, 1, 3, 5)
    return o.reshape(B, S, D_HEADS * D_HEAD_DIM)


SAMPLE_KEYS = PAST_LEN + DEC_SEQ
SAMPLE_KEYS_PAD = -(-SAMPLE_KEYS // IDX_TK) * IDX_TK
assert (SAMPLE_KEYS - 1) // CHUNK == PAST_LEN // CHUNK


def _dsa_sample_kernel(q_ref, kT_ref, v_ref, mask_ref, bias_ref, o_ref):
    mk = jnp.concatenate([mask_ref[t] for t in range(mask_ref.shape[0])], axis=1).astype(jnp.float32)
    mk = jnp.concatenate([mk] * D_GROUP, axis=0)

    def group(g, carry):
        s = jnp.dot(q_ref[g], kT_ref[g], preferred_element_type=jnp.float32) + (mk + bias_ref[g])
        p = jnp.exp(s - s.max(axis=1, keepdims=True))
        o = jnp.dot(p.astype(jnp.bfloat16), v_ref[g], preferred_element_type=jnp.float32)
        o_ref[g] = (o / p.sum(axis=1, keepdims=True)).astype(o_ref.dtype)
        return carry

    lax.fori_loop(0, D_KV_HEADS, group, 0, unroll=2)


def dsa_sample(q, kT, v, mask, bias):
    B, _, rows, _ = q.shape
    lp = kT.shape[-1]
    T = rows // D_GROUP
    return pl.pallas_call(
        _dsa_sample_kernel, grid=(B,),
        in_specs=[pl.BlockSpec((None, D_KV_HEADS, rows, D_HEAD_DIM), lambda b: (b, 0, 0, 0)),
                  pl.BlockSpec((None, D_KV_HEADS, D_HEAD_DIM, lp), lambda b: (b, 0, 0, 0)),
                  pl.BlockSpec((None, D_KV_HEADS, lp, D_HEAD_DIM), lambda b: (b, 0, 0, 0)),
                  pl.BlockSpec((None, lp // LANES, T, LANES), lambda b: (b, 0, 0, 0)),
                  pl.BlockSpec((D_KV_HEADS, rows, lp), lambda b: (0, 0, 0))],
        out_specs=pl.BlockSpec((None, D_KV_HEADS, rows, D_HEAD_DIM), lambda b: (b, 0, 0, 0)),
        out_shape=jax.ShapeDtypeStruct((B, D_KV_HEADS, rows, D_HEAD_DIM), jnp.bfloat16),
        compiler_params=_params("parallel"), name="dsa_sample",
    )(q, kT, v, mask, bias)


def _mla_sample_kernel(q_ref, k_ref, v_ref, o_ref, *, n_keys):
    for hh in range(MLA_HP):
        qs = slice(hh * MLA_QK, (hh + 1) * MLA_QK)
        vs = slice(hh * MLA_V, (hh + 1) * MLA_V)
        s = lax.dot_general(q_ref[:, qs], k_ref[:, qs], (((1,), (1,)), ((), ())),
                            preferred_element_type=jnp.float32)
        col = lax.broadcasted_iota(jnp.int32, s.shape, 1)
        s = jnp.where(col < n_keys, s, NEG_INF)
        p = jnp.exp(s - s.max(axis=1, keepdims=True))
        o = jnp.dot(p.astype(jnp.bfloat16), v_ref[:, vs], preferred_element_type=jnp.float32)
        o_ref[:, vs] = (o / p.sum(axis=1, keepdims=True)).astype(o_ref.dtype)


def mla_sample(qc, kc, v, n_keys):
    B, T, _ = qc.shape
    lp = kc.shape[1]
    qk_w, v_w = MLA_HP * MLA_QK, MLA_HP * MLA_V
    return pl.pallas_call(
        functools.partial(_mla_sample_kernel, n_keys=n_keys), grid=(B, C_HEADS // MLA_HP),
        in_specs=[pl.BlockSpec((None, T, qk_w), lambda b, h: (b, 0, h)),
                  pl.BlockSpec((None, lp, qk_w), lambda b, h: (b, 0, h)),
                  pl.BlockSpec((None, lp, v_w), lambda b, h: (b, 0, h))],
        out_specs=pl.BlockSpec((None, T, v_w), lambda b, h: (b, 0, h)),
        out_shape=jax.ShapeDtypeStruct((B, T, C_HEADS * MLA_V), jnp.bfloat16),
        compiler_params=_params("parallel", "parallel"), name="mla_sample",
    )(qc, kc, v)


def dsa_sample_mix(dq, dk_all, dv_all, iq, ik_all, iw, rel_bias, n_sel):
    B, T, _ = dq.shape
    L = dk_all.shape[1]
    lp = SAMPLE_KEYS_PAD
    bf = jnp.bfloat16
    ke, ko = indexer_keys(ik_all, lp)
    w = iw * (IDX_HEADS ** -0.5) * (IDX_DIM ** -0.5)
    mask = indexer_mask(iq.astype(bf), w, ke, ko, n_sel, tq=T, pos0=L - T, n_keys=L)
    q = (dq * D_HEAD_DIM ** -0.5).astype(bf).reshape(B, T, D_KV_HEADS, D_GROUP, D_HEAD_DIM)
    q = q.transpose(0, 2, 3, 1, 4).reshape(B, D_KV_HEADS, D_GROUP * T, D_HEAD_DIM)
    kp = jnp.pad(dk_all.astype(bf), ((0, 0), (0, lp - L), (0, 0))).reshape(B, lp, D_KV_HEADS, D_HEAD_DIM)
    vp = jnp.pad(dv_all.astype(bf), ((0, 0), (0, lp - L), (0, 0))).reshape(B, lp, D_KV_HEADS, D_HEAD_DIM)
    rel = jnp.arange(lp)[None, :] - (L - T + jnp.arange(T))[:, None]
    bias = rel_bias[t5_bucket(rel)].transpose(2, 0, 1).reshape(D_KV_HEADS, D_GROUP * T, lp)
    o = dsa_sample(q, kp.transpose(0, 2, 3, 1), vp.transpose(0, 2, 1, 3), mask, bias)
    o = o.reshape(B, D_KV_HEADS, D_GROUP, T, D_HEAD_DIM).transpose(0, 3, 1, 2, 4)
    return o.reshape(B, T, D_HEADS * D_HEAD_DIM)


def odd_mix(z, pos, g_qa, g_kva, w_qb, w_kvb, rel_bias, cache):
    B, T, _ = z.shape
    q_lat, kv_lat, kpe_raw, dq, dk, dv, iq, ik, iw = odd_split(z)
    qn = rmsnorm(q_lat, g_qa).reshape(B * T, Q_LORA)
    q = matmul(qn, w_qb, tm=min(ROW_BLOCK, B * T)).reshape(B, T, C_HEADS, NOPE_DIM + ROPE_DIM)
    q_nope = q[..., :NOPE_DIM]
    q_pe = apply_rope(q[..., NOPE_DIM:], pos)
    lat = rmsnorm(kv_lat, g_kva)
    kpe = apply_rope(kpe_raw, pos)
    bf = jnp.bfloat16
    new = (lat, kpe, dk.reshape(B, T, D_KV_HEADS, D_HEAD_DIM), dv.reshape(B, T, D_KV_HEADS, D_HEAD_DIM), ik)
    qpad = jnp.zeros((B, T, C_HEADS, MLA_QK - NOPE_DIM - ROPE_DIM), jnp.float32)
    qc = (jnp.concatenate([q_nope, q_pe, qpad], axis=-1) * (NOPE_DIM + ROPE_DIM) ** -0.5).astype(bf)
    qc = qc.reshape(B, T, C_HEADS * MLA_QK)
    if cache is None:
        lat_all, kpe_all, dk_all, dv_all, ik_all = lat, kpe, dk, dv, ik
    else:
        c_lat, c_kpe, c_dk, c_dv, c_ik = cache
        lat_all = jnp.concatenate([c_lat, lat], axis=1)
        kpe_all = jnp.concatenate([c_kpe, kpe], axis=1)
        dk_all = jnp.concatenate([c_dk.reshape(B, -1, D_KV_HEADS * D_HEAD_DIM), dk], axis=1)
        dv_all = jnp.concatenate([c_dv.reshape(B, -1, D_KV_HEADS * D_HEAD_DIM), dv], axis=1)
        ik_all = jnp.concatenate([c_ik, ik], axis=1)
    L = lat_all.shape[1]
    kv = matmul(lat_all.reshape(B * L, KV_LORA), w_kvb).reshape(B, L, C_HEADS, NOPE_DIM + V_DIM)
    k_nope, v_c = kv[..., :NOPE_DIM], kv[..., NOPE_DIM:]
    kpad = jnp.zeros((B, L, C_HEADS, MLA_QK - NOPE_DIM - ROPE_DIM), jnp.float32)
    kc = jnp.concatenate([k_nope, jnp.broadcast_to(kpe_all[:, :, None, :], (B, L, C_HEADS, ROPE_DIM)), kpad],
                         axis=-1).astype(bf).reshape(B, L, C_HEADS * MLA_QK)
    vc = v_c.astype(bf).reshape(B, L, C_HEADS * V_DIM)
    n_sel = min(TOPK_MAX, L // 4)
    if cache is None:
        oc = mla_prompt(qc, kc, with_ones_column(v_c.astype(bf), MLA_VW).reshape(B, L, C_HEADS * MLA_VW))
        od = dsa_prompt_mix(dq, dk, dv, iq, ik, iw, rel_bias, n_sel)
    else:
        rows = ((0, 0), (0, SAMPLE_KEYS_PAD - L), (0, 0))
        oc = mla_sample(qc, jnp.pad(kc, rows), jnp.pad(vc, rows), L)
        od = dsa_sample_mix(dq, dk_all, dv_all, iq, ik_all, iw, rel_bias, n_sel)
    return jnp.concatenate([oc, od], axis=-1), new


def _final_norm_kernel(x_ref, g_ref, o_ref):
    x = x_ref[...]
    o_ref[...] = x * lax.rsqrt(jnp.mean(x * x, axis=-1, keepdims=True) + RMS_EPS) * g_ref[...]


def final_norm(x, g):
    row = pl.BlockSpec((ROW_BLOCK, D_MODEL), lambda i: (i, 0))
    return pl.pallas_call(
        _final_norm_kernel, grid=(N_ROW_BLOCKS,),
        in_specs=[row, pl.BlockSpec((1, D_MODEL), lambda i: (0, 0))], out_specs=row,
        out_shape=jax.ShapeDtypeStruct((N_TOK, D_MODEL), jnp.float32),
        compiler_params=_params("parallel"), name="final_norm")(x, g.reshape(1, D_MODEL))


def kernel(x_prompt, x_sample, c_prompt, c_sample, cache_a_k, cache_a_v, state_b_pool, cache_c_latent, cache_c_kpe, cache_d_k, cache_d_v, cache_d_idx, rel_bias, ada_mix_w, ada_mix_b, ada_ffn_w, ada_ffn_b, norm_mix, norm_ffn, norm_final, even_w_in, even_w_out, a_sinks, pool_w, pool_scale, odd_w_in, odd_w_out, c_q_norm, c_kv_norm, c_w_qb, c_w_kvb, moe_router_group, moe_router_expert, moe_w_gate, moe_w_up, moe_w_down):
    bf = jnp.bfloat16
    x = jnp.concatenate([x_prompt.reshape(N_PROMPT, D_MODEL), x_sample.reshape(N_SAMPLE, D_MODEL)], axis=0)
    c = jnp.concatenate([c_prompt, c_sample], axis=0)
    mod_mix = modulation_all(c, ada_mix_w, ada_mix_b)
    mod_ffn = modulation_all(c, ada_ffn_w, ada_ffn_b)
    pos_p = jnp.arange(SEQ)
    pos_s = PAST_LEN + jnp.arange(DEC_SEQ)
    w_router = jnp.concatenate(
        [moe_router_group, moe_router_expert,
         jnp.zeros((DEPTH, D_MODEL, ROUTER_PAD - N_GROUPS - N_EXPERTS), jnp.float32)], axis=-1)
    new_p = [[] for _ in range(8)]
    new_s = [[] for _ in range(8)]
    for l in range(DEPTH):
        j = l // 2
        shift, scale, gate = jnp.split(mod_mix[l], 3, axis=-1)
        h = norm_mod(x, norm_mix[l], scale, shift)
        if l % 2 == 0:
            z = matmul(h, even_w_in[j].astype(bf))
            zp = z[:N_PROMPT].reshape(BATCH, SEQ, -1)
            zs = z[N_PROMPT:].reshape(DEC_BATCH, DEC_SEQ, -1)
            mp, st_p = even_mix(zp, pos_p, a_sinks[j], pool_w[j], pool_scale[j], rel_bias, None)
            ms, st_s = even_mix(zs, pos_s, a_sinks[j], pool_w[j], pool_scale[j], rel_bias,
                                (cache_a_k[j], cache_a_v[j], state_b_pool[j]))
            off = 0
            w_out = even_w_out[j]
        else:
            z = matmul(h, odd_in_weight(odd_w_in[j]), tn=ODD_IN_PAD // 9)
            zp = z[:N_PROMPT].reshape(BATCH, SEQ, -1)
            zs = z[N_PROMPT:].reshape(DEC_BATCH, DEC_SEQ, -1)
            wq, wkv = c_w_qb[j].astype(bf), c_w_kvb[j].astype(bf)
            mp, st_p = odd_mix(zp, pos_p, c_q_norm[j], c_kv_norm[j], wq, wkv, rel_bias, None)
            ms, st_s = odd_mix(zs, pos_s, c_q_norm[j], c_kv_norm[j], wq, wkv, rel_bias,
                               (cache_c_latent[j], cache_c_kpe[j], cache_d_k[j], cache_d_v[j], cache_d_idx[j]))
            off = 3
            w_out = odd_w_out[j]
        for i, s in enumerate(st_p):
            new_p[off + i].append(s)
        for i, s in enumerate(st_s):
            new_s[off + i].append(s)
        mix = jnp.concatenate([mp.reshape(N_PROMPT, -1), ms.reshape(N_SAMPLE, -1)], axis=0).astype(bf)
        x = matmul_resid(mix, w_out.astype(bf), x, gate)
        shift, scale, gate = jnp.split(mod_ffn[l], 3, axis=-1)
        h, logits = norm_mod(x, norm_ffn[l], scale, shift, w_router[l])
        x = moe_ffn(x, h, logits, gate, moe_w_gate[l].astype(bf), moe_w_up[l].astype(bf), moe_w_down[l].astype(bf))
    y = final_norm(x, norm_final)
    st_p = [jnp.stack(s) for s in new_p]
    st_s = [jnp.stack(s) for s in new_s]
    a_k_p, a_v_p, pool_p, lat_p, kpe_p, dk_p, dv_p, didx_p = st_p
    a_k_s, a_v_s, pool_s, lat_s, kpe_s, dk_s, dv_s, didx_s = st_s
    return (y[:N_PROMPT].reshape(BATCH, SEQ, D_MODEL), y[N_PROMPT:].reshape(DEC_BATCH, DEC_SEQ, D_MODEL),
            a_k_p, a_k_s, a_v_p, a_v_s, pool_p, pool_s, lat_p, lat_s,
            kpe_p, kpe_s, dk_p, dk_s, dv_p, dv_s, didx_p, didx_s)
```

```python
import functools
import math

import jax
import jax.numpy as jnp
from jax import lax
from jax.experimental import pallas as pl
from jax.experimental.pallas import tpu as pltpu

D_MODEL = 4096
BATCH = 2
SEQ = 8192
DEPTH = 4
DEC_BATCH = 16
DEC_SEQ = 32
PAST_LEN = 1024
CHUNK = 64
QBLOCK = 128
N_PAIR = DEPTH // 2
A_HEADS = 32
A_KV_HEADS = 8
A_HEAD_DIM = 64
A_GROUP = A_HEADS // A_KV_HEADS
WINDOW = 128
WIN_CHUNKS = WINDOW // CHUNK
POOL_WIDTH = 2048
POOL_GROUPS = 4
POOL_GROUP_WIDTH = POOL_WIDTH // POOL_GROUPS
POOL_WINDOWS = (2, 4, 8, 16)
POOL_HIST = max(POOL_WINDOWS) - 1
C_HEADS = 16
Q_LORA = 1024
KV_LORA = 512
NOPE_DIM = 128
ROPE_DIM = 64
V_DIM = 128
ROPE_THETA = 10000.0
D_HEADS = 32
D_KV_HEADS = 8
D_HEAD_DIM = 64
D_GROUP = D_HEADS // D_KV_HEADS
IDX_HEADS = 32
IDX_DIM = 64
TOPK_MAX = 256
NUM_BUCKETS = 32
MAX_DISTANCE = 128
N_GROUPS = 8
EXPERTS_PER_GROUP = 8
N_EXPERTS = N_GROUPS * EXPERTS_PER_GROUP
TOP_K = 2
D_EXPERT = 512
RMS_EPS = 1e-6
NEG_INF = -1e30

EVEN_SPLITS = (A_HEADS * A_HEAD_DIM, A_KV_HEADS * A_HEAD_DIM, A_KV_HEADS * A_HEAD_DIM, POOL_WIDTH)
ODD_SPLITS = (Q_LORA, KV_LORA, ROPE_DIM, D_HEADS * D_HEAD_DIM, D_KV_HEADS * D_HEAD_DIM,
              D_KV_HEADS * D_HEAD_DIM, IDX_HEADS * IDX_DIM, IDX_DIM, IDX_HEADS)
ODD_IN = sum(ODD_SPLITS)
ODD_IN_PAD = -(-ODD_IN // 128) * 128
ODD_ORDER = (3, 6, 0, 1, 4, 5, 2, 7, 8)


def odd_in_weight(w):
    segs = split_cols(w, ODD_SPLITS)
    out = jnp.concatenate([segs[s] for s in ODD_ORDER], axis=-1).astype(jnp.bfloat16)
    return jnp.pad(out, ((0, 0), (0, ODD_IN_PAD - ODD_IN)))


def odd_split(split):
    parts = split([ODD_SPLITS[s] for s in ODD_ORDER])
    out = [None] * len(ODD_ORDER)
    for pos, s in enumerate(ODD_ORDER):
        out[s] = parts[pos]
    return out

N_PROMPT = BATCH * SEQ
N_SAMPLE = DEC_BATCH * DEC_SEQ
N_TOK = N_PROMPT + N_SAMPLE
N_COND = BATCH + DEC_BATCH

ROW_BLOCK = 512
ROW_GROUPS = ROW_BLOCK // DEC_SEQ
N_ROW_BLOCKS = N_TOK // ROW_BLOCK
MOE_ROWS = 256
ROUTER_PAD = 128
VMEM_LIMIT = 56 * 1024 * 1024

assert N_SAMPLE == ROW_BLOCK and SEQ % ROW_BLOCK == 0 and N_TOK % ROW_BLOCK == 0


def _params(*sem):
    return pltpu.CompilerParams(dimension_semantics=sem, vmem_limit_bytes=VMEM_LIMIT)


def split_cols(z, sizes):
    out, o = [], 0
    for s in sizes:
        out.append(z[..., o:o + s])
        o += s
    return out


def _modulation_kernel(c_ref, w_ref, b_ref, o_ref):
    c = c_ref[...]
    a = (c * jax.nn.sigmoid(c)).astype(jnp.bfloat16)
    o_ref[0] = jnp.dot(a, w_ref[0].astype(jnp.bfloat16), preferred_element_type=jnp.float32) + b_ref[0]


def modulation_all(c, w, b):
    tn = 1024
    depth, d, n = w.shape
    return pl.pallas_call(
        _modulation_kernel,
        grid=(depth, n // tn),
        in_specs=[pl.BlockSpec((N_COND, d), lambda l, j: (0, 0)),
                  pl.BlockSpec((1, d, tn), lambda l, j: (l, 0, j)),
                  pl.BlockSpec((1, 1, tn), lambda l, j: (l, 0, j))],
        out_specs=pl.BlockSpec((1, N_COND, tn), lambda l, j: (l, 0, j)),
        out_shape=jax.ShapeDtypeStruct((depth, N_COND, n), jnp.float32),
        compiler_params=_params("parallel", "parallel"),
        name="modulation",
    )(c, w, b.reshape(depth, 1, n))


def expand_rows(m, block_rows=ROW_BLOCK):
    d = m.shape[-1]
    groups = block_rows // DEC_SEQ
    p = jnp.broadcast_to(m[:BATCH, None, None, :], (BATCH, SEQ // block_rows, groups, d))
    return jnp.concatenate([p.reshape(N_PROMPT // block_rows, groups, d),
                            m[BATCH:].reshape(N_SAMPLE // block_rows, groups, d)], axis=0)


def _norm_mod(x_ref, g_ref, scale_ref, shift_ref):
    x = x_ref[...]
    y = x * lax.rsqrt(jnp.mean(x * x, axis=-1, keepdims=True) + RMS_EPS) * g_ref[...]
    y = y.reshape(ROW_GROUPS, DEC_SEQ, D_MODEL)
    y = y * (1.0 + scale_ref[0][:, None, :]) + shift_ref[0][:, None, :]
    return y.reshape(ROW_BLOCK, D_MODEL)


def _norm_mod_kernel(x_ref, g_ref, scale_ref, shift_ref, h_ref):
    h_ref[...] = _norm_mod(x_ref, g_ref, scale_ref, shift_ref).astype(jnp.bfloat16)


def _norm_mod_router_kernel(x_ref, g_ref, scale_ref, shift_ref, whi_ref, wlo_ref, h_ref, lg_ref):
    h = _norm_mod(x_ref, g_ref, scale_ref, shift_ref)
    hi = h.astype(jnp.bfloat16)
    lo = (h - hi.astype(jnp.float32)).astype(jnp.bfloat16)
    h_ref[...] = h
    lg_ref[...] = (jnp.dot(hi, whi_ref[...], preferred_element_type=jnp.float32)
                   + jnp.dot(lo, whi_ref[...], preferred_element_type=jnp.float32)
                   + jnp.dot(hi, wlo_ref[...], preferred_element_type=jnp.float32))


def norm_mod(x, g, scale, shift, w_router=None):
    row = pl.BlockSpec((ROW_BLOCK, D_MODEL), lambda i: (i, 0))
    grp = pl.BlockSpec((1, ROW_GROUPS, D_MODEL), lambda i: (i, 0, 0))
    in_specs = [row, pl.BlockSpec((1, D_MODEL), lambda i: (0, 0)), grp, grp]
    h_shape = jax.ShapeDtypeStruct((N_TOK, D_MODEL), jnp.bfloat16)
    args = (x, g.reshape(1, D_MODEL), expand_rows(scale), expand_rows(shift))
    if w_router is None:
        return pl.pallas_call(
            _norm_mod_kernel, grid=(N_ROW_BLOCKS,), in_specs=in_specs, out_specs=row, out_shape=h_shape,
            compiler_params=_params("parallel"), name="norm_mod")(*args)
    w_hi = w_router.astype(jnp.bfloat16)
    w_lo = (w_router - w_hi.astype(jnp.float32)).astype(jnp.bfloat16)
    wspec = pl.BlockSpec((D_MODEL, ROUTER_PAD), lambda i: (0, 0))
    return pl.pallas_call(
        _norm_mod_router_kernel, grid=(N_ROW_BLOCKS,), in_specs=in_specs + [wspec, wspec],
        out_specs=(row, pl.BlockSpec((ROW_BLOCK, ROUTER_PAD), lambda i: (i, 0))),
        out_shape=(jax.ShapeDtypeStruct((N_TOK, D_MODEL), jnp.float32),
                   jax.ShapeDtypeStruct((N_TOK, ROUTER_PAD), jnp.float32)),
        compiler_params=_params("parallel"), name="norm_mod_router")(*args, w_hi, w_lo)


def _matmul_kernel(a_ref, b_ref, o_ref):
    o_ref[...] = jnp.dot(a_ref[...].astype(jnp.bfloat16), b_ref[...],
                         preferred_element_type=jnp.float32).astype(o_ref.dtype)


def matmul(a, b, *, tm=ROW_BLOCK, tn=512, out_dtype=jnp.float32):
    m, k = a.shape
    n = b.shape[1]
    tm, tn = min(tm, m), min(tn, n)
    assert m % tm == 0 and n % tn == 0, (a.shape, b.shape, tm, tn)
    return pl.pallas_call(
        _matmul_kernel, grid=(m // tm, n // tn),
        in_specs=[pl.BlockSpec((tm, k), lambda i, j: (i, 0)), pl.BlockSpec((k, tn), lambda i, j: (0, j))],
        out_specs=pl.BlockSpec((tm, tn), lambda i, j: (i, j)),
        out_shape=jax.ShapeDtypeStruct((m, n), out_dtype),
        compiler_params=_params("parallel", "parallel"), name="matmul",
    )(a, b)


def _matmul_resid_kernel(a_ref, b_ref, x_ref, gate_ref, o_ref):
    y = jnp.dot(a_ref[...], b_ref[...], preferred_element_type=jnp.float32)
    tn = y.shape[-1]
    y = y.reshape(ROW_GROUPS, DEC_SEQ, tn) * gate_ref[0][:, None, :]
    o_ref[...] = x_ref[...] + y.reshape(ROW_BLOCK, tn)


def matmul_resid(a, b, x, gate, *, tn=512):
    k = a.shape[1]
    return pl.pallas_call(
        _matmul_resid_kernel, grid=(N_ROW_BLOCKS, D_MODEL // tn),
        in_specs=[pl.BlockSpec((ROW_BLOCK, k), lambda i, j: (i, 0)),
                  pl.BlockSpec((k, tn), lambda i, j: (0, j)),
                  pl.BlockSpec((ROW_BLOCK, tn), lambda i, j: (i, j)),
                  pl.BlockSpec((1, ROW_GROUPS, tn), lambda i, j: (i, 0, j))],
        out_specs=pl.BlockSpec((ROW_BLOCK, tn), lambda i, j: (i, j)),
        out_shape=jax.ShapeDtypeStruct((N_TOK, D_MODEL), jnp.float32),
        compiler_params=_params("parallel", "parallel"), name="matmul_resid",
    )(a, b, x, expand_rows(gate))


GATHER_UNROLL = 8


def _start_row_gather(idx_ref, src_hbm, dst_buf, sem, n_rows):
    def body(r, carry):
        pltpu.make_async_copy(src_hbm.at[pl.ds(idx_ref[0, r], 1)], dst_buf.at[pl.ds(r, 1)], sem).start()
        return carry

    lax.fori_loop(0, n_rows, body, 0, unroll=GATHER_UNROLL)


def _wait_row_gather(src_hbm, dst_buf, sem, n_rows):
    pltpu.make_async_copy(src_hbm.at[pl.ds(0, n_rows)], dst_buf, sem).wait()


def _expert_kernel(block_e_ref, n_active_ref, tok_ref, tok_next_ref, h_hbm, wg_ref, wu_ref, wd_ref, gate_ref,
                   o_ref, xbuf, sem):
    i = pl.program_id(0)
    n_act = n_active_ref[0]
    slot = i % 2

    @pl.when(i == 0)
    def _():
        _start_row_gather(tok_ref, h_hbm, xbuf.at[0], sem.at[0], MOE_ROWS)

    @pl.when(i + 1 < n_act)
    def _():
        _start_row_gather(tok_next_ref, h_hbm, xbuf.at[1 - slot], sem.at[1 - slot], MOE_ROWS)

    @pl.when(i < n_act)
    def _():
        _wait_row_gather(h_hbm, xbuf.at[slot], sem.at[slot], MOE_ROWS)
        x = xbuf[slot].astype(jnp.bfloat16)
        g = jnp.dot(x, wg_ref[0], preferred_element_type=jnp.float32)
        u = jnp.dot(x, wu_ref[0], preferred_element_type=jnp.float32)
        a = (g * jax.nn.sigmoid(g) * u).astype(jnp.bfloat16)
        o_ref[...] = jnp.dot(a, wd_ref[0], preferred_element_type=jnp.float32) * gate_ref[...]

    @pl.when(i >= n_act)
    def _():
        o_ref[...] = jnp.zeros_like(o_ref)


def expert_blocks(h, tok_of_row, gate_of_row, block_e, n_active, w_gate, w_up, w_down):
    n_blocks = tok_of_row.shape[0]

    def smem_blk(index_map):
        return pl.BlockSpec((None, 1, MOE_ROWS), index_map, memory_space=pltpu.SMEM)

    return pl.pallas_call(
        _expert_kernel,
        grid_spec=pltpu.PrefetchScalarGridSpec(
            num_scalar_prefetch=2, grid=(n_blocks,),
            in_specs=[smem_blk(lambda i, be, na: (i, 0, 0)),
                      smem_blk(lambda i, be, na: (jnp.minimum(i + 1, n_blocks - 1), 0, 0)),
                      pl.BlockSpec(memory_space=pl.ANY),
                      pl.BlockSpec((1, D_MODEL, D_EXPERT), lambda i, be, na: (be[i], 0, 0)),
                      pl.BlockSpec((1, D_MODEL, D_EXPERT), lambda i, be, na: (be[i], 0, 0)),
                      pl.BlockSpec((1, D_EXPERT, D_MODEL), lambda i, be, na: (be[i], 0, 0)),
                      pl.BlockSpec((MOE_ROWS, 1), lambda i, be, na: (i, 0))],
            out_specs=pl.BlockSpec((MOE_ROWS, D_MODEL), lambda i, be, na: (i, 0)),
            scratch_shapes=[pltpu.VMEM((2, MOE_ROWS, D_MODEL), jnp.float32), pltpu.SemaphoreType.DMA((2,))]),
        out_shape=jax.ShapeDtypeStruct((n_blocks * MOE_ROWS, D_MODEL), jnp.float32),
        compiler_params=_params("arbitrary"), name="moe_experts",
    )(block_e, n_active, tok_of_row, tok_of_row, h, w_gate, w_up, w_down, gate_of_row)


COMBINE_ROWS = 256
COMBINE_GROUPS = COMBINE_ROWS // DEC_SEQ


def _combine_kernel(pos_ref, pos_next_ref, y_hbm, x_ref, gate_ref, o_ref, ybuf, sem):
    i = pl.program_id(0)
    n = pl.num_programs(0)
    slot = i % 2
    n_rows = TOP_K * COMBINE_ROWS

    @pl.when(i == 0)
    def _():
        _start_row_gather(pos_ref, y_hbm, ybuf.at[0], sem.at[0], n_rows)

    @pl.when(i + 1 < n)
    def _():
        _start_row_gather(pos_next_ref, y_hbm, ybuf.at[1 - slot], sem.at[1 - slot], n_rows)

    _wait_row_gather(y_hbm, ybuf.at[slot], sem.at[slot], n_rows)
    f = ybuf[slot, :COMBINE_ROWS] + ybuf[slot, COMBINE_ROWS:]
    f = f.reshape(COMBINE_GROUPS, DEC_SEQ, D_MODEL) * gate_ref[0][:, None, :]
    o_ref[...] = x_ref[...] + f.reshape(COMBINE_ROWS, D_MODEL)


def moe_combine(y, pos, x, gate):
    nb = N_TOK // COMBINE_ROWS

    def smem_blk(index_map):
        return pl.BlockSpec((None, 1, TOP_K * COMBINE_ROWS), index_map, memory_space=pltpu.SMEM)

    return pl.pallas_call(
        _combine_kernel, grid=(nb,),
        in_specs=[smem_blk(lambda i: (i, 0, 0)), smem_blk(lambda i: (jnp.minimum(i + 1, nb - 1), 0, 0)),
                  pl.BlockSpec(memory_space=pl.ANY),
                  pl.BlockSpec((COMBINE_ROWS, D_MODEL), lambda i: (i, 0)),
                  pl.BlockSpec((1, COMBINE_GROUPS, D_MODEL), lambda i: (i, 0, 0))],
        out_specs=pl.BlockSpec((COMBINE_ROWS, D_MODEL), lambda i: (i, 0)),
        out_shape=jax.ShapeDtypeStruct((N_TOK, D_MODEL), jnp.float32),
        scratch_shapes=[pltpu.VMEM((2, TOP_K * COMBINE_ROWS, D_MODEL), jnp.float32),
                        pltpu.SemaphoreType.DMA((2,))],
        compiler_params=_params("arbitrary"), name="moe_combine",
    )(pos, pos, y, x, expand_rows(gate, COMBINE_ROWS))


def moe_ffn(x, h, logits, gate, w_gate, w_up, w_down):
    n = h.shape[0]
    lg = logits[:, :N_GROUPS]
    grp = jnp.argmax(lg, axis=-1)
    p_grp = jnp.take_along_axis(jax.nn.softmax(lg, axis=-1), grp[:, None], axis=-1)
    le = logits[:, N_GROUPS:N_GROUPS + N_EXPERTS].reshape(n, N_GROUPS, EXPERTS_PER_GROUP)
    le = jnp.take_along_axis(le, grp[:, None, None], axis=1)[:, 0]
    pe, ie = lax.top_k(jax.nn.softmax(le, axis=-1), TOP_K)
    gates = p_grp * pe / pe.sum(-1, keepdims=True)
    expert = (grp[:, None] * EXPERTS_PER_GROUP + ie).astype(jnp.int32)
    m = n * TOP_K
    flat_e = expert.reshape(m)
    onehot = (flat_e[:, None] == jnp.arange(N_EXPERTS, dtype=jnp.int32)[None, :]).astype(jnp.int32)
    csum = jnp.cumsum(onehot, axis=0)
    rank = jnp.take_along_axis(csum, flat_e[:, None], axis=1)[:, 0] - 1
    sizes = csum[-1]
    padded = (sizes + MOE_ROWS - 1) // MOE_ROWS * MOE_ROWS
    pend = jnp.cumsum(padded)
    pstart = pend - padded
    dest = (pstart[flat_e] + rank).astype(jnp.int32)
    n_blocks = -(-m // MOE_ROWS) + N_EXPERTS
    total = n_blocks * MOE_ROWS
    tok_of_row = jnp.zeros((total,), jnp.int32).at[dest].set(jnp.arange(m, dtype=jnp.int32) // TOP_K)
    gate_of_row = jnp.zeros((total,), jnp.float32).at[dest].set(gates.reshape(m))
    block_start = jnp.arange(n_blocks, dtype=jnp.int32) * MOE_ROWS
    block_e = jnp.minimum(jnp.searchsorted(pend, block_start, side='right'), N_EXPERTS - 1).astype(jnp.int32)
    n_active = (pend[-1] // MOE_ROWS).astype(jnp.int32).reshape(1)
    y = expert_blocks(h, tok_of_row.reshape(n_blocks, 1, MOE_ROWS), gate_of_row.reshape(total, 1), block_e, n_active,
                      w_gate, w_up, w_down)
    nb = n // COMBINE_ROWS
    pos = dest.reshape(nb, COMBINE_ROWS, TOP_K).transpose(0, 2, 1).reshape(nb, 1, TOP_K * COMBINE_ROWS)
    return moe_combine(y, pos, x, gate)


def t5_bucket(rel):
    half = NUM_BUCKETS // 2
    max_exact = half // 2
    ret = jnp.where(rel > 0, half, 0)
    n = jnp.abs(rel)
    nf = jnp.maximum(n, 1).astype(jnp.float32)
    large = max_exact + (jnp.log(nf / max_exact) / math.log(MAX_DISTANCE / max_exact)
                         * (half - max_exact)).astype(jnp.int32)
    large = jnp.minimum(large, half - 1)
    return ret + jnp.where(n < max_exact, n, large)


def head_bias(rel, table, n_kv, group):
    b = table[t5_bucket(rel)]
    return jnp.moveaxis(b, -1, 0).reshape((n_kv, group) + rel.shape)


def apply_rope(x, pos):
    inv = jnp.exp(-math.log(ROPE_THETA) * jnp.arange(0, ROPE_DIM, 2, dtype=jnp.float32) / ROPE_DIM)
    ang = pos.astype(jnp.float32)[:, None] * inv[None, :]
    ang = ang.reshape(ang.shape[:1] + (1,) * (x.ndim - 3) + ang.shape[1:])
    cos, sin = jnp.cos(ang), jnp.sin(ang)
    x1, x2 = jnp.split(x.astype(jnp.float32), 2, axis=-1)
    return jnp.concatenate([x1 * cos - x2 * sin, x2 * cos + x1 * sin], axis=-1)


def rmsnorm(x, g):
    return x * lax.rsqrt(jnp.mean(x * x, axis=-1, keepdims=True) + RMS_EPS) * g


def sink_attention(q, k, v, bias, valid, sinks):
    n, tq = q.shape[:2]
    s = jnp.einsum('nqkgd,nskd->nkgqs', q, k) * A_HEAD_DIM ** -0.5 + bias
    s = jnp.where(valid[:, None, None], s, NEG_INF)
    sink = sinks.reshape(1, A_KV_HEADS, A_GROUP, 1, 1)
    mx = jnp.maximum(s.max(-1, keepdims=True), sink)
    e = jnp.exp(s - mx)
    p = e / (e.sum(-1, keepdims=True) + jnp.exp(sink - mx))
    o = jnp.einsum('nkgqs,nskd->nqkgd', p, v)
    return o.reshape(n, tq, A_HEADS * A_HEAD_DIM)


def window_attention_sample(q, k_all, v_all, pos, W, sinks, rel_bias):
    B, T = q.shape[:2]
    k_pos = jnp.concatenate([PAST_LEN - W + jnp.arange(W), pos])
    qc = (pos // CHUNK)[:, None]
    kc = (k_pos // CHUNK)[None, :]
    valid = (kc <= qc) & (kc >= qc - WIN_CHUNKS)
    valid = jnp.broadcast_to(valid[None], (B,) + valid.shape)
    bias = head_bias(k_pos[None, :] - pos[:, None], rel_bias, A_KV_HEADS, A_GROUP)
    return sink_attention(q, k_all, v_all, bias, valid, sinks)


def pool_mix(u_ext, pos, n_hist, pool_w, pool_scale):
    B, L, P = u_ext.shape
    T = L - n_hist
    cs = jnp.cumsum(u_ext, axis=1)
    cs = jnp.concatenate([jnp.zeros_like(cs[:, :1]), cs], axis=1)
    end = n_hist + jnp.arange(T) + 1
    outs = []
    for g, w in enumerate(POOL_WINDOWS):
        lo, hi = g * POOL_GROUP_WIDTH, (g + 1) * POOL_GROUP_WIDTH
        csg = cs[..., lo:hi]
        start = jnp.maximum(end - w, 0)
        cnt = jnp.minimum(w, pos + 1).astype(jnp.float32)[None, :, None]
        mean = (csg[:, end] - csg[:, start]) / cnt
        outs.append(mean - u_ext[:, n_hist:, lo:hi])
    d = jnp.stack(outs, axis=2)
    y = jnp.einsum('btgc,gcd->btgd', d, pool_w).reshape(B, T, P)
    return y * pool_scale


def token_rows(z, r0, b, t):
    def split(sizes):
        out, o = [], 0
        for s in sizes:
            out.append(z[r0:r0 + b * t, o:o + s].reshape(b, t, s))
            o += s
        return out
    return split


WIN_TQ = 2 * CHUNK
WIN_ROWS = A_GROUP * WIN_TQ
WIN_VW = 128
assert WIN_CHUNKS * CHUNK == WIN_TQ


def _window_kernel(q_ref, k0_ref, v0_ref, k1_ref, v1_ref, bias_ref, sink_ref, o_ref):
    i = pl.program_id(1)
    row_chunk = lax.broadcasted_iota(jnp.int32, (WIN_TQ, 2 * WIN_TQ), 0) // CHUNK
    key_chunk = lax.broadcasted_iota(jnp.int32, (WIN_TQ, 2 * WIN_TQ), 1) // CHUNK - WIN_CHUNKS
    ok = jnp.logical_and(key_chunk <= row_chunk, key_chunk >= row_chunk - WIN_CHUNKS)
    ok = jnp.logical_and(ok, jnp.logical_or(i > 0, key_chunk >= 0))
    mk = jnp.where(ok, 0.0, NEG_INF)
    mk = jnp.concatenate([mk] * A_GROUP, axis=0)

    def group(g, carry):
        q = q_ref[g]
        b = bias_ref[g] + mk
        s0 = jnp.dot(q, k0_ref[g], preferred_element_type=jnp.float32) + b[:, :WIN_TQ]
        s1 = jnp.dot(q, k1_ref[g], preferred_element_type=jnp.float32) + b[:, WIN_TQ:]
        sink = sink_ref[g]
        mx = jnp.maximum(jnp.maximum(s0.max(axis=1, keepdims=True), s1.max(axis=1, keepdims=True)), sink)
        e0 = jnp.exp(s0 - mx).astype(jnp.bfloat16)
        e1 = jnp.exp(s1 - mx).astype(jnp.bfloat16)
        acc = (jnp.dot(e0, v0_ref[g], preferred_element_type=jnp.float32)
               + jnp.dot(e1, v1_ref[g], preferred_element_type=jnp.float32))
        den = acc[:, A_HEAD_DIM:A_HEAD_DIM + 1] + jnp.exp(sink - mx)
        o_ref[g] = (acc[:, :A_HEAD_DIM] / den).astype(o_ref.dtype)
        return carry

    lax.fori_loop(0, A_KV_HEADS, group, 0, unroll=2)


def window_prompt(q, kT, v, bias, sink):
    B, _, nq, _, _ = q.shape

    def prev_blk(i):
        return jnp.maximum(i - 1, 0)

    return pl.pallas_call(
        _window_kernel, grid=(B, nq),
        in_specs=[pl.BlockSpec((None, A_KV_HEADS, None, WIN_ROWS, A_HEAD_DIM), lambda b, i: (b, 0, i, 0, 0)),
                  pl.BlockSpec((None, A_KV_HEADS, A_HEAD_DIM, WIN_TQ), lambda b, i: (b, 0, 0, prev_blk(i))),
                  pl.BlockSpec((None, A_KV_HEADS, WIN_TQ, WIN_VW), lambda b, i: (b, 0, prev_blk(i), 0)),
                  pl.BlockSpec((None, A_KV_HEADS, A_HEAD_DIM, WIN_TQ), lambda b, i: (b, 0, 0, i)),
                  pl.BlockSpec((None, A_KV_HEADS, WIN_TQ, WIN_VW), lambda b, i: (b, 0, i, 0)),
                  pl.BlockSpec((A_KV_HEADS, WIN_ROWS, 2 * WIN_TQ), lambda b, i: (0, 0, 0)),
                  pl.BlockSpec((A_KV_HEADS, WIN_ROWS, 1), lambda b, i: (0, 0, 0))],
        out_specs=pl.BlockSpec((None, A_KV_HEADS, None, WIN_ROWS, A_HEAD_DIM), lambda b, i: (b, 0, i, 0, 0)),
        out_shape=jax.ShapeDtypeStruct((B, A_KV_HEADS, nq, WIN_ROWS, A_HEAD_DIM), jnp.bfloat16),
        compiler_params=_params("parallel", "parallel"), name="window_attention",
    )(q, kT, v, kT, v, bias, sink)


def window_prompt_mix(q, k, v, sinks, rel_bias):
    B, S, _ = q.shape
    bf = jnp.bfloat16
    nq = S // WIN_TQ
    qs = (q * A_HEAD_DIM ** -0.5).astype(bf).reshape(B, nq, WIN_TQ, A_KV_HEADS, A_GROUP, A_HEAD_DIM)
    qs = qs.transpose(0, 3, 1, 4, 2, 5).reshape(B, A_KV_HEADS, nq, WIN_ROWS, A_HEAD_DIM)
    kT = k.astype(bf).reshape(B, S, A_KV_HEADS, A_HEAD_DIM).transpose(0, 2, 3, 1)
    vv = with_ones_column(v.astype(bf).reshape(B, S, A_KV_HEADS, A_HEAD_DIM), WIN_VW).transpose(0, 2, 1, 3)
    rel = (jnp.arange(2 * WIN_TQ)[None, :] - WIN_TQ) - jnp.arange(WIN_TQ)[:, None]
    bias = rel_bias[t5_bucket(rel)].transpose(2, 0, 1).reshape(A_KV_HEADS, WIN_ROWS, 2 * WIN_TQ)
    sink = jnp.broadcast_to(sinks.reshape(A_KV_HEADS, A_GROUP, 1, 1), (A_KV_HEADS, A_GROUP, WIN_TQ, 1))
    o = window_prompt(qs, kT, vv, bias, sink.reshape(A_KV_HEADS, WIN_ROWS, 1))
    o = o.reshape(B, A_KV_HEADS, nq, A_GROUP, WIN_TQ, A_HEAD_DIM).transpose(0, 2, 4, 1, 3, 5)
    return o.reshape(B, S, A_HEADS * A_HEAD_DIM)


POOL_TB = 256
POOL_HALO = 16
assert POOL_HALO >= POOL_HIST


def _pool_kernel(u_ref, prev_ref, w_ref, scale_ref, o_ref):
    i = pl.program_id(1)
    t0 = i * POOL_TB
    halo = jnp.where(i > 0, prev_ref[POOL_TB - POOL_HALO:, :], 0.0)
    pos = t0 + lax.broadcasted_iota(jnp.int32, (POOL_TB, 1), 0)
    outs = []
    for g, w in enumerate(POOL_WINDOWS):
        cols = slice(g * POOL_GROUP_WIDTH, (g + 1) * POOL_GROUP_WIDTH)
        u = u_ref[:, cols]
        x = jnp.concatenate([halo[:, cols], u], axis=0)
        span = 1
        while span < w:
            x = x + pltpu.roll(x, span, axis=0)
            span *= 2
        cnt = jnp.minimum(w, pos + 1).astype(jnp.float32)
        d = (x[POOL_HALO:] / cnt - u).astype(jnp.bfloat16)
        outs.append(jnp.dot(d, w_ref[g], preferred_element_type=jnp.float32))
    o_ref[...] = (jnp.concatenate(outs, axis=1) * scale_ref[...]).astype(o_ref.dtype)


def pool_prompt(u, pool_w, pool_scale):
    B, S, P = u.shape
    return pl.pallas_call(
        _pool_kernel, grid=(B, S // POOL_TB),
        in_specs=[pl.BlockSpec((None, POOL_TB, P), lambda b, i: (b, i, 0)),
                  pl.BlockSpec((None, POOL_TB, P), lambda b, i: (b, jnp.maximum(i - 1, 0), 0)),
                  pl.BlockSpec((POOL_GROUPS, POOL_GROUP_WIDTH, POOL_GROUP_WIDTH), lambda b, i: (0, 0, 0)),
                  pl.BlockSpec((1, P), lambda b, i: (0, 0))],
        out_specs=pl.BlockSpec((None, POOL_TB, P), lambda b, i: (b, i, 0)),
        out_shape=jax.ShapeDtypeStruct((B, S, P), jnp.bfloat16),
        compiler_params=_params("parallel", "parallel"), name="pool_mix",
    )(u, u, pool_w, pool_scale.reshape(1, P))


def even_mix(split, B, T, pos, sinks, pool_w, pool_scale, rel_bias, cache):
    q, k, v, u = split(EVEN_SPLITS)
    if cache is None:
        a = window_prompt_mix(q, k, v, sinks, rel_bias)
        p = pool_prompt(u, pool_w.astype(jnp.bfloat16), pool_scale)
        new = (k[:, -WINDOW:].reshape(B, WINDOW, A_KV_HEADS, A_HEAD_DIM),
               v[:, -WINDOW:].reshape(B, WINDOW, A_KV_HEADS, A_HEAD_DIM), u[:, -POOL_HIST:])
        return jnp.concatenate([a, p], axis=-1), new
    q = q.reshape(B, T, A_KV_HEADS, A_GROUP, A_HEAD_DIM)
    k = k.reshape(B, T, A_KV_HEADS, A_HEAD_DIM)
    v = v.reshape(B, T, A_KV_HEADS, A_HEAD_DIM)
    k_cache, v_cache, u_cache = cache
    W = k_cache.shape[1]
    k_all = jnp.concatenate([k_cache, k], axis=1)
    v_all = jnp.concatenate([v_cache, v], axis=1)
    a = window_attention_sample(q, k_all, v_all, pos, W, sinks, rel_bias)
    u_all = jnp.concatenate([u_cache, u], axis=1)
    p = pool_mix(u_all, pos, POOL_HIST, pool_w, pool_scale)
    new = (k_all[:, -W:], v_all[:, -W:], u_all[:, -POOL_HIST:])
    return jnp.concatenate([a, p], axis=-1).astype(jnp.bfloat16), new


MLA_T = 512
MLA_QK = 256
MLA_V = 128
MLA_HP = 2


MLA_VW = 2 * MLA_V


def _mla_kernel(q_ref, k_ref, v_ref, o_ref, m_ref, acc_ref):
    i = pl.program_id(2)
    k = pl.program_id(3)

    @pl.when(k == 0)
    def _():
        m_ref[...] = jnp.full_like(m_ref, NEG_INF)
        acc_ref[...] = jnp.zeros_like(acc_ref)

    def step(diagonal):
        for hh in range(MLA_HP):
            qs = slice(hh * MLA_QK, (hh + 1) * MLA_QK)
            vs = slice(hh * MLA_VW, (hh + 1) * MLA_VW)
            s = lax.dot_general(q_ref[:, qs], k_ref[:, qs], (((1,), (1,)), ((), ())),
                                preferred_element_type=jnp.float32)
            if diagonal:
                row = lax.broadcasted_iota(jnp.int32, s.shape, 0) // CHUNK
                col = lax.broadcasted_iota(jnp.int32, s.shape, 1) // CHUNK
                s = jnp.where(col <= row, s, NEG_INF)
            m_prev = m_ref[hh]
            m_new = jnp.maximum(m_prev, s.max(axis=1, keepdims=True))
            alpha = jnp.exp(m_prev - m_new)
            p = jnp.exp(s - m_new)
            acc_ref[:, vs] = alpha * acc_ref[:, vs] + jnp.dot(p.astype(jnp.bfloat16), v_ref[:, vs],
                                                              preferred_element_type=jnp.float32)
            m_ref[hh] = m_new

    @pl.when(k < i)
    def _():
        step(False)

    @pl.when(k == i)
    def _():
        step(True)
        for hh in range(MLA_HP):
            a0 = hh * MLA_VW
            o_ref[:, hh * MLA_V:(hh + 1) * MLA_V] = (
                acc_ref[:, a0:a0 + MLA_V] / acc_ref[:, a0 + MLA_V:a0 + MLA_V + 1]).astype(o_ref.dtype)


def mla_prompt(qc, kc, v):
    B, S, _ = qc.shape
    n = S // MLA_T
    qk_w, v_w, o_w = MLA_HP * MLA_QK, MLA_HP * MLA_VW, MLA_HP * MLA_V
    return pl.pallas_call(
        _mla_kernel, grid=(B, C_HEADS // MLA_HP, n, n),
        in_specs=[pl.BlockSpec((None, MLA_T, qk_w), lambda b, h, i, k: (b, i, h)),
                  pl.BlockSpec((None, MLA_T, qk_w), lambda b, h, i, k: (b, jnp.minimum(k, i), h)),
                  pl.BlockSpec((None, MLA_T, v_w), lambda b, h, i, k: (b, jnp.minimum(k, i), h))],
        out_specs=pl.BlockSpec((None, MLA_T, o_w), lambda b, h, i, k: (b, i, h)),
        out_shape=jax.ShapeDtypeStruct((B, S, C_HEADS * MLA_V), jnp.bfloat16),
        scratch_shapes=[pltpu.VMEM((MLA_HP, MLA_T, 1), jnp.float32), pltpu.VMEM((MLA_T, v_w), jnp.float32)],
        compiler_params=_params("parallel", "parallel", "parallel", "arbitrary"), name="mla_prompt",
    )(qc, kc, v)


IDX_TQ = 128
IDX_TK = 256
LANES = 128
INT_MIN = -2 ** 31


def _indexer_kernel(q_ref, w_ref, ke_ref, ko_ref, mask_ref, key_ref, *, n_sel, tq, pos0, n_keys):
    i = pl.program_id(1)
    q0 = pos0 + i * tq
    n_lane_tiles = mask_ref.shape[0]
    sub = IDX_TK // LANES
    n_kb = jnp.minimum((q0 + tq + IDX_TK - 1) // IDX_TK, n_lane_tiles // sub)
    row_chunk = (q0 + lax.broadcasted_iota(jnp.int32, (tq, LANES), 0)) // CHUNK
    lane_iota = lax.broadcasted_iota(jnp.int32, (tq, LANES), 1)
    padded_keys = n_keys < n_lane_tiles * LANES

    def admissible(j):
        col = j * LANES + lane_iota
        adm = col // CHUNK <= row_chunk
        if padded_keys:
            adm = jnp.logical_and(adm, col < n_keys)
        return adm

    def score_tile(kb, carry):
        ke = ke_ref[kb]
        ko = ko_ref[kb]
        acc = jnp.zeros((tq, IDX_TK), jnp.float32)
        for p in range(IDX_HEADS // 2):
            qp = q_ref[:, p * LANES:(p + 1) * LANES]
            se = jnp.dot(qp, ke, preferred_element_type=jnp.float32)
            so = jnp.dot(qp, ko, preferred_element_type=jnp.float32)
            acc = acc + jnp.maximum(se, 0.0) * w_ref[:, 2 * p:2 * p + 1]
            acc = acc + jnp.maximum(so, 0.0) * w_ref[:, 2 * p + 1:2 * p + 2]
        for t in range(sub):
            j = kb * sub + t
            a = jnp.where(admissible(j), acc[:, t * LANES:(t + 1) * LANES], -jnp.inf)
            bits = pltpu.bitcast(a, jnp.int32)
            key_ref[j] = bits ^ ((bits >> 31) & jnp.int32(0x7FFFFFFF))
        return carry

    lax.fori_loop(0, n_kb, score_tile, 0)
    n_tiles = n_kb * sub

    def count_ge(cand):
        def body(j, cnt):
            return cnt + jnp.where(key_ref[j] >= cand, 1, 0)
        cnt = lax.fori_loop(0, n_tiles, body, jnp.zeros((tq, LANES), jnp.int32))
        return cnt.sum(axis=1, keepdims=True)

    thr = jnp.where(count_ge(jnp.zeros((tq, 1), jnp.int32)) >= n_sel,
                    jnp.zeros((tq, 1), jnp.int32), jnp.full((tq, 1), INT_MIN, jnp.int32))

    def bit_step(b, thr):
        cand = thr | (jnp.int32(1) << (30 - b))
        return jnp.where(count_ge(cand) >= n_sel, cand, thr)

    thr = lax.fori_loop(0, 31, bit_step, thr)

    def write_mask(j, carry):
        sel = jnp.logical_and(key_ref[j] >= thr, admissible(j))
        mask_ref[j] = jnp.where(sel, 0.0, NEG_INF).astype(mask_ref.dtype)
        return carry

    lax.fori_loop(0, n_tiles, write_mask, 0)

    def write_rest(j, carry):
        mask_ref[j] = jnp.full((tq, LANES), NEG_INF, mask_ref.dtype)
        return carry

    lax.fori_loop(n_tiles, n_lane_tiles, write_rest, 0)


def indexer_mask(qi, wi, ke, ko, n_sel, *, tq, pos0, n_keys):
    B, T, _ = qi.shape
    nkb = ke.shape[1]
    L = nkb * IDX_TK
    return pl.pallas_call(
        functools.partial(_indexer_kernel, n_sel=n_sel, tq=tq, pos0=pos0, n_keys=n_keys), grid=(B, T // tq),
        in_specs=[pl.BlockSpec((None, tq, IDX_HEADS * IDX_DIM), lambda b, i: (b, i, 0)),
                  pl.BlockSpec((None, tq, IDX_HEADS), lambda b, i: (b, i, 0)),
                  pl.BlockSpec((None, nkb, LANES, IDX_TK), lambda b, i: (b, 0, 0, 0)),
                  pl.BlockSpec((None, nkb, LANES, IDX_TK), lambda b, i: (b, 0, 0, 0))],
        out_specs=pl.BlockSpec((None, L // LANES, tq, LANES), lambda b, i: (b, 0, i, 0)),
        out_shape=jax.ShapeDtypeStruct((B, L // LANES, T, LANES), jnp.bfloat16),
        scratch_shapes=[pltpu.VMEM((L // LANES, tq, LANES), jnp.int32)],
        compiler_params=_params("parallel", "parallel"), name="dsa_indexer",
    )(qi, wi, ke, ko)


def indexer_keys(ik, l_pad):
    B, L, _ = ik.shape
    ikT = jnp.pad(jnp.swapaxes(ik, 1, 2).astype(jnp.bfloat16), ((0, 0), (0, 0), (0, l_pad - L)))
    z = jnp.zeros_like(ikT)
    ke = jnp.concatenate([ikT, z], axis=1).reshape(B, LANES, l_pad // IDX_TK, IDX_TK).swapaxes(1, 2)
    ko = jnp.concatenate([z, ikT], axis=1).reshape(B, LANES, l_pad // IDX_TK, IDX_TK).swapaxes(1, 2)
    return ke, ko


DSA_TQ = 128
DSA_TK = 512
NEAR = 128


GROUP_ROWS = D_GROUP * DSA_TQ


DSA_VW = LANES


def _dsa_kernel(q_ref, kf_ref, vf_ref, mf_ref, k0_ref, v0_ref, m0_ref, k1_ref, v1_ref, m1_ref, bias_ref,
                o_ref, m_ref, acc_ref, *, n_far_max):
    i = pl.program_id(1)
    k = pl.program_id(2)
    q0 = i * DSA_TQ
    n_far = (i + 2) // 4

    @pl.when(k == 0)
    def _():
        m_ref[...] = jnp.full_like(m_ref, NEG_INF)
        acc_ref[...] = jnp.zeros_like(acc_ref)

    def update(g, s, v):
        m_prev = m_ref[g]
        m_new = jnp.maximum(m_prev, s.max(axis=1, keepdims=True))
        alpha = jnp.exp(m_prev - m_new)
        p = jnp.exp(s - m_new)
        acc_ref[g] = alpha * acc_ref[g] + jnp.dot(p.astype(jnp.bfloat16), v, preferred_element_type=jnp.float32)
        m_ref[g] = m_new

    def stack(mk):
        return jnp.concatenate([mk] * D_GROUP, axis=0)

    @pl.when(k < n_far)
    def _():
        col = k * DSA_TK + lax.broadcasted_iota(jnp.int32, (DSA_TQ, DSA_TK), 1)
        mk = jnp.concatenate([mf_ref[t] for t in range(DSA_TK // LANES)], axis=1).astype(jnp.float32)
        mk = stack(jnp.where(col < q0 - NEAR, mk, NEG_INF))

        def group(g, carry):
            s = jnp.dot(q_ref[g], kf_ref[g], preferred_element_type=jnp.float32) + mk
            update(g, s, vf_ref[g])
            return carry

        lax.fori_loop(0, D_KV_HEADS, group, 0, unroll=2)

    @pl.when(k == n_far_max)
    def _():
        mk0 = stack(m0_ref[...].astype(jnp.float32) + jnp.where(i > 0, 0.0, NEG_INF))
        mk1 = stack(m1_ref[...].astype(jnp.float32))

        def group(g, carry):
            q = q_ref[g]
            b = bias_ref[g]
            s0 = jnp.dot(q, k0_ref[g], preferred_element_type=jnp.float32) + (mk0 + b[:, :NEAR])
            update(g, s0, v0_ref[g])
            s1 = jnp.dot(q, k1_ref[g], preferred_element_type=jnp.float32) + (mk1 + b[:, NEAR:])
            update(g, s1, v1_ref[g])
            acc = acc_ref[g]
            o_ref[g] = (acc[:, :D_HEAD_DIM] / acc[:, D_HEAD_DIM:D_HEAD_DIM + 1]).astype(o_ref.dtype)
            return carry

        lax.fori_loop(0, D_KV_HEADS, group, 0, unroll=2)


def dsa_prompt(q, kT, v, mask, near_bias):
    B, _, nq, _, _ = q.shape
    S = nq * DSA_TQ
    n_far_max = S // DSA_TK

    def far_blk(i, k):
        return jnp.minimum(k, jnp.maximum((i + 2) // 4 - 1, 0))

    def prev_blk(i):
        return jnp.maximum(i - 1, 0)

    return pl.pallas_call(
        functools.partial(_dsa_kernel, n_far_max=n_far_max), grid=(B, nq, n_far_max + 1),
        in_specs=[pl.BlockSpec((None, D_KV_HEADS, None, GROUP_ROWS, D_HEAD_DIM), lambda b, i, k: (b, 0, i, 0, 0)),
                  pl.BlockSpec((None, D_KV_HEADS, D_HEAD_DIM, DSA_TK), lambda b, i, k: (b, 0, 0, far_blk(i, k))),
                  pl.BlockSpec((None, D_KV_HEADS, DSA_TK, DSA_VW), lambda b, i, k: (b, 0, far_blk(i, k), 0)),
                  pl.BlockSpec((None, DSA_TK // LANES, DSA_TQ, LANES), lambda b, i, k: (b, far_blk(i, k), i, 0)),
                  pl.BlockSpec((None, D_KV_HEADS, D_HEAD_DIM, NEAR), lambda b, i, k: (b, 0, 0, prev_blk(i))),
                  pl.BlockSpec((None, D_KV_HEADS, NEAR, DSA_VW), lambda b, i, k: (b, 0, prev_blk(i), 0)),
                  pl.BlockSpec((None, None, DSA_TQ, LANES), lambda b, i, k: (b, prev_blk(i), i, 0)),
                  pl.BlockSpec((None, D_KV_HEADS, D_HEAD_DIM, NEAR), lambda b, i, k: (b, 0, 0, i)),
                  pl.BlockSpec((None, D_KV_HEADS, NEAR, DSA_VW), lambda b, i, k: (b, 0, i, 0)),
                  pl.BlockSpec((None, None, DSA_TQ, LANES), lambda b, i, k: (b, i, i, 0)),
                  pl.BlockSpec((D_KV_HEADS, GROUP_ROWS, 2 * NEAR), lambda b, i, k: (0, 0, 0))],
        out_specs=pl.BlockSpec((None, D_KV_HEADS, None, GROUP_ROWS, D_HEAD_DIM), lambda b, i, k: (b, 0, i, 0, 0)),
        out_shape=jax.ShapeDtypeStruct((B, D_KV_HEADS, nq, GROUP_ROWS, D_HEAD_DIM), jnp.bfloat16),
        scratch_shapes=[pltpu.VMEM((D_KV_HEADS, GROUP_ROWS, 1), jnp.float32),
                        pltpu.VMEM((D_KV_HEADS, GROUP_ROWS, DSA_VW), jnp.float32)],
        compiler_params=_params("parallel", "parallel", "arbitrary"), name="dsa_attention",
    )(q, kT, v, mask, kT, v, mask, kT, v, mask, near_bias)


def with_ones_column(v, width):
    ones = jnp.ones(v.shape[:-1] + (1,), v.dtype)
    zeros = jnp.zeros(v.shape[:-1] + (width - v.shape[-1] - 1,), v.dtype)
    return jnp.concatenate([v, ones, zeros], axis=-1)


def dsa_prompt_mix(dq, dk, dv, iq, ik, iw, rel_bias, n_sel):
    B, S, _ = dq.shape
    bf = jnp.bfloat16
    ke, ko = indexer_keys(ik, S)
    w = iw * (IDX_HEADS ** -0.5) * (IDX_DIM ** -0.5)
    mask = indexer_mask(iq.astype(bf), w, ke, ko, n_sel, tq=IDX_TQ, pos0=0, n_keys=S)
    nq = S // DSA_TQ
    q = (dq * D_HEAD_DIM ** -0.5).astype(bf).reshape(B, nq, DSA_TQ, D_KV_HEADS, D_GROUP, D_HEAD_DIM)
    q = q.transpose(0, 3, 1, 4, 2, 5).reshape(B, D_KV_HEADS, nq, GROUP_ROWS, D_HEAD_DIM)
    kT = dk.astype(bf).reshape(B, S, D_KV_HEADS, D_HEAD_DIM).transpose(0, 2, 3, 1)
    v = with_ones_column(dv.astype(bf).reshape(B, S, D_KV_HEADS, D_HEAD_DIM), DSA_VW).transpose(0, 2, 1, 3)
    rel = (jnp.arange(2 * NEAR)[None, :] - NEAR) - jnp.arange(DSA_TQ)[:, None]
    far_bucket = NUM_BUCKETS // 2 - 1
    near_bias = (rel_bias[t5_bucket(rel)] - rel_bias[far_bucket]).transpose(2, 0, 1)
    o = dsa_prompt(q, kT, v, mask, near_bias.reshape(D_KV_HEADS, GROUP_ROWS, 2 * NEAR))
    o = o.reshape(B, D_KV_HEADS, nq, D_GROUP, DSA_TQ, D_HEAD_DIM).transpose(0, 2, 4, 1, 3, 5)
    return o.reshape(B, S, D_HEADS * D_HEAD_DIM)


SAMPLE_KEYS = PAST_LEN + DEC_SEQ
SAMPLE_KEYS_PAD = -(-SAMPLE_KEYS // IDX_TK) * IDX_TK
assert (SAMPLE_KEYS - 1) // CHUNK == PAST_LEN // CHUNK


def _dsa_sample_kernel(q_ref, kT_ref, v_ref, mask_ref, bias_ref, o_ref):
    mk = jnp.concatenate([mask_ref[t] for t in range(mask_ref.shape[0])], axis=1).astype(jnp.float32)
    mk = jnp.concatenate([mk] * D_GROUP, axis=0)

    def group(g, carry):
        s = jnp.dot(q_ref[g], kT_ref[g], preferred_element_type=jnp.float32) + (mk + bias_ref[g])
        p = jnp.exp(s - s.max(axis=1, keepdims=True))
        o = jnp.dot(p.astype(jnp.bfloat16), v_ref[g], preferred_element_type=jnp.float32)
        o_ref[g] = (o / p.sum(axis=1, keepdims=True)).astype(o_ref.dtype)
        return carry

    lax.fori_loop(0, D_KV_HEADS, group, 0, unroll=2)


def dsa_sample(q, kT, v, mask, bias):
    B, _, rows, _ = q.shape
    lp = kT.shape[-1]
    T = rows // D_GROUP
    return pl.pallas_call(
        _dsa_sample_kernel, grid=(B,),
        in_specs=[pl.BlockSpec((None, D_KV_HEADS, rows, D_HEAD_DIM), lambda b: (b, 0, 0, 0)),
                  pl.BlockSpec((None, D_KV_HEADS, D_HEAD_DIM, lp), lambda b: (b, 0, 0, 0)),
                  pl.BlockSpec((None, D_KV_HEADS, lp, D_HEAD_DIM), lambda b: (b, 0, 0, 0)),
                  pl.BlockSpec((None, lp // LANES, T, LANES), lambda b: (b, 0, 0, 0)),
                  pl.BlockSpec((D_KV_HEADS, rows, lp), lambda b: (0, 0, 0))],
        out_specs=pl.BlockSpec((None, D_KV_HEADS, rows, D_HEAD_DIM), lambda b: (b, 0, 0, 0)),
        out_shape=jax.ShapeDtypeStruct((B, D_KV_HEADS, rows, D_HEAD_DIM), jnp.bfloat16),
        compiler_params=_params("parallel"), name="dsa_sample",
    )(q, kT, v, mask, bias)


def _mla_sample_kernel(q_ref, k_ref, v_ref, o_ref, *, n_keys):
    for hh in range(MLA_HP):
        qs = slice(hh * MLA_QK, (hh + 1) * MLA_QK)
        vs = slice(hh * MLA_V, (hh + 1) * MLA_V)
        s = lax.dot_general(q_ref[:, qs], k_ref[:, qs], (((1,), (1,)), ((), ())),
                            preferred_element_type=jnp.float32)
        col = lax.broadcasted_iota(jnp.int32, s.shape, 1)
        s = jnp.where(col < n_keys, s, NEG_INF)
        p = jnp.exp(s - s.max(axis=1, keepdims=True))
        o = jnp.dot(p.astype(jnp.bfloat16), v_ref[:, vs], preferred_element_type=jnp.float32)
        o_ref[:, vs] = (o / p.sum(axis=1, keepdims=True)).astype(o_ref.dtype)


def mla_sample(qc, kc, v, n_keys):
    B, T, _ = qc.shape
    lp = kc.shape[1]
    qk_w, v_w = MLA_HP * MLA_QK, MLA_HP * MLA_V
    return pl.pallas_call(
        functools.partial(_mla_sample_kernel, n_keys=n_keys), grid=(B, C_HEADS // MLA_HP),
        in_specs=[pl.BlockSpec((None, T, qk_w), lambda b, h: (b, 0, h)),
                  pl.BlockSpec((None, lp, qk_w), lambda b, h: (b, 0, h)),
                  pl.BlockSpec((None, lp, v_w), lambda b, h: (b, 0, h))],
        out_specs=pl.BlockSpec((None, T, v_w), lambda b, h: (b, 0, h)),
        out_shape=jax.ShapeDtypeStruct((B, T, C_HEADS * MLA_V), jnp.bfloat16),
        compiler_params=_params("parallel", "parallel"), name="mla_sample",
    )(qc, kc, v)


def dsa_sample_mix(dq, dk_all, dv_all, iq, ik_all, iw, rel_bias, n_sel):
    B, T, _ = dq.shape
    L = dk_all.shape[1]
    lp = SAMPLE_KEYS_PAD
    bf = jnp.bfloat16
    ke, ko = indexer_keys(ik_all, lp)
    w = iw * (IDX_HEADS ** -0.5) * (IDX_DIM ** -0.5)
    mask = indexer_mask(iq.astype(bf), w, ke, ko, n_sel, tq=T, pos0=L - T, n_keys=L)
    q = (dq * D_HEAD_DIM ** -0.5).astype(bf).reshape(B, T, D_KV_HEADS, D_GROUP, D_HEAD_DIM)
    q = q.transpose(0, 2, 3, 1, 4).reshape(B, D_KV_HEADS, D_GROUP * T, D_HEAD_DIM)
    kp = jnp.pad(dk_all.astype(bf), ((0, 0), (0, lp - L), (0, 0))).reshape(B, lp, D_KV_HEADS, D_HEAD_DIM)
    vp = jnp.pad(dv_all.astype(bf), ((0, 0), (0, lp - L), (0, 0))).reshape(B, lp, D_KV_HEADS, D_HEAD_DIM)
    rel = jnp.arange(lp)[None, :] - (L - T + jnp.arange(T))[:, None]
    bias = rel_bias[t5_bucket(rel)].transpose(2, 0, 1).reshape(D_KV_HEADS, D_GROUP * T, lp)
    o = dsa_sample(q, kp.transpose(0, 2, 3, 1), vp.transpose(0, 2, 1, 3), mask, bias)
    o = o.reshape(B, D_KV_HEADS, D_GROUP, T, D_HEAD_DIM).transpose(0, 3, 1, 2, 4)
    return o.reshape(B, T, D_HEADS * D_HEAD_DIM)


def odd_mix(split, B, T, pos, g_qa, g_kva, w_qb, w_kvb, rel_bias, cache):
    q_lat, kv_lat, kpe_raw, dq, dk, dv, iq, ik, iw = odd_split(split)
    qn = rmsnorm(q_lat, g_qa).reshape(B * T, Q_LORA)
    q = matmul(qn, w_qb, tm=min(ROW_BLOCK, B * T)).reshape(B, T, C_HEADS, NOPE_DIM + ROPE_DIM)
    q_nope = q[..., :NOPE_DIM]
    q_pe = apply_rope(q[..., NOPE_DIM:], pos)
    lat = rmsnorm(kv_lat, g_kva)
    kpe = apply_rope(kpe_raw, pos)
    bf = jnp.bfloat16
    new = (lat, kpe, dk.reshape(B, T, D_KV_HEADS, D_HEAD_DIM), dv.reshape(B, T, D_KV_HEADS, D_HEAD_DIM), ik)
    qpad = jnp.zeros((B, T, C_HEADS, MLA_QK - NOPE_DIM - ROPE_DIM), jnp.float32)
    qc = (jnp.concatenate([q_nope, q_pe, qpad], axis=-1) * (NOPE_DIM + ROPE_DIM) ** -0.5).astype(bf)
    qc = qc.reshape(B, T, C_HEADS * MLA_QK)
    if cache is None:
        lat_all, kpe_all, dk_all, dv_all, ik_all = lat, kpe, dk, dv, ik
    else:
        c_lat, c_kpe, c_dk, c_dv, c_ik = cache
        lat_all = jnp.concatenate([c_lat, lat], axis=1)
        kpe_all = jnp.concatenate([c_kpe, kpe], axis=1)
        dk_all = jnp.concatenate([c_dk.reshape(B, -1, D_KV_HEADS * D_HEAD_DIM), dk], axis=1)
        dv_all = jnp.concatenate([c_dv.reshape(B, -1, D_KV_HEADS * D_HEAD_DIM), dv], axis=1)
        ik_all = jnp.concatenate([c_ik, ik], axis=1)
    L = lat_all.shape[1]
    kv = matmul(lat_all.reshape(B * L, KV_LORA), w_kvb).reshape(B, L, C_HEADS, NOPE_DIM + V_DIM)
    k_nope, v_c = kv[..., :NOPE_DIM], kv[..., NOPE_DIM:]
    kpad = jnp.zeros((B, L, C_HEADS, MLA_QK - NOPE_DIM - ROPE_DIM), jnp.float32)
    kc = jnp.concatenate([k_nope, jnp.broadcast_to(kpe_all[:, :, None, :], (B, L, C_HEADS, ROPE_DIM)), kpad],
                         axis=-1).astype(bf).reshape(B, L, C_HEADS * MLA_QK)
    vc = v_c.astype(bf).reshape(B, L, C_HEADS * V_DIM)
    n_sel = min(TOPK_MAX, L // 4)
    if cache is None:
        oc = mla_prompt(qc, kc, with_ones_column(v_c.astype(bf), MLA_VW).reshape(B, L, C_HEADS * MLA_VW))
        od = dsa_prompt_mix(dq, dk, dv, iq, ik, iw, rel_bias, n_sel)
    else:
        rows = ((0, 0), (0, SAMPLE_KEYS_PAD - L), (0, 0))
        oc = mla_sample(qc, jnp.pad(kc, rows), jnp.pad(vc, rows), L)
        od = dsa_sample_mix(dq, dk_all, dv_all, iq, ik_all, iw, rel_bias, n_sel)
    return jnp.concatenate([oc, od], axis=-1), new


def _final_norm_kernel(x_ref, g_ref, o_ref):
    x = x_ref[...]
    o_ref[...] = x * lax.rsqrt(jnp.mean(x * x, axis=-1, keepdims=True) + RMS_EPS) * g_ref[...]


def final_norm(x, g):
    row = pl.BlockSpec((ROW_BLOCK, D_MODEL), lambda i: (i, 0))
    return pl.pallas_call(
        _final_norm_kernel, grid=(N_ROW_BLOCKS,),
        in_specs=[row, pl.BlockSpec((1, D_MODEL), lambda i: (0, 0))], out_specs=row,
        out_shape=jax.ShapeDtypeStruct((N_TOK, D_MODEL), jnp.float32),
        compiler_params=_params("parallel"), name="final_norm")(x, g.reshape(1, D_MODEL))


def kernel(x_prompt, x_sample, c_prompt, c_sample, cache_a_k, cache_a_v, state_b_pool, cache_c_latent, cache_c_kpe, cache_d_k, cache_d_v, cache_d_idx, rel_bias, ada_mix_w, ada_mix_b, ada_ffn_w, ada_ffn_b, norm_mix, norm_ffn, norm_final, even_w_in, even_w_out, a_sinks, pool_w, pool_scale, odd_w_in, odd_w_out, c_q_norm, c_kv_norm, c_w_qb, c_w_kvb, moe_router_group, moe_router_expert, moe_w_gate, moe_w_up, moe_w_down):
    bf = jnp.bfloat16
    x = jnp.concatenate([x_prompt.reshape(N_PROMPT, D_MODEL), x_sample.reshape(N_SAMPLE, D_MODEL)], axis=0)
    c = jnp.concatenate([c_prompt, c_sample], axis=0)
    mod_mix = modulation_all(c, ada_mix_w, ada_mix_b)
    mod_ffn = modulation_all(c, ada_ffn_w, ada_ffn_b)
    pos_p = jnp.arange(SEQ)
    pos_s = PAST_LEN + jnp.arange(DEC_SEQ)
    w_router = jnp.concatenate(
        [moe_router_group, moe_router_expert,
         jnp.zeros((DEPTH, D_MODEL, ROUTER_PAD - N_GROUPS - N_EXPERTS), jnp.float32)], axis=-1)
    new_p = [[] for _ in range(8)]
    new_s = [[] for _ in range(8)]
    for l in range(DEPTH):
        j = l // 2
        shift, scale, gate = jnp.split(mod_mix[l], 3, axis=-1)
        h = norm_mod(x, norm_mix[l], scale, shift)
        if l % 2 == 0:
            z = matmul(h, even_w_in[j].astype(bf))
            zp = token_rows(z, 0, BATCH, SEQ)
            zs = token_rows(z, N_PROMPT, DEC_BATCH, DEC_SEQ)
            mp, st_p = even_mix(zp, BATCH, SEQ, pos_p, a_sinks[j], pool_w[j], pool_scale[j], rel_bias, None)
            ms, st_s = even_mix(zs, DEC_BATCH, DEC_SEQ, pos_s, a_sinks[j], pool_w[j], pool_scale[j], rel_bias,
                                (cache_a_k[j], cache_a_v[j], state_b_pool[j]))
            off = 0
            w_out = even_w_out[j]
        else:
            z = matmul(h, odd_in_weight(odd_w_in[j]), tn=ODD_IN_PAD // 9)
            zp = token_rows(z, 0, BATCH, SEQ)
            zs = token_rows(z, N_PROMPT, DEC_BATCH, DEC_SEQ)
            wq, wkv = c_w_qb[j].astype(bf), c_w_kvb[j].astype(bf)
            mp, st_p = odd_mix(zp, BATCH, SEQ, pos_p, c_q_norm[j], c_kv_norm[j], wq, wkv, rel_bias, None)
            ms, st_s = odd_mix(zs, DEC_BATCH, DEC_SEQ, pos_s, c_q_norm[j], c_kv_norm[j], wq, wkv, rel_bias,
                               (cache_c_latent[j], cache_c_kpe[j], cache_d_k[j], cache_d_v[j], cache_d_idx[j]))
            off = 3
            w_out = odd_w_out[j]
        for i, s in enumerate(st_p):
            new_p[off + i].append(s)
        for i, s in enumerate(st_s):
            new_s[off + i].append(s)
        mix = jnp.concatenate([mp.reshape(N_PROMPT, -1), ms.reshape(N_SAMPLE, -1)], axis=0).astype(bf)
        x = matmul_resid(mix, w_out.astype(bf), x, gate)
        shift, scale, gate = jnp.split(mod_ffn[l], 3, axis=-1)
        h, logits = norm_mod(x, norm_ffn[l], scale, shift, w_router[l])
        x = moe_ffn(x, h, logits, gate, moe_w_gate[l].astype(bf), moe_w_up[l].astype(bf), moe_w_down[l].astype(bf))
    y = final_norm(x, norm_final)
    st_p = [jnp.stack(s) for s in new_p]
    st_s = [jnp.stack(s) for s in new_s]
    a_k_p, a_v_p, pool_p, lat_p, kpe_p, dk_p, dv_p, didx_p = st_p
    a_k_s, a_v_s, pool_s, lat_s, kpe_s, dk_s, dv_s, didx_s = st_s
    return (y[:N_PROMPT].reshape(BATCH, SEQ, D_MODEL), y[N_PROMPT:].reshape(DEC_BATCH, DEC_SEQ, D_MODEL),
            a_k_p, a_k_s, a_v_p, a_v_s, pool_p, pool_s, lat_p, lat_s,
            kpe_p, kpe_s, dk_p, dk_s, dv_p, dv_s, didx_p, didx_s)
```

```python
import functools
import math

import jax
import jax.numpy as jnp
from jax import lax
from jax.experimental import pallas as pl
from jax.experimental.pallas import tpu as pltpu

D_MODEL = 4096
BATCH = 2
SEQ = 8192
DEPTH = 4
DEC_BATCH = 16
DEC_SEQ = 32
PAST_LEN = 1024
CHUNK = 64
QBLOCK = 128
N_PAIR = DEPTH // 2
A_HEADS = 32
A_KV_HEADS = 8
A_HEAD_DIM = 64
A_GROUP = A_HEADS // A_KV_HEADS
WINDOW = 128
WIN_CHUNKS = WINDOW // CHUNK
POOL_WIDTH = 2048
POOL_GROUPS = 4
POOL_GROUP_WIDTH = POOL_WIDTH // POOL_GROUPS
POOL_WINDOWS = (2, 4, 8, 16)
POOL_HIST = max(POOL_WINDOWS) - 1
C_HEADS = 16
Q_LORA = 1024
KV_LORA = 512
NOPE_DIM = 128
ROPE_DIM = 64
V_DIM = 128
ROPE_THETA = 10000.0
D_HEADS = 32
D_KV_HEADS = 8
D_HEAD_DIM = 64
D_GROUP = D_HEADS // D_KV_HEADS
IDX_HEADS = 32
IDX_DIM = 64
TOPK_MAX = 256
NUM_BUCKETS = 32
MAX_DISTANCE = 128
N_GROUPS = 8
EXPERTS_PER_GROUP = 8
N_EXPERTS = N_GROUPS * EXPERTS_PER_GROUP
TOP_K = 2
D_EXPERT = 512
RMS_EPS = 1e-6
NEG_INF = -1e30

EVEN_SPLITS = (A_HEADS * A_HEAD_DIM, A_KV_HEADS * A_HEAD_DIM, A_KV_HEADS * A_HEAD_DIM, POOL_WIDTH)
ODD_SPLITS = (Q_LORA, KV_LORA, ROPE_DIM, D_HEADS * D_HEAD_DIM, D_KV_HEADS * D_HEAD_DIM,
              D_KV_HEADS * D_HEAD_DIM, IDX_HEADS * IDX_DIM, IDX_DIM, IDX_HEADS)
ODD_IN = sum(ODD_SPLITS)
ODD_IN_PAD = -(-ODD_IN // 128) * 128
ODD_ORDER = (3, 6, 0, 1, 4, 5, 2, 7, 8)


def odd_in_weight(w):
    segs = split_cols(w, ODD_SPLITS)
    out = jnp.concatenate([segs[s] for s in ODD_ORDER], axis=-1).astype(jnp.bfloat16)
    return jnp.pad(out, ((0, 0), (0, ODD_IN_PAD - ODD_IN)))


def odd_split(split):
    parts = split([ODD_SPLITS[s] for s in ODD_ORDER])
    out = [None] * len(ODD_ORDER)
    for pos, s in enumerate(ODD_ORDER):
        out[s] = parts[pos]
    return out

N_PROMPT = BATCH * SEQ
N_SAMPLE = DEC_BATCH * DEC_SEQ
N_TOK = N_PROMPT + N_SAMPLE
N_COND = BATCH + DEC_BATCH

ROW_BLOCK = 512
ROW_GROUPS = ROW_BLOCK // DEC_SEQ
N_ROW_BLOCKS = N_TOK // ROW_BLOCK
MOE_ROWS = 256
ROUTER_PAD = 128
VMEM_LIMIT = 56 * 1024 * 1024

assert N_SAMPLE == ROW_BLOCK and SEQ % ROW_BLOCK == 0 and N_TOK % ROW_BLOCK == 0


def _params(*sem):
    return pltpu.CompilerParams(dimension_semantics=sem, vmem_limit_bytes=VMEM_LIMIT)


def split_cols(z, sizes):
    out, o = [], 0
    for s in sizes:
        out.append(z[..., o:o + s])
        o += s
    return out


def _modulation_kernel(c_ref, w_ref, b_ref, o_ref):
    c = c_ref[...]
    a = (c * jax.nn.sigmoid(c)).astype(jnp.bfloat16)
    o_ref[0] = jnp.dot(a, w_ref[0].astype(jnp.bfloat16), preferred_element_type=jnp.float32) + b_ref[0]


def modulation_all(c, w, b):
    tn = 1024
    depth, d, n = w.shape
    return pl.pallas_call(
        _modulation_kernel,
        grid=(depth, n // tn),
        in_specs=[pl.BlockSpec((N_COND, d), lambda l, j: (0, 0)),
                  pl.BlockSpec((1, d, tn), lambda l, j: (l, 0, j)),
                  pl.BlockSpec((1, 1, tn), lambda l, j: (l, 0, j))],
        out_specs=pl.BlockSpec((1, N_COND, tn), lambda l, j: (l, 0, j)),
        out_shape=jax.ShapeDtypeStruct((depth, N_COND, n), jnp.float32),
        compiler_params=_params("parallel", "parallel"),
        name="modulation",
    )(c, w, b.reshape(depth, 1, n))


def expand_rows(m, block_rows=ROW_BLOCK):
    d = m.shape[-1]
    groups = block_rows // DEC_SEQ
    p = jnp.broadcast_to(m[:BATCH, None, None, :], (BATCH, SEQ // block_rows, groups, d))
    return jnp.concatenate([p.reshape(N_PROMPT // block_rows, groups, d),
                            m[BATCH:].reshape(N_SAMPLE // block_rows, groups, d)], axis=0)


def _norm_mod(x_ref, g_ref, scale_ref, shift_ref):
    x = x_ref[...]
    y = x * lax.rsqrt(jnp.mean(x * x, axis=-1, keepdims=True) + RMS_EPS) * g_ref[...]
    y = y.reshape(ROW_GROUPS, DEC_SEQ, D_MODEL)
    y = y * (1.0 + scale_ref[0][:, None, :]) + shift_ref[0][:, None, :]
    return y.reshape(ROW_BLOCK, D_MODEL)


def _norm_mod_kernel(x_ref, g_ref, scale_ref, shift_ref, h_ref):
    h_ref[...] = _norm_mod(x_ref, g_ref, scale_ref, shift_ref).astype(jnp.bfloat16)


def _norm_mod_router_kernel(x_ref, g_ref, scale_ref, shift_ref, whi_ref, wlo_ref, h_ref, lg_ref):
    h = _norm_mod(x_ref, g_ref, scale_ref, shift_ref)
    hi = h.astype(jnp.bfloat16)
    lo = (h - hi.astype(jnp.float32)).astype(jnp.bfloat16)
    h_ref[...] = h
    lg_ref[...] = (jnp.dot(hi, whi_ref[...], preferred_element_type=jnp.float32)
                   + jnp.dot(lo, whi_ref[...], preferred_element_type=jnp.float32)
                   + jnp.dot(hi, wlo_ref[...], preferred_element_type=jnp.float32))


def norm_mod(x, g, scale, shift, w_router=None):
    row = pl.BlockSpec((ROW_BLOCK, D_MODEL), lambda i: (i, 0))
    grp = pl.BlockSpec((1, ROW_GROUPS, D_MODEL), lambda i: (i, 0, 0))
    in_specs = [row, pl.BlockSpec((1, D_MODEL), lambda i: (0, 0)), grp, grp]
    h_shape = jax.ShapeDtypeStruct((N_TOK, D_MODEL), jnp.bfloat16)
    args = (x, g.reshape(1, D_MODEL), expand_rows(scale), expand_rows(shift))
    if w_router is None:
        return pl.pallas_call(
            _norm_mod_kernel, grid=(N_ROW_BLOCKS,), in_specs=in_specs, out_specs=row, out_shape=h_shape,
            compiler_params=_params("parallel"), name="norm_mod")(*args)
    w_hi = w_router.astype(jnp.bfloat16)
    w_lo = (w_router - w_hi.astype(jnp.float32)).astype(jnp.bfloat16)
    wspec = pl.BlockSpec((D_MODEL, ROUTER_PAD), lambda i: (0, 0))
    return pl.pallas_call(
        _norm_mod_router_kernel, grid=(N_ROW_BLOCKS,), in_specs=in_specs + [wspec, wspec],
        out_specs=(row, pl.BlockSpec((ROW_BLOCK, ROUTER_PAD), lambda i: (i, 0))),
        out_shape=(jax.ShapeDtypeStruct((N_TOK, D_MODEL), jnp.float32),
                   jax.ShapeDtypeStruct((N_TOK, ROUTER_PAD), jnp.float32)),
        compiler_params=_params("parallel"), name="norm_mod_router")(*args, w_hi, w_lo)


def _matmul_kernel(a_ref, b_ref, o_ref):
    o_ref[...] = jnp.dot(a_ref[...].astype(jnp.bfloat16), b_ref[...],
                         preferred_element_type=jnp.float32).astype(o_ref.dtype)


def matmul(a, b, *, tm=ROW_BLOCK, tn=512, out_dtype=jnp.float32):
    m, k = a.shape
    n = b.shape[1]
    tm, tn = min(tm, m), min(tn, n)
    assert m % tm == 0 and n % tn == 0, (a.shape, b.shape, tm, tn)
    return pl.pallas_call(
        _matmul_kernel, grid=(m // tm, n // tn),
        in_specs=[pl.BlockSpec((tm, k), lambda i, j: (i, 0)), pl.BlockSpec((k, tn), lambda i, j: (0, j))],
        out_specs=pl.BlockSpec((tm, tn), lambda i, j: (i, j)),
        out_shape=jax.ShapeDtypeStruct((m, n), out_dtype),
        compiler_params=_params("parallel", "parallel"), name="matmul",
    )(a, b)


def _matmul_resid_kernel(a_ref, b_ref, x_ref, gate_ref, o_ref):
    y = jnp.dot(a_ref[...], b_ref[...], preferred_element_type=jnp.float32)
    tn = y.shape[-1]
    y = y.reshape(ROW_GROUPS, DEC_SEQ, tn) * gate_ref[0][:, None, :]
    o_ref[...] = x_ref[...] + y.reshape(ROW_BLOCK, tn)


def matmul_resid(a, b, x, gate, *, tn=512):
    k = a.shape[1]
    return pl.pallas_call(
        _matmul_resid_kernel, grid=(N_ROW_BLOCKS, D_MODEL // tn),
        in_specs=[pl.BlockSpec((ROW_BLOCK, k), lambda i, j: (i, 0)),
                  pl.BlockSpec((k, tn), lambda i, j: (0, j)),
                  pl.BlockSpec((ROW_BLOCK, tn), lambda i, j: (i, j)),
                  pl.BlockSpec((1, ROW_GROUPS, tn), lambda i, j: (i, 0, j))],
        out_specs=pl.BlockSpec((ROW_BLOCK, tn), lambda i, j: (i, j)),
        out_shape=jax.ShapeDtypeStruct((N_TOK, D_MODEL), jnp.float32),
        compiler_params=_params("parallel", "parallel"), name="matmul_resid",
    )(a, b, x, expand_rows(gate))


GATHER_UNROLL = 8


def _start_row_gather(idx_ref, src_hbm, dst_buf, sem, n_rows):
    def body(r, carry):
        pltpu.make_async_copy(src_hbm.at[pl.ds(idx_ref[0, r], 1)], dst_buf.at[pl.ds(r, 1)], sem).start()
        return carry

    lax.fori_loop(0, n_rows, body, 0, unroll=GATHER_UNROLL)


def _wait_row_gather(src_hbm, dst_buf, sem, n_rows):
    pltpu.make_async_copy(src_hbm.at[pl.ds(0, n_rows)], dst_buf, sem).wait()


def _expert_kernel(block_e_ref, n_active_ref, tok_ref, tok_next_ref, h_hbm, wg_ref, wu_ref, wd_ref, gate_ref,
                   o_ref, xbuf, sem):
    i = pl.program_id(0)
    n_act = n_active_ref[0]
    slot = i % 2

    @pl.when(i == 0)
    def _():
        _start_row_gather(tok_ref, h_hbm, xbuf.at[0], sem.at[0], MOE_ROWS)

    @pl.when(i + 1 < n_act)
    def _():
        _start_row_gather(tok_next_ref, h_hbm, xbuf.at[1 - slot], sem.at[1 - slot], MOE_ROWS)

    @pl.when(i < n_act)
    def _():
        _wait_row_gather(h_hbm, xbuf.at[slot], sem.at[slot], MOE_ROWS)
        x = xbuf[slot].astype(jnp.bfloat16)
        g = jnp.dot(x, wg_ref[0], preferred_element_type=jnp.float32)
        u = jnp.dot(x, wu_ref[0], preferred_element_type=jnp.float32)
        a = (g * jax.nn.sigmoid(g) * u).astype(jnp.bfloat16)
        o_ref[...] = jnp.dot(a, wd_ref[0], preferred_element_type=jnp.float32) * gate_ref[...]

    @pl.when(i >= n_act)
    def _():
        o_ref[...] = jnp.zeros_like(o_ref)


def expert_blocks(h, tok_of_row, gate_of_row, block_e, n_active, w_gate, w_up, w_down):
    n_blocks = tok_of_row.shape[0]

    def smem_blk(index_map):
        return pl.BlockSpec((None, 1, MOE_ROWS), index_map, memory_space=pltpu.SMEM)

    return pl.pallas_call(
        _expert_kernel,
        grid_spec=pltpu.PrefetchScalarGridSpec(
            num_scalar_prefetch=2, grid=(n_blocks,),
            in_specs=[smem_blk(lambda i, be, na: (i, 0, 0)),
                      smem_blk(lambda i, be, na: (jnp.minimum(i + 1, n_blocks - 1), 0, 0)),
                      pl.BlockSpec(memory_space=pl.ANY),
                      pl.BlockSpec((1, D_MODEL, D_EXPERT), lambda i, be, na: (be[i], 0, 0)),
                      pl.BlockSpec((1, D_MODEL, D_EXPERT), lambda i, be, na: (be[i], 0, 0)),
                      pl.BlockSpec((1, D_EXPERT, D_MODEL), lambda i, be, na: (be[i], 0, 0)),
                      pl.BlockSpec((MOE_ROWS, 1), lambda i, be, na: (i, 0))],
            out_specs=pl.BlockSpec((MOE_ROWS, D_MODEL), lambda i, be, na: (i, 0)),
            scratch_shapes=[pltpu.VMEM((2, MOE_ROWS, D_MODEL), jnp.float32), pltpu.SemaphoreType.DMA((2,))]),
        out_shape=jax.ShapeDtypeStruct((n_blocks * MOE_ROWS, D_MODEL), jnp.float32),
        compiler_params=_params("arbitrary"), name="moe_experts",
    )(block_e, n_active, tok_of_row, tok_of_row, h, w_gate, w_up, w_down, gate_of_row)


COMBINE_ROWS = 256
COMBINE_GROUPS = COMBINE_ROWS // DEC_SEQ


def _combine_kernel(pos_ref, pos_next_ref, y_hbm, x_ref, gate_ref, o_ref, ybuf, sem):
    i = pl.program_id(0)
    n = pl.num_programs(0)
    slot = i % 2
    n_rows = TOP_K * COMBINE_ROWS

    @pl.when(i == 0)
    def _():
        _start_row_gather(pos_ref, y_hbm, ybuf.at[0], sem.at[0], n_rows)

    @pl.when(i + 1 < n)
    def _():
        _start_row_gather(pos_next_ref, y_hbm, ybuf.at[1 - slot], sem.at[1 - slot], n_rows)

    _wait_row_gather(y_hbm, ybuf.at[slot], sem.at[slot], n_rows)
    f = ybuf[slot, :COMBINE_ROWS] + ybuf[slot, COMBINE_ROWS:]
    f = f.reshape(COMBINE_GROUPS, DEC_SEQ, D_MODEL) * gate_ref[0][:, None, :]
    o_ref[...] = x_ref[...] + f.reshape(COMBINE_ROWS, D_MODEL)


def moe_combine(y, pos, x, gate):
    nb = N_TOK // COMBINE_ROWS

    def smem_blk(index_map):
        return pl.BlockSpec((None, 1, TOP_K * COMBINE_ROWS), index_map, memory_space=pltpu.SMEM)

    return pl.pallas_call(
        _combine_kernel, grid=(nb,),
        in_specs=[smem_blk(lambda i: (i, 0, 0)), smem_blk(lambda i: (jnp.minimum(i + 1, nb - 1), 0, 0)),
                  pl.BlockSpec(memory_space=pl.ANY),
                  pl.BlockSpec((COMBINE_ROWS, D_MODEL), lambda i: (i, 0)),
                  pl.BlockSpec((1, COMBINE_GROUPS, D_MODEL), lambda i: (i, 0, 0))],
        out_specs=pl.BlockSpec((COMBINE_ROWS, D_MODEL), lambda i: (i, 0)),
        out_shape=jax.ShapeDtypeStruct((N_TOK, D_MODEL), jnp.float32),
        scratch_shapes=[pltpu.VMEM((2, TOP_K * COMBINE_ROWS, D_MODEL), jnp.float32),
                        pltpu.SemaphoreType.DMA((2,))],
        compiler_params=_params("arbitrary"), name="moe_combine",
    )(pos, pos, y, x, expand_rows(gate, COMBINE_ROWS))


def moe_ffn(x, h, logits, gate, w_gate, w_up, w_down):
    n = h.shape[0]
    lg = logits[:, :N_GROUPS]
    grp = jnp.argmax(lg, axis=-1)
    p_grp = jnp.take_along_axis(jax.nn.softmax(lg, axis=-1), grp[:, None], axis=-1)
    le = logits[:, N_GROUPS:N_GROUPS + N_EXPERTS].reshape(n, N_GROUPS, EXPERTS_PER_GROUP)
    le = jnp.take_along_axis(le, grp[:, None, None], axis=1)[:, 0]
    pe, ie = lax.top_k(jax.nn.softmax(le, axis=-1), TOP_K)
    gates = p_grp * pe / pe.sum(-1, keepdims=True)
    expert = (grp[:, None] * EXPERTS_PER_GROUP + ie).astype(jnp.int32)
    m = n * TOP_K
    flat_e = expert.reshape(m)
    onehot = (flat_e[:, None] == jnp.arange(N_EXPERTS, dtype=jnp.int32)[None, :]).astype(jnp.int32)
    csum = jnp.cumsum(onehot, axis=0)
    rank = jnp.take_along_axis(csum, flat_e[:, None], axis=1)[:, 0] - 1
    sizes = csum[-1]
    padded = (sizes + MOE_ROWS - 1) // MOE_ROWS * MOE_ROWS
    pend = jnp.cumsum(padded)
    pstart = pend - padded
    dest = (pstart[flat_e] + rank).astype(jnp.int32)
    n_blocks = -(-m // MOE_ROWS) + N_EXPERTS
    total = n_blocks * MOE_ROWS
    tok_of_row = jnp.zeros((total,), jnp.int32).at[dest].set(jnp.arange(m, dtype=jnp.int32) // TOP_K)
    gate_of_row = jnp.zeros((total,), jnp.float32).at[dest].set(gates.reshape(m))
    block_start = jnp.arange(n_blocks, dtype=jnp.int32) * MOE_ROWS
    block_e = jnp.minimum(jnp.searchsorted(pend, block_start, side='right'), N_EXPERTS - 1).astype(jnp.int32)
    n_active = (pend[-1] // MOE_ROWS).astype(jnp.int32).reshape(1)
    y = expert_blocks(h, tok_of_row.reshape(n_blocks, 1, MOE_ROWS), gate_of_row.reshape(total, 1), block_e, n_active,
                      w_gate, w_up, w_down)
    nb = n // COMBINE_ROWS
    pos = dest.reshape(nb, COMBINE_ROWS, TOP_K).transpose(0, 2, 1).reshape(nb, 1, TOP_K * COMBINE_ROWS)
    return moe_combine(y, pos, x, gate)


def t5_bucket(rel):
    half = NUM_BUCKETS // 2
    max_exact = half // 2
    ret = jnp.where(rel > 0, half, 0)
    n = jnp.abs(rel)
    nf = jnp.maximum(n, 1).astype(jnp.float32)
    large = max_exact + (jnp.log(nf / max_exact) / math.log(MAX_DISTANCE / max_exact)
                         * (half - max_exact)).astype(jnp.int32)
    large = jnp.minimum(large, half - 1)
    return ret + jnp.where(n < max_exact, n, large)


def head_bias(rel, table, n_kv, group):
    b = table[t5_bucket(rel)]
    return jnp.moveaxis(b, -1, 0).reshape((n_kv, group) + rel.shape)


def apply_rope(x, pos):
    inv = jnp.exp(-math.log(ROPE_THETA) * jnp.arange(0, ROPE_DIM, 2, dtype=jnp.float32) / ROPE_DIM)
    ang = pos.astype(jnp.float32)[:, None] * inv[None, :]
    ang = ang.reshape(ang.shape[:1] + (1,) * (x.ndim - 3) + ang.shape[1:])
    cos, sin = jnp.cos(ang), jnp.sin(ang)
    x1, x2 = jnp.split(x.astype(jnp.float32), 2, axis=-1)
    return jnp.concatenate([x1 * cos - x2 * sin, x2 * cos + x1 * sin], axis=-1)


def rmsnorm(x, g):
    return x * lax.rsqrt(jnp.mean(x * x, axis=-1, keepdims=True) + RMS_EPS) * g


def sink_attention(q, k, v, bias, valid, sinks):
    n, tq = q.shape[:2]
    s = jnp.einsum('nqkgd,nskd->nkgqs', q, k) * A_HEAD_DIM ** -0.5 + bias
    s = jnp.where(valid[:, None, None], s, NEG_INF)
    sink = sinks.reshape(1, A_KV_HEADS, A_GROUP, 1, 1)
    mx = jnp.maximum(s.max(-1, keepdims=True), sink)
    e = jnp.exp(s - mx)
    p = e / (e.sum(-1, keepdims=True) + jnp.exp(sink - mx))
    o = jnp.einsum('nkgqs,nskd->nqkgd', p, v)
    return o.reshape(n, tq, A_HEADS * A_HEAD_DIM)


def window_attention_sample(q, k_all, v_all, pos, W, sinks, rel_bias):
    B, T = q.shape[:2]
    k_pos = jnp.concatenate([PAST_LEN - W + jnp.arange(W), pos])
    qc = (pos // CHUNK)[:, None]
    kc = (k_pos // CHUNK)[None, :]
    valid = (kc <= qc) & (kc >= qc - WIN_CHUNKS)
    valid = jnp.broadcast_to(valid[None], (B,) + valid.shape)
    bias = head_bias(k_pos[None, :] - pos[:, None], rel_bias, A_KV_HEADS, A_GROUP)
    return sink_attention(q, k_all, v_all, bias, valid, sinks)


def pool_mix(u_ext, pos, n_hist, pool_w, pool_scale):
    B, L, P = u_ext.shape
    T = L - n_hist
    cs = jnp.cumsum(u_ext, axis=1)
    cs = jnp.concatenate([jnp.zeros_like(cs[:, :1]), cs], axis=1)
    end = n_hist + jnp.arange(T) + 1
    outs = []
    for g, w in enumerate(POOL_WINDOWS):
        lo, hi = g * POOL_GROUP_WIDTH, (g + 1) * POOL_GROUP_WIDTH
        csg = cs[..., lo:hi]
        start = jnp.maximum(end - w, 0)
        cnt = jnp.minimum(w, pos + 1).astype(jnp.float32)[None, :, None]
        mean = (csg[:, end] - csg[:, start]) / cnt
        outs.append(mean - u_ext[:, n_hist:, lo:hi])
    d = jnp.stack(outs, axis=2)
    y = jnp.einsum('btgc,gcd->btgd', d, pool_w).reshape(B, T, P)
    return y * pool_scale


def token_rows(z, r0, b, t):
    def split(sizes):
        out, o = [], 0
        for s in sizes:
            out.append(z[r0:r0 + b * t, o:o + s].reshape(b, t, s))
            o += s
        return out
    return split


WIN_TQ = 2 * CHUNK
WIN_ROWS = A_GROUP * WIN_TQ
WIN_VW = 128
assert WIN_CHUNKS * CHUNK == WIN_TQ


def _window_kernel(q_ref, k0_ref, v0_ref, k1_ref, v1_ref, bias_ref, sink_ref, o_ref):
    i = pl.program_id(1)
    row_chunk = lax.broadcasted_iota(jnp.int32, (WIN_TQ, 2 * WIN_TQ), 0) // CHUNK
    key_chunk = lax.broadcasted_iota(jnp.int32, (WIN_TQ, 2 * WIN_TQ), 1) // CHUNK - WIN_CHUNKS
    ok = jnp.logical_and(key_chunk <= row_chunk, key_chunk >= row_chunk - WIN_CHUNKS)
    ok = jnp.logical_and(ok, jnp.logical_or(i > 0, key_chunk >= 0))
    mk = jnp.where(ok, 0.0, NEG_INF)
    mk = jnp.concatenate([mk] * A_GROUP, axis=0)

    def group(g, carry):
        q = q_ref[g]
        b = bias_ref[g] + mk
        s0 = jnp.dot(q, k0_ref[g], preferred_element_type=jnp.float32) + b[:, :WIN_TQ]
        s1 = jnp.dot(q, k1_ref[g], preferred_element_type=jnp.float32) + b[:, WIN_TQ:]
        sink = sink_ref[g]
        mx = jnp.maximum(jnp.maximum(s0.max(axis=1, keepdims=True), s1.max(axis=1, keepdims=True)), sink)
        e0 = jnp.exp(s0 - mx).astype(jnp.bfloat16)
        e1 = jnp.exp(s1 - mx).astype(jnp.bfloat16)
        acc = (jnp.dot(e0, v0_ref[g], preferred_element_type=jnp.float32)
               + jnp.dot(e1, v1_ref[g], preferred_element_type=jnp.float32))
        den = acc[:, A_HEAD_DIM:A_HEAD_DIM + 1] + jnp.exp(sink - mx)
        o_ref[g] = (acc[:, :A_HEAD_DIM] / den).astype(o_ref.dtype)
        return carry

    lax.fori_loop(0, A_KV_HEADS, group, 0, unroll=2)


def window_prompt(q, kT, v, bias, sink):
    B, _, nq, _, _ = q.shape

    def prev_blk(i):
        return jnp.maximum(i - 1, 0)

    return pl.pallas_call(
        _window_kernel, grid=(B, nq),
        in_specs=[pl.BlockSpec((None, A_KV_HEADS, None, WIN_ROWS, A_HEAD_DIM), lambda b, i: (b, 0, i, 0, 0)),
                  pl.BlockSpec((None, A_KV_HEADS, A_HEAD_DIM, WIN_TQ), lambda b, i: (b, 0, 0, prev_blk(i))),
                  pl.BlockSpec((None, A_KV_HEADS, WIN_TQ, WIN_VW), lambda b, i: (b, 0, prev_blk(i), 0)),
                  pl.BlockSpec((None, A_KV_HEADS, A_HEAD_DIM, WIN_TQ), lambda b, i: (b, 0, 0, i)),
                  pl.BlockSpec((None, A_KV_HEADS, WIN_TQ, WIN_VW), lambda b, i: (b, 0, i, 0)),
                  pl.BlockSpec((A_KV_HEADS, WIN_ROWS, 2 * WIN_TQ), lambda b, i: (0, 0, 0)),
                  pl.BlockSpec((A_KV_HEADS, WIN_ROWS, 1), lambda b, i: (0, 0, 0))],
        out_specs=pl.BlockSpec((None, A_KV_HEADS, None, WIN_ROWS, A_HEAD_DIM), lambda b, i: (b, 0, i, 0, 0)),
        out_shape=jax.ShapeDtypeStruct((B, A_KV_HEADS, nq, WIN_ROWS, A_HEAD_DIM), jnp.bfloat16),
        compiler_params=_params("parallel", "parallel"), name="window_attention",
    )(q, kT, v, kT, v, bias, sink)


def window_prompt_mix(q, k, v, sinks, rel_bias):
    B, S, _ = q.shape
    bf = jnp.bfloat16
    nq = S // WIN_TQ
    qs = (q * A_HEAD_DIM ** -0.5).astype(bf).reshape(B, nq, WIN_TQ, A_KV_HEADS, A_GROUP, A_HEAD_DIM)
    qs = qs.transpose(0, 3, 1, 4, 2, 5).reshape(B, A_KV_HEADS, nq, WIN_ROWS, A_HEAD_DIM)
    kT = k.astype(bf).reshape(B, S, A_KV_HEADS, A_HEAD_DIM).transpose(0, 2, 3, 1)
    vv = with_ones_column(v.astype(bf).reshape(B, S, A_KV_HEADS, A_HEAD_DIM), WIN_VW).transpose(0, 2, 1, 3)
    rel = (jnp.arange(2 * WIN_TQ)[None, :] - WIN_TQ) - jnp.arange(WIN_TQ)[:, None]
    bias = rel_bias[t5_bucket(rel)].transpose(2, 0, 1).reshape(A_KV_HEADS, WIN_ROWS, 2 * WIN_TQ)
    sink = jnp.broadcast_to(sinks.reshape(A_KV_HEADS, A_GROUP, 1, 1), (A_KV_HEADS, A_GROUP, WIN_TQ, 1))
    o = window_prompt(qs, kT, vv, bias, sink.reshape(A_KV_HEADS, WIN_ROWS, 1))
    o = o.reshape(B, A_KV_HEADS, nq, A_GROUP, WIN_TQ, A_HEAD_DIM).transpose(0, 2, 4, 1, 3, 5)
    return o.reshape(B, S, A_HEADS * A_HEAD_DIM)


POOL_TB = 256
POOL_HALO = 16
assert POOL_HALO >= POOL_HIST


def _pool_kernel(u_ref, prev_ref, w_ref, scale_ref, o_ref):
    i = pl.program_id(1)
    t0 = i * POOL_TB
    halo = jnp.where(i > 0, prev_ref[POOL_TB - POOL_HALO:, :], 0.0)
    pos = t0 + lax.broadcasted_iota(jnp.int32, (POOL_TB, 1), 0)
    outs = []
    for g, w in enumerate(POOL_WINDOWS):
        cols = slice(g * POOL_GROUP_WIDTH, (g + 1) * POOL_GROUP_WIDTH)
        u = u_ref[:, cols]
        x = jnp.concatenate([halo[:, cols], u], axis=0)
        span = 1
        while span < w:
            x = x + pltpu.roll(x, span, axis=0)
            span *= 2
        cnt = jnp.minimum(w, pos + 1).astype(jnp.float32)
        d = (x[POOL_HALO:] / cnt - u).astype(jnp.bfloat16)
        outs.append(jnp.dot(d, w_ref[g], preferred_element_type=jnp.float32))
    o_ref[...] = (jnp.concatenate(outs, axis=1) * scale_ref[...]).astype(o_ref.dtype)


def pool_prompt(u, pool_w, pool_scale):
    B, S, P = u.shape
    return pl.pallas_call(
        _pool_kernel, grid=(B, S // POOL_TB),
        in_specs=[pl.BlockSpec((None, POOL_TB, P), lambda b, i: (b, i, 0)),
                  pl.BlockSpec((None, POOL_TB, P), lambda b, i: (b, jnp.maximum(i - 1, 0), 0)),
                  pl.BlockSpec((POOL_GROUPS, POOL_GROUP_WIDTH, POOL_GROUP_WIDTH), lambda b, i: (0, 0, 0)),
                  pl.BlockSpec((1, P), lambda b, i: (0, 0))],
        out_specs=pl.BlockSpec((None, POOL_TB, P), lambda b, i: (b, i, 0)),
        out_shape=jax.ShapeDtypeStruct((B, S, P), jnp.bfloat16),
        compiler_params=_params("parallel", "parallel"), name="pool_mix",
    )(u, u, pool_w, pool_scale.reshape(1, P))


def even_mix(split, B, T, pos, sinks, pool_w, pool_scale, rel_bias, cache):
    q, k, v, u = split(EVEN_SPLITS)
    if cache is None:
        a = window_prompt_mix(q, k, v, sinks, rel_bias)
        p = pool_prompt(u, pool_w.astype(jnp.bfloat16), pool_scale)
        new = (k[:, -WINDOW:].reshape(B, WINDOW, A_KV_HEADS, A_HEAD_DIM),
               v[:, -WINDOW:].reshape(B, WINDOW, A_KV_HEADS, A_HEAD_DIM), u[:, -POOL_HIST:])
        return jnp.concatenate([a, p], axis=-1), new
    q = q.reshape(B, T, A_KV_HEADS, A_GROUP, A_HEAD_DIM)
    k = k.reshape(B, T, A_KV_HEADS, A_HEAD_DIM)
    v = v.reshape(B, T, A_KV_HEADS, A_HEAD_DIM)
    k_cache, v_cache, u_cache = cache
    W = k_cache.shape[1]
    k_all = jnp.concatenate([k_cache, k], axis=1)
    v_all = jnp.concatenate([v_cache, v], axis=1)
    a = window_attention_sample(q, k_all, v_all, pos, W, sinks, rel_bias)
    u_all = jnp.concatenate([u_cache, u], axis=1)
    p = pool_mix(u_all, pos, POOL_HIST, pool_w, pool_scale)
    new = (k_all[:, -W:], v_all[:, -W:], u_all[:, -POOL_HIST:])
    return jnp.concatenate([a, p], axis=-1).astype(jnp.bfloat16), new


MLA_T = 512
MLA_QK = 256
MLA_V = 128
MLA_HP = 2


MLA_VW = 2 * MLA_V


def _mla_kernel(q_ref, k_ref, v_ref, o_ref, m_ref, acc_ref):
    i = pl.program_id(2)
    k = pl.program_id(3)

    @pl.when(k == 0)
    def _():
        m_ref[...] = jnp.full_like(m_ref, NEG_INF)
        acc_ref[...] = jnp.zeros_like(acc_ref)

    def step(diagonal):
        for hh in range(MLA_HP):
            qs = slice(hh * MLA_QK, (hh + 1) * MLA_QK)
            vs = slice(hh * MLA_VW, (hh + 1) * MLA_VW)
            s = lax.dot_general(q_ref[:, qs], k_ref[:, qs], (((1,), (1,)), ((), ())),
                                preferred_element_type=jnp.float32)
            if diagonal:
                row = lax.broadcasted_iota(jnp.int32, s.shape, 0) // CHUNK
                col = lax.broadcasted_iota(jnp.int32, s.shape, 1) // CHUNK
                s = jnp.where(col <= row, s, NEG_INF)
            m_prev = m_ref[hh]
            m_new = jnp.maximum(m_prev, s.max(axis=1, keepdims=True))
            alpha = jnp.exp(m_prev - m_new)
            p = jnp.exp(s - m_new)
            acc_ref[:, vs] = alpha * acc_ref[:, vs] + jnp.dot(p.astype(jnp.bfloat16), v_ref[:, vs],
                                                              preferred_element_type=jnp.float32)
            m_ref[hh] = m_new

    @pl.when(k < i)
    def _():
        step(False)

    @pl.when(k == i)
    def _():
        step(True)
        for hh in range(MLA_HP):
            a0 = hh * MLA_VW
            o_ref[:, hh * MLA_V:(hh + 1) * MLA_V] = (
                acc_ref[:, a0:a0 + MLA_V] / acc_ref[:, a0 + MLA_V:a0 + MLA_V + 1]).astype(o_ref.dtype)


def mla_prompt(qc, kc, v):
    B, S, _ = qc.shape
    n = S // MLA_T
    qk_w, v_w, o_w = MLA_HP * MLA_QK, MLA_HP * MLA_VW, MLA_HP * MLA_V
    return pl.pallas_call(
        _mla_kernel, grid=(B, C_HEADS // MLA_HP, n, n),
        in_specs=[pl.BlockSpec((None, MLA_T, qk_w), lambda b, h, i, k: (b, i, h)),
                  pl.BlockSpec((None, MLA_T, qk_w), lambda b, h, i, k: (b, jnp.minimum(k, i), h)),
                  pl.BlockSpec((None, MLA_T, v_w), lambda b, h, i, k: (b, jnp.minimum(k, i), h))],
        out_specs=pl.BlockSpec((None, MLA_T, o_w), lambda b, h, i, k: (b, i, h)),
        out_shape=jax.ShapeDtypeStruct((B, S, C_HEADS * MLA_V), jnp.bfloat16),
        scratch_shapes=[pltpu.VMEM((MLA_HP, MLA_T, 1), jnp.float32), pltpu.VMEM((MLA_T, v_w), jnp.float32)],
        compiler_params=_params("parallel", "parallel", "parallel", "arbitrary"), name="mla_prompt",
    )(qc, kc, v)


IDX_TQ = 128
IDX_TK = 256
LANES = 128
INT_MIN = -2 ** 31


def _indexer_kernel(q_ref, w_ref, ke_ref, ko_ref, mask_ref, key_ref, *, n_sel, tq, pos0, n_keys):
    i = pl.program_id(1)
    q0 = pos0 + i * tq
    n_lane_tiles = mask_ref.shape[0]
    sub = IDX_TK // LANES
    n_kb = jnp.minimum((q0 + tq + IDX_TK - 1) // IDX_TK, n_lane_tiles // sub)
    row_chunk = (q0 + lax.broadcasted_iota(jnp.int32, (tq, LANES), 0)) // CHUNK
    lane_iota = lax.broadcasted_iota(jnp.int32, (tq, LANES), 1)
    padded_keys = n_keys < n_lane_tiles * LANES

    def admissible(j):
        col = j * LANES + lane_iota
        adm = col // CHUNK <= row_chunk
        if padded_keys:
            adm = jnp.logical_and(adm, col < n_keys)
        return adm

    def score_tile(kb, carry):
        ke = ke_ref[kb]
        ko = ko_ref[kb]
        acc = jnp.zeros((tq, IDX_TK), jnp.float32)
        for p in range(IDX_HEADS // 2):
            qp = q_ref[:, p * LANES:(p + 1) * LANES]
            se = jnp.dot(qp, ke, preferred_element_type=jnp.float32)
            so = jnp.dot(qp, ko, preferred_element_type=jnp.float32)
            acc = acc + jnp.maximum(se, 0.0) * w_ref[:, 2 * p:2 * p + 1]
            acc = acc + jnp.maximum(so, 0.0) * w_ref[:, 2 * p + 1:2 * p + 2]
        for t in range(sub):
            j = kb * sub + t
            a = jnp.where(admissible(j), acc[:, t * LANES:(t + 1) * LANES], -jnp.inf)
            bits = pltpu.bitcast(a, jnp.int32)
            key_ref[j] = bits ^ ((bits >> 31) & jnp.int32(0x7FFFFFFF))
        return carry

    lax.fori_loop(0, n_kb, score_tile, 0)
    n_tiles = n_kb * sub

    def count_ge(cand):
        def body(kb, cnt):
            for t in range(sub):
                cnt = cnt + jnp.where(key_ref[kb * sub + t] >= cand, 1, 0)
            return cnt
        cnt = lax.fori_loop(0, n_kb, body, jnp.zeros((tq, LANES), jnp.int32))
        return cnt.sum(axis=1, keepdims=True)

    thr = jnp.where(count_ge(jnp.zeros((tq, 1), jnp.int32)) >= n_sel,
                    jnp.zeros((tq, 1), jnp.int32), jnp.full((tq, 1), INT_MIN, jnp.int32))

    def bit_step(b, thr):
        cand = thr | (jnp.int32(1) << (30 - b))
        return jnp.where(count_ge(cand) >= n_sel, cand, thr)

    thr = lax.fori_loop(0, 31, bit_step, thr)

    def write_mask(j, carry):
        sel = jnp.logical_and(key_ref[j] >= thr, admissible(j))
        mask_ref[j] = jnp.where(sel, 0.0, NEG_INF).astype(mask_ref.dtype)
        return carry

    lax.fori_loop(0, n_tiles, write_mask, 0)

    def write_rest(j, carry):
        mask_ref[j] = jnp.full((tq, LANES), NEG_INF, mask_ref.dtype)
        return carry

    lax.fori_loop(n_tiles, n_lane_tiles, write_rest, 0)


def indexer_mask(qi, wi, ke, ko, n_sel, *, tq, pos0, n_keys):
    B, T, _ = qi.shape
    nkb = ke.shape[1]
    L = nkb * IDX_TK
    return pl.pallas_call(
        functools.partial(_indexer_kernel, n_sel=n_sel, tq=tq, pos0=pos0, n_keys=n_keys), grid=(B, T // tq),
        in_specs=[pl.BlockSpec((None, tq, IDX_HEADS * IDX_DIM), lambda b, i: (b, i, 0)),
                  pl.BlockSpec((None, tq, IDX_HEADS), lambda b, i: (b, i, 0)),
                  pl.BlockSpec((None, nkb, LANES, IDX_TK), lambda b, i: (b, 0, 0, 0)),
                  pl.BlockSpec((None, nkb, LANES, IDX_TK), lambda b, i: (b, 0, 0, 0))],
        out_specs=pl.BlockSpec((None, L // LANES, tq, LANES), lambda b, i: (b, 0, i, 0)),
        out_shape=jax.ShapeDtypeStruct((B, L // LANES, T, LANES), jnp.bfloat16),
        scratch_shapes=[pltpu.VMEM((L // LANES, tq, LANES), jnp.int32)],
        compiler_params=_params("parallel", "parallel"), name="dsa_indexer",
    )(qi, wi, ke, ko)


def indexer_keys(ik, l_pad):
    B, L, _ = ik.shape
    ikT = jnp.pad(jnp.swapaxes(ik, 1, 2).astype(jnp.bfloat16), ((0, 0), (0, 0), (0, l_pad - L)))
    z = jnp.zeros_like(ikT)
    ke = jnp.concatenate([ikT, z], axis=1).reshape(B, LANES, l_pad // IDX_TK, IDX_TK).swapaxes(1, 2)
    ko = jnp.concatenate([z, ikT], axis=1).reshape(B, LANES, l_pad // IDX_TK, IDX_TK).swapaxes(1, 2)
    return ke, ko


DSA_TQ = 128
DSA_TK = 512
NEAR = 128


GROUP_ROWS = D_GROUP * DSA_TQ


DSA_VW = LANES


def _dsa_kernel(q_ref, kf_ref, vf_ref, mf_ref, k0_ref, v0_ref, m0_ref, k1_ref, v1_ref, m1_ref, bias_ref,
                o_ref, m_ref, acc_ref, *, n_far_max):
    i = pl.program_id(1)
    k = pl.program_id(2)
    q0 = i * DSA_TQ
    n_far = (i + 2) // 4

    @pl.when(k == 0)
    def _():
        m_ref[...] = jnp.full_like(m_ref, NEG_INF)
        acc_ref[...] = jnp.zeros_like(acc_ref)

    def update(g, s, v):
        m_prev = m_ref[g]
        m_new = jnp.maximum(m_prev, s.max(axis=1, keepdims=True))
        alpha = jnp.exp(m_prev - m_new)
        p = jnp.exp(s - m_new)
        acc_ref[g] = alpha * acc_ref[g] + jnp.dot(p.astype(jnp.bfloat16), v, preferred_element_type=jnp.float32)
        m_ref[g] = m_new

    def stack(mk):
        return jnp.concatenate([mk] * D_GROUP, axis=0)

    @pl.when(k < n_far)
    def _():
        col = k * DSA_TK + lax.broadcasted_iota(jnp.int32, (DSA_TQ, DSA_TK), 1)
        mk = jnp.concatenate([mf_ref[t] for t in range(DSA_TK // LANES)], axis=1).astype(jnp.float32)
        mk = stack(jnp.where(col < q0 - NEAR, mk, NEG_INF))

        def group(g, carry):
            s = jnp.dot(q_ref[g], kf_ref[g], preferred_element_type=jnp.float32) + mk
            update(g, s, vf_ref[g])
            return carry

        lax.fori_loop(0, D_KV_HEADS, group, 0, unroll=4)

    @pl.when(k == n_far_max)
    def _():
        mk0 = stack(m0_ref[...].astype(jnp.float32) + jnp.where(i > 0, 0.0, NEG_INF))
        mk1 = stack(m1_ref[...].astype(jnp.float32))

        def group(g, carry):
            q = q_ref[g]
            b = bias_ref[g]
            s0 = jnp.dot(q, k0_ref[g], preferred_element_type=jnp.float32) + (mk0 + b[:, :NEAR])
            update(g, s0, v0_ref[g])
            s1 = jnp.dot(q, k1_ref[g], preferred_element_type=jnp.float32) + (mk1 + b[:, NEAR:])
            update(g, s1, v1_ref[g])
            acc = acc_ref[g]
            o_ref[g] = (acc[:, :D_HEAD_DIM] / acc[:, D_HEAD_DIM:D_HEAD_DIM + 1]).astype(o_ref.dtype)
            return carry

        lax.fori_loop(0, D_KV_HEADS, group, 0, unroll=2)


def dsa_prompt(q, kT, v, mask, near_bias):
    B, _, nq, _, _ = q.shape
    S = nq * DSA_TQ
    n_far_max = S // DSA_TK

    def far_blk(i, k):
        return jnp.minimum(k, jnp.maximum((i + 2) // 4 - 1, 0))

    def prev_blk(i):
        return jnp.maximum(i - 1, 0)

    return pl.pallas_call(
        functools.partial(_dsa_kernel, n_far_max=n_far_max), grid=(B, nq, n_far_max + 1),
        in_specs=[pl.BlockSpec((None, D_KV_HEADS, None, GROUP_ROWS, D_HEAD_DIM), lambda b, i, k: (b, 0, i, 0, 0)),
                  pl.BlockSpec((None, D_KV_HEADS, D_HEAD_DIM, DSA_TK), lambda b, i, k: (b, 0, 0, far_blk(i, k))),
                  pl.BlockSpec((None, D_KV_HEADS, DSA_TK, DSA_VW), lambda b, i, k: (b, 0, far_blk(i, k), 0)),
                  pl.BlockSpec((None, DSA_TK // LANES, DSA_TQ, LANES), lambda b, i, k: (b, far_blk(i, k), i, 0)),
                  pl.BlockSpec((None, D_KV_HEADS, D_HEAD_DIM, NEAR), lambda b, i, k: (b, 0, 0, prev_blk(i))),
                  pl.BlockSpec((None, D_KV_HEADS, NEAR, DSA_VW), lambda b, i, k: (b, 0, prev_blk(i), 0)),
                  pl.BlockSpec((None, None, DSA_TQ, LANES), lambda b, i, k: (b, prev_blk(i), i, 0)),
                  pl.BlockSpec((None, D_KV_HEADS, D_HEAD_DIM, NEAR), lambda b, i, k: (b, 0, 0, i)),
                  pl.BlockSpec((None, D_KV_HEADS, NEAR, DSA_VW), lambda b, i, k: (b, 0, i, 0)),
                  pl.BlockSpec((None, None, DSA_TQ, LANES), lambda b, i, k: (b, i, i, 0)),
                  pl.BlockSpec((D_KV_HEADS, GROUP_ROWS, 2 * NEAR), lambda b, i, k: (0, 0, 0))],
        out_specs=pl.BlockSpec((None, D_KV_HEADS, None, GROUP_ROWS, D_HEAD_DIM), lambda b, i, k: (b, 0, i, 0, 0)),
        out_shape=jax.ShapeDtypeStruct((B, D_KV_HEADS, nq, GROUP_ROWS, D_HEAD_DIM), jnp.bfloat16),
        scratch_shapes=[pltpu.VMEM((D_KV_HEADS, GROUP_ROWS, 1), jnp.float32),
                        pltpu.VMEM((D_KV_HEADS, GROUP_ROWS, DSA_VW), jnp.float32)],
        compiler_params=_params("parallel", "parallel", "arbitrary"), name="dsa_attention",
    )(q, kT, v, mask, kT, v, mask, kT, v, mask, near_bias)


def with_ones_column(v, width):
    ones = jnp.ones(v.shape[:-1] + (1,), v.dtype)
    zeros = jnp.zeros(v.shape[:-1] + (width - v.shape[-1] - 1,), v.dtype)
    return jnp.concatenate([v, ones, zeros], axis=-1)


def dsa_prompt_mix(dq, dk, dv, iq, ik, iw, rel_bias, n_sel):
    B, S, _ = dq.shape
    bf = jnp.bfloat16
    ke, ko = indexer_keys(ik, S)
    w = iw * (IDX_HEADS ** -0.5) * (IDX_DIM ** -0.5)
    mask = indexer_mask(iq.astype(bf), w, ke, ko, n_sel, tq=IDX_TQ, pos0=0, n_keys=S)
    nq = S // DSA_TQ
    q = (dq * D_HEAD_DIM ** -0.5).astype(bf).reshape(B, nq, DSA_TQ, D_KV_HEADS, D_GROUP, D_HEAD_DIM)
    q = q.transpose(0, 3, 1, 4, 2, 5).reshape(B, D_KV_HEADS, nq, GROUP_ROWS, D_HEAD_DIM)
    kT = dk.astype(bf).reshape(B, S, D_KV_HEADS, D_HEAD_DIM).transpose(0, 2, 3, 1)
    v = with_ones_column(dv.astype(bf).reshape(B, S, D_KV_HEADS, D_HEAD_DIM), DSA_VW).transpose(0, 2, 1, 3)
    rel = (jnp.arange(2 * NEAR)[None, :] - NEAR) - jnp.arange(DSA_TQ)[:, None]
    far_bucket = NUM_BUCKETS // 2 - 1
    near_bias = (rel_bias[t5_bucket(rel)] - rel_bias[far_bucket]).transpose(2, 0, 1)
    o = dsa_prompt(q, kT, v, mask, near_bias.reshape(D_KV_HEADS, GROUP_ROWS, 2 * NEAR))
    o = o.reshape(B, D_KV_HEADS, nq, D_GROUP, DSA_TQ, D_HEAD_DIM).transpose(0, 2, 4, 1, 3, 5)
    return o.reshape(B, S, D_HEADS * D_HEAD_DIM)


SAMPLE_KEYS = PAST_LEN + DEC_SEQ
SAMPLE_KEYS_PAD = -(-SAMPLE_KEYS // IDX_TK) * IDX_TK
assert (SAMPLE_KEYS - 1) // CHUNK == PAST_LEN // CHUNK


def _dsa_sample_kernel(q_ref, kT_ref, v_ref, mask_ref, bias_ref, o_ref):
    mk = jnp.concatenate([mask_ref[t] for t in range(mask_ref.shape[0])], axis=1).astype(jnp.float32)
    mk = jnp.concatenate([mk] * D_GROUP, axis=0)

    def group(g, carry):
        s = jnp.dot(q_ref[g], kT_ref[g], preferred_element_type=jnp.float32) + (mk + bias_ref[g])
        p = jnp.exp(s - s.max(axis=1, keepdims=True))
        o = jnp.dot(p.astype(jnp.bfloat16), v_ref[g], preferred_element_type=jnp.float32)
        o_ref[g] = (o / p.sum(axis=1, keepdims=True)).astype(o_ref.dtype)
        return carry

    lax.fori_loop(0, D_KV_HEADS, group, 0, unroll=2)


def dsa_sample(q, kT, v, mask, bias):
    B, _, rows, _ = q.shape
    lp = kT.shape[-1]
    T = rows // D_GROUP
    return pl.pallas_call(
        _dsa_sample_kernel, grid=(B,),
        in_specs=[pl.BlockSpec((None, D_KV_HEADS, rows, D_HEAD_DIM), lambda b: (b, 0, 0, 0)),
                  pl.BlockSpec((None, D_KV_HEADS, D_HEAD_DIM, lp), lambda b: (b, 0, 0, 0)),
                  pl.BlockSpec((None, D_KV_HEADS, lp, D_HEAD_DIM), lambda b: (b, 0, 0, 0)),
                  pl.BlockSpec((None, lp // LANES, T, LANES), lambda b: (b, 0, 0, 0)),
                  pl.BlockSpec((D_KV_HEADS, rows, lp), lambda b: (0, 0, 0))],
        out_specs=pl.BlockSpec((None, D_KV_HEADS, rows, D_HEAD_DIM), lambda b: (b, 0, 0, 0)),
        out_shape=jax.ShapeDtypeStruct((B, D_KV_HEADS, rows, D_HEAD_DIM), jnp.bfloat16),
        compiler_params=_params("parallel"), name="dsa_sample",
    )(q, kT, v, mask, bias)


def _mla_sample_kernel(q_ref, k_ref, v_ref, o_ref, *, n_keys):
    for hh in range(MLA_HP):
        qs = slice(hh * MLA_QK, (hh + 1) * MLA_QK)
        vs = slice(hh * MLA_V, (hh + 1) * MLA_V)
        s = lax.dot_general(q_ref[:, qs], k_ref[:, qs], (((1,), (1,)), ((), ())),
                            preferred_element_type=jnp.float32)
        col = lax.broadcasted_iota(jnp.int32, s.shape, 1)
        s = jnp.where(col < n_keys, s, NEG_INF)
        p = jnp.exp(s - s.max(axis=1, keepdims=True))
        o = jnp.dot(p.astype(jnp.bfloat16), v_ref[:, vs], preferred_element_type=jnp.float32)
        o_ref[:, vs] = (o / p.sum(axis=1, keepdims=True)).astype(o_ref.dtype)


def mla_sample(qc, kc, v, n_keys):
    B, T, _ = qc.shape
    lp = kc.shape[1]
    qk_w, v_w = MLA_HP * MLA_QK, MLA_HP * MLA_V
    return pl.pallas_call(
        functools.partial(_mla_sample_kernel, n_keys=n_keys), grid=(B, C_HEADS // MLA_HP),
        in_specs=[pl.BlockSpec((None, T, qk_w), lambda b, h: (b, 0, h)),
                  pl.BlockSpec((None, lp, qk_w), lambda b, h: (b, 0, h)),
                  pl.BlockSpec((None, lp, v_w), lambda b, h: (b, 0, h))],
        out_specs=pl.BlockSpec((None, T, v_w), lambda b, h: (b, 0, h)),
        out_shape=jax.ShapeDtypeStruct((B, T, C_HEADS * MLA_V), jnp.bfloat16),
        compiler_params=_params("parallel", "parallel"), name="mla_sample",
    )(qc, kc, v)


def dsa_sample_mix(dq, dk_all, dv_all, iq, ik_all, iw, rel_bias, n_sel):
    B, T, _ = dq.shape
    L = dk_all.shape[1]
    lp = SAMPLE_KEYS_PAD
    bf = jnp.bfloat16
    ke, ko = indexer_keys(ik_all, lp)
    w = iw * (IDX_HEADS ** -0.5) * (IDX_DIM ** -0.5)
    mask = indexer_mask(iq.astype(bf), w, ke, ko, n_sel, tq=T, pos0=L - T, n_keys=L)
    q = (dq * D_HEAD_DIM ** -0.5).astype(bf).reshape(B, T, D_KV_HEADS, D_GROUP, D_HEAD_DIM)
    q = q.transpose(0, 2, 3, 1, 4).reshape(B, D_KV_HEADS, D_GROUP * T, D_HEAD_DIM)
    kp = jnp.pad(dk_all.astype(bf), ((0, 0), (0, lp - L), (0, 0))).reshape(B, lp, D_KV_HEADS, D_HEAD_DIM)
    vp = jnp.pad(dv_all.astype(bf), ((0, 0), (0, lp - L), (0, 0))).reshape(B, lp, D_KV_HEADS, D_HEAD_DIM)
    rel = jnp.arange(lp)[None, :] - (L - T + jnp.arange(T))[:, None]
    bias = rel_bias[t5_bucket(rel)].transpose(2, 0, 1).reshape(D_KV_HEADS, D_GROUP * T, lp)
    o = dsa_sample(q, kp.transpose(0, 2, 3, 1), vp.transpose(0, 2, 1, 3), mask, bias)
    o = o.reshape(B, D_KV_HEADS, D_GROUP, T, D_HEAD_DIM).transpose(0, 3, 1, 2, 4)
    return o.reshape(B, T, D_HEADS * D_HEAD_DIM)


def odd_mix(split, B, T, pos, g_qa, g_kva, w_qb, w_kvb, rel_bias, cache):
    q_lat, kv_lat, kpe_raw, dq, dk, dv, iq, ik, iw = odd_split(split)
    qn = rmsnorm(q_lat, g_qa).reshape(B * T, Q_LORA)
    q = matmul(qn, w_qb, tm=min(ROW_BLOCK, B * T)).reshape(B, T, C_HEADS, NOPE_DIM + ROPE_DIM)
    q_nope = q[..., :NOPE_DIM]
    q_pe = apply_rope(q[..., NOPE_DIM:], pos)
    lat = rmsnorm(kv_lat, g_kva)
    kpe = apply_rope(kpe_raw, pos)
    bf = jnp.bfloat16
    new = (lat, kpe, dk.reshape(B, T, D_KV_HEADS, D_HEAD_DIM), dv.reshape(B, T, D_KV_HEADS, D_HEAD_DIM), ik)
    qpad = jnp.zeros((B, T, C_HEADS, MLA_QK - NOPE_DIM - ROPE_DIM), jnp.float32)
    qc = (jnp.concatenate([q_nope, q_pe, qpad], axis=-1) * (NOPE_DIM + ROPE_DIM) ** -0.5).astype(bf)
    qc = qc.reshape(B, T, C_HEADS * MLA_QK)
    if cache is None:
        lat_all, kpe_all, dk_all, dv_all, ik_all = lat, kpe, dk, dv, ik
    else:
        c_lat, c_kpe, c_dk, c_dv, c_ik = cache
        lat_all = jnp.concatenate([c_lat, lat], axis=1)
        kpe_all = jnp.concatenate([c_kpe, kpe], axis=1)
        dk_all = jnp.concatenate([c_dk.reshape(B, -1, D_KV_HEADS * D_HEAD_DIM), dk], axis=1)
        dv_all = jnp.concatenate([c_dv.reshape(B, -1, D_KV_HEADS * D_HEAD_DIM), dv], axis=1)
        ik_all = jnp.concatenate([c_ik, ik], axis=1)
    L = lat_all.shape[1]
    kv = matmul(lat_all.reshape(B * L, KV_LORA), w_kvb).reshape(B, L, C_HEADS, NOPE_DIM + V_DIM)
    k_nope, v_c = kv[..., :NOPE_DIM], kv[..., NOPE_DIM:]
    kpad = jnp.zeros((B, L, C_HEADS, MLA_QK - NOPE_DIM - ROPE_DIM), jnp.float32)
    kc = jnp.concatenate([k_nope, jnp.broadcast_to(kpe_all[:, :, None, :], (B, L, C_HEADS, ROPE_DIM)), kpad],
                         axis=-1).astype(bf).reshape(B, L, C_HEADS * MLA_QK)
    vc = v_c.astype(bf).reshape(B, L, C_HEADS * V_DIM)
    n_sel = min(TOPK_MAX, L // 4)
    if cache is None:
        oc = mla_prompt(qc, kc, with_ones_column(v_c.astype(bf), MLA_VW).reshape(B, L, C_HEADS * MLA_VW))
        od = dsa_prompt_mix(dq, dk, dv, iq, ik, iw, rel_bias, n_sel)
    else:
        rows = ((0, 0), (0, SAMPLE_KEYS_PAD - L), (0, 0))
        oc = mla_sample(qc, jnp.pad(kc, rows), jnp.pad(vc, rows), L)
        od = dsa_sample_mix(dq, dk_all, dv_all, iq, ik_all, iw, rel_bias, n_sel)
    return jnp.concatenate([oc, od], axis=-1), new


def _final_norm_kernel(x_ref, g_ref, o_ref):
    x = x_ref[...]
    o_ref[...] = x * lax.rsqrt(jnp.mean(x * x, axis=-1, keepdims=True) + RMS_EPS) * g_ref[...]


def final_norm(x, g):
    row = pl.BlockSpec((ROW_BLOCK, D_MODEL), lambda i: (i, 0))
    return pl.pallas_call(
        _final_norm_kernel, grid=(N_ROW_BLOCKS,),
        in_specs=[row, pl.BlockSpec((1, D_MODEL), lambda i: (0, 0))], out_specs=row,
        out_shape=jax.ShapeDtypeStruct((N_TOK, D_MODEL), jnp.float32),
        compiler_params=_params("parallel"), name="final_norm")(x, g.reshape(1, D_MODEL))


def kernel(x_prompt, x_sample, c_prompt, c_sample, cache_a_k, cache_a_v, state_b_pool, cache_c_latent, cache_c_kpe, cache_d_k, cache_d_v, cache_d_idx, rel_bias, ada_mix_w, ada_mix_b, ada_ffn_w, ada_ffn_b, norm_mix, norm_ffn, norm_final, even_w_in, even_w_out, a_sinks, pool_w, pool_scale, odd_w_in, odd_w_out, c_q_norm, c_kv_norm, c_w_qb, c_w_kvb, moe_router_group, moe_router_expert, moe_w_gate, moe_w_up, moe_w_down):
    bf = jnp.bfloat16
    x = jnp.concatenate([x_prompt.reshape(N_PROMPT, D_MODEL), x_sample.reshape(N_SAMPLE, D_MODEL)], axis=0)
    c = jnp.concatenate([c_prompt, c_sample], axis=0)
    mod_mix = modulation_all(c, ada_mix_w, ada_mix_b)
    mod_ffn = modulation_all(c, ada_ffn_w, ada_ffn_b)
    pos_p = jnp.arange(SEQ)
    pos_s = PAST_LEN + jnp.arange(DEC_SEQ)
    w_router = jnp.concatenate(
        [moe_router_group, moe_router_expert,
         jnp.zeros((DEPTH, D_MODEL, ROUTER_PAD - N_GROUPS - N_EXPERTS), jnp.float32)], axis=-1)
    new_p = [[] for _ in range(8)]
    new_s = [[] for _ in range(8)]
    for l in range(DEPTH):
        j = l // 2
        shift, scale, gate = jnp.split(mod_mix[l], 3, axis=-1)
        h = norm_mod(x, norm_mix[l], scale, shift)
        if l % 2 == 0:
            z = matmul(h, even_w_in[j].astype(bf))
            zp = token_rows(z, 0, BATCH, SEQ)
            zs = token_rows(z, N_PROMPT, DEC_BATCH, DEC_SEQ)
            mp, st_p = even_mix(zp, BATCH, SEQ, pos_p, a_sinks[j], pool_w[j], pool_scale[j], rel_bias, None)
            ms, st_s = even_mix(zs, DEC_BATCH, DEC_SEQ, pos_s, a_sinks[j], pool_w[j], pool_scale[j], rel_bias,
                                (cache_a_k[j], cache_a_v[j], state_b_pool[j]))
            off = 0
            w_out = even_w_out[j]
        else:
            z = matmul(h, odd_in_weight(odd_w_in[j]), tn=ODD_IN_PAD // 9)
            zp = token_rows(z, 0, BATCH, SEQ)
            zs = token_rows(z, N_PROMPT, DEC_BATCH, DEC_SEQ)
            wq, wkv = c_w_qb[j].astype(bf), c_w_kvb[j].astype(bf)
            mp, st_p = odd_mix(zp, BATCH, SEQ, pos_p, c_q_norm[j], c_kv_norm[j], wq, wkv, rel_bias, None)
            ms, st_s = odd_mix(zs, DEC_BATCH, DEC_SEQ, pos_s, c_q_norm[j], c_kv_norm[j], wq, wkv, rel_bias,
                               (cache_c_latent[j], cache_c_kpe[j], cache_d_k[j], cache_d_v[j], cache_d_idx[j]))
            off = 3
            w_out = odd_w_out[j]
        for i, s in enumerate(st_p):
            new_p[off + i].append(s)
        for i, s in enumerate(st_s):
            new_s[off + i].append(s)
        mix = jnp.concatenate([mp.reshape(N_PROMPT, -1), ms.reshape(N_SAMPLE, -1)], axis=0).astype(bf)
        x = matmul_resid(mix, w_out.astype(bf), x, gate)
        shift, scale, gate = jnp.split(mod_ffn[l], 3, axis=-1)
        h, logits = norm_mod(x, norm_ffn[l], scale, shift, w_router[l])
        x = moe_ffn(x, h, logits, gate, moe_w_gate[l].astype(bf), moe_w_up[l].astype(bf), moe_w_down[l].astype(bf))
    y = final_norm(x, norm_final)
    st_p = [jnp.stack(s) for s in new_p]
    st_s = [jnp.stack(s) for s in new_s]
    a_k_p, a_v_p, pool_p, lat_p, kpe_p, dk_p, dv_p, didx_p = st_p
    a_k_s, a_v_s, pool_s, lat_s, kpe_s, dk_s, dv_s, didx_s = st_s
    return (y[:N_PROMPT].reshape(BATCH, SEQ, D_MODEL), y[N_PROMPT:].reshape(DEC_BATCH, DEC_SEQ, D_MODEL),
            a_k_p, a_k_s, a_v_p, a_v_s, pool_p, pool_s, lat_p, lat_s,
            kpe_p, kpe_s, dk_p, dk_s, dv_p, dv_s, didx_p, didx_s)
```
